```python
import jax, jax.numpy as jnp
from jax import lax
import numpy as np

D_MODEL = 1024
BATCH = 16
SEQ = 4096
DEPTH = 4

MIX_WIDTH = D_MODEL
N_HEADS = 8
QK_NOPE_DIM = 64
QK_ROPE_DIM = 32
V_HEAD_DIM = 64
Q_LORA_RANK = 384
KV_LORA_RANK = 256
ATTN_WIDTH = N_HEADS * V_HEAD_DIM
ROPE_THETA = 10000.0
Q_BLOCK = 128
CONV_WIDTH = MIX_WIDTH - ATTN_WIDTH
CONV_GROUPS = 8
CONV_K = 3
PROJ_SPLITS = (Q_LORA_RANK, KV_LORA_RANK, QK_ROPE_DIM, CONV_WIDTH, CONV_WIDTH, CONV_WIDTH)
PROJ_WIDTH = sum(PROJ_SPLITS)
D_FF = 2816
FFN_RESIDUAL = 0.5
N_MOD = 9
EPS = 1e-6

kernel_name = "hybrid_mla_shortconv_macaron_adaln"


def rms_norm(x, g):
    xf = x.astype(jnp.float32)
    y = xf * lax.rsqrt(jnp.mean(xf * xf, axis=-1, keepdims=True) + EPS)
    return (y * g.astype(jnp.float32)).astype(x.dtype)


def modulate(h, shift, scale):
    return h * (1 + scale[:, None, :]) + shift[:, None, :]


def swiglu(h, w_gu, w_down):
    g, u = jnp.split(h @ w_gu, 2, axis=-1)
    return (jax.nn.silu(g) * u) @ w_down


def rope_tables(positions):
    inv_freq = 1.0 / (ROPE_THETA ** (jnp.arange(0, QK_ROPE_DIM, 2, dtype=jnp.float32) / QK_ROPE_DIM))
    ang = positions.astype(jnp.float32)[..., None] * inv_freq
    return jnp.cos(ang), jnp.sin(ang)


def apply_rope(x, cos, sin):
    x1, x2 = jnp.split(x.astype(jnp.float32), 2, axis=-1)
    c, s = cos[:, :, None, :], sin[:, :, None, :]
    return jnp.concatenate([x1 * c - x2 * s, x2 * c + x1 * s], axis=-1).astype(x.dtype)


def causal_attention(q, k, v):
    S = q.shape[1]
    scale = (QK_NOPE_DIM + QK_ROPE_DIM) ** -0.5
    outs = []
    for i in range(S // Q_BLOCK):
        kv_len = (i + 1) * Q_BLOCK
        qi = q[:, i * Q_BLOCK:kv_len]
        s = jnp.einsum('bqhd,bkhd->bhqk', qi, k[:, :kv_len]).astype(jnp.float32) * scale
        q_pos = i * Q_BLOCK + jnp.arange(Q_BLOCK)
        mask = jnp.arange(kv_len)[None, :] <= q_pos[:, None]
        p = jax.nn.softmax(jnp.where(mask, s, -jnp.inf), axis=-1)
        outs.append(jnp.einsum('bhqk,bkhd->bqhd', p.astype(v.dtype), v[:, :kv_len]))
    return jnp.concatenate(outs, axis=1)


def causal_short_conv(u, w):
    S = u.shape[1]
    up = jnp.pad(u, ((0, 0), (CONV_K - 1, 0), (0, 0)))
    y = w[0] * up[:, 0:S]
    for j in range(1, CONV_K):
        y = y + w[j] * up[:, j:j + S]
    return y


def hybrid_mixer(h, cos, sin, w_in, q_a_norm, w_uq, kv_a_norm, w_ukv, conv_w,
                 attn_out_norm, conv_out_norm, w_o):
    B, S, _ = h.shape
    proj = h @ w_in
    offs = np.cumsum(PROJ_SPLITS)[:-1].tolist()
    c_q, c_kv, k_pe, gate_b, gate_c, val = jnp.split(proj, offs, axis=-1)
    q = (rms_norm(c_q, q_a_norm) @ w_uq).reshape(B, S, N_HEADS, QK_NOPE_DIM + QK_ROPE_DIM)
    q_nope, q_pe = jnp.split(q, [QK_NOPE_DIM], axis=-1)
    q_pe = apply_rope(q_pe, cos, sin)
    kv = (rms_norm(c_kv, kv_a_norm) @ w_ukv).reshape(B, S, N_HEADS, QK_NOPE_DIM + V_HEAD_DIM)
    k_nope, v = jnp.split(kv, [QK_NOPE_DIM], axis=-1)
    k_pe = apply_rope(k_pe[:, :, None, :], cos, sin)
    q_full = jnp.concatenate([q_nope, q_pe], axis=-1)
    k_full = jnp.concatenate([k_nope, jnp.broadcast_to(k_pe, (B, S, N_HEADS, QK_ROPE_DIM))], axis=-1)
    attn = causal_attention(q_full, k_full, v).reshape(B, S, ATTN_WIDTH)
    conv = gate_b * causal_short_conv(gate_c * val, conv_w)
    merged = jnp.concatenate([rms_norm(attn, attn_out_norm), rms_norm(conv, conv_out_norm)], axis=-1)
    return merged @ w_o


def setup_inputs(seed: int = 0) -> dict:
    key = jax.random.key(seed)
    ks = iter(jax.random.split(key, 32))
    L, D = DEPTH, D_MODEL

    def nrm(shape, scale):
        return jax.random.normal(next(ks), shape, jnp.float32) * scale

    def gain(shape):
        return 1.0 + nrm(shape, 0.05)

    x = nrm((BATCH, SEQ, D), 1.0)
    c = nrm((BATCH, D), 1.0)
    offset = jax.random.randint(next(ks), (BATCH, 1), 0, 4096, dtype=jnp.int32)
    positions = offset + jnp.arange(SEQ, dtype=jnp.int32)[None, :]
    return {
        "x": x, "c": c, "positions": positions,
        "w_ada": nrm((L, D, N_MOD * D), 0.3 * D ** -0.5),
        "b_ada": nrm((L, N_MOD * D), 0.01),
        "ffn1_norm": gain((L, D)),
        "ffn1_w_gu": nrm((L, D, 2 * D_FF), D ** -0.5),
        "ffn1_w_down": nrm((L, D_FF, D), D_FF ** -0.5),
        "mix_norm": gain((L, D)),
        "w_in": nrm((L, D, PROJ_WIDTH), D ** -0.5),
        "q_a_norm": gain((L, Q_LORA_RANK)),
        "w_uq": nrm((L, Q_LORA_RANK, N_HEADS * (QK_NOPE_DIM + QK_ROPE_DIM)), Q_LORA_RANK ** -0.5),
        "kv_a_norm": gain((L, KV_LORA_RANK)),
        "w_ukv": nrm((L, KV_LORA_RANK, N_HEADS * (QK_NOPE_DIM + V_HEAD_DIM)), KV_LORA_RANK ** -0.5),
        "conv_w": nrm((L, CONV_K, CONV_WIDTH), CONV_K ** -0.5),
        "attn_out_norm": gain((L, ATTN_WIDTH)),
        "conv_out_norm": gain((L, CONV_WIDTH)),
        "w_o": nrm((L, MIX_WIDTH, D), MIX_WIDTH ** -0.5),
        "ffn2_norm": gain((L, D)),
        "ffn2_w_gu": nrm((L, D, 2 * D_FF), D ** -0.5),
        "ffn2_w_down": nrm((L, D_FF, D), D_FF ** -0.5),
        "final_norm": gain((D,)),
    }


def reference(x, c, positions, w_ada, b_ada, ffn1_norm, ffn1_w_gu, ffn1_w_down,
              mix_norm, w_in, q_a_norm, w_uq, kv_a_norm, w_ukv, conv_w,
              attn_out_norm, conv_out_norm, w_o, ffn2_norm, ffn2_w_gu, ffn2_w_down,
              final_norm):
    cos, sin = rope_tables(positions)
    c_act = jax.nn.silu(c)
    for l in range(DEPTH):
        mod = jnp.split(c_act @ w_ada[l] + b_ada[l], N_MOD, axis=-1)
        h = modulate(rms_norm(x, ffn1_norm[l]), mod[0], mod[1])
        x = x + FFN_RESIDUAL * (1 + mod[2])[:, None, :] * swiglu(h, ffn1_w_gu[l], ffn1_w_down[l])
        h = modulate(rms_norm(x, mix_norm[l]), mod[3], mod[4])
        y = hybrid_mixer(h, cos, sin, w_in[l], q_a_norm[l], w_uq[l], kv_a_norm[l], w_ukv[l],
                         conv_w[l], attn_out_norm[l], conv_out_norm[l], w_o[l])
        x = x + (1 + mod[5])[:, None, :] * y
        h = modulate(rms_norm(x, ffn2_norm[l]), mod[6], mod[7])
        x = x + FFN_RESIDUAL * (1 + mod[8])[:, None, :] * swiglu(h, ffn2_w_gu[l], ffn2_w_down[l])
    return rms_norm(x, final_norm)
```

```python
import functools

import jax
import jax.numpy as jnp
from jax import lax
from jax.experimental import pallas as pl
from jax.experimental.pallas import tpu as pltpu

F32 = jnp.float32
BF16 = jnp.bfloat16

D_MODEL = 1024
N_HEADS = 8
QK_NOPE = 64
QK_ROPE = 32
V_DIM = 64
Q_RANK = 384
KV_RANK = 256
CONV_W = 512
ATTN_W = N_HEADS * V_DIM
D_FF = 2816
N_MOD = 9
EPS = 1e-6
ROPE_THETA = 10000.0
HEAD_SLAB = 128
HALF_ROPE = QK_ROPE // 2
P_CQ = 0
P_CKV = P_CQ + Q_RANK
P_KPE = P_CKV + KV_RANK
P_KPE_SW = P_KPE + HEAD_SLAB
P_GB = P_KPE_SW + HEAD_SLAB
P_GC = P_GB + CONV_W
P_VAL = P_GC + CONV_W
P_WIDTH = P_VAL + CONV_W
HALO = 8
Q_SCALE = float((QK_NOPE + QK_ROPE) ** -0.5 * 1.4426950408889634)
VMEM_LIMIT = 56 * 1024 * 1024


def _rms(x, g):
    return x * lax.rsqrt(jnp.mean(x * x, axis=-1, keepdims=True) + EPS) * g


def _silu(x):
    return x / (1.0 + jnp.exp(-x))


def _params(n_axes):
    return pltpu.CompilerParams(dimension_semantics=("arbitrary",) * n_axes,
                                vmem_limit_bytes=VMEM_LIMIT)


def _resident(shape, index_map):
    return pl.BlockSpec(shape, index_map, pipeline_mode=pl.Buffered(1))


def _row_spec(width, layer):
    return pl.BlockSpec((None, 1, width), lambda *_: (layer, 0, 0))


def _mod_spec(layer):
    return pl.BlockSpec((None, 1, N_MOD, D_MODEL), lambda b, *_: (layer, b, 0, 0))


def _mod_kernel(c_ref, w_ref, b_ref, o_ref):
    c = c_ref[...]
    ca = _silu(c).astype(BF16)
    o_ref[0] = jnp.dot(ca, w_ref[0].astype(BF16), preferred_element_type=F32) + b_ref[0]


def _modulation(c, w_ada, b_ada):
    L, D, N = w_ada.shape
    B = c.shape[0]
    tn = N // 8
    out = pl.pallas_call(
        _mod_kernel,
        grid=(L, N // tn),
        in_specs=[pl.BlockSpec((B, D), lambda l, n: (0, 0)),
                  pl.BlockSpec((1, D, tn), lambda l, n: (l, 0, n)),
                  pl.BlockSpec((1, 1, tn), lambda l, n: (l, 0, n))],
        out_specs=pl.BlockSpec((1, B, tn), lambda l, n: (l, 0, n)),
        out_shape=jax.ShapeDtypeStruct((L, B, N), F32),
        compiler_params=_params(2),
        name="adaln_mod",
    )(c, w_ada, b_ada.reshape(L, 1, N))
    return out.reshape(L, B, N_MOD, D)


def _rope_kernel(pos_ref, invf_ref, sgn_ref, cos_ref, sin_ref):
    ang = pos_ref[0].astype(F32) * invf_ref[...]
    cos_ref[0] = jnp.cos(ang)
    sin_ref[0] = jnp.sin(ang) * sgn_ref[...]


def _rope_tables(positions):
    B, S = positions.shape
    ts = min(S, 512)
    inv_freq = 1.0 / (ROPE_THETA ** (jnp.arange(0, QK_ROPE, 2, dtype=F32) / QK_ROPE))
    zeros_nope = jnp.zeros((QK_NOPE,), F32)
    zeros_pad = jnp.zeros((HEAD_SLAB - QK_NOPE - QK_ROPE,), F32)
    invf = jnp.concatenate([zeros_nope, inv_freq, inv_freq, zeros_pad])[None, :]
    ones = jnp.ones((HALF_ROPE,), F32)
    sgn = jnp.concatenate([zeros_nope, -ones, ones, zeros_pad])[None, :]
    tab = jax.ShapeDtypeStruct((B, S, HEAD_SLAB), F32)
    return pl.pallas_call(
        _rope_kernel,
        grid=(B, S // ts),
        in_specs=[pl.BlockSpec((1, ts, 1), lambda b, i: (b, i, 0)),
                  pl.BlockSpec((1, HEAD_SLAB), lambda b, i: (0, 0)),
                  pl.BlockSpec((1, HEAD_SLAB), lambda b, i: (0, 0))],
        out_specs=[pl.BlockSpec((1, ts, HEAD_SLAB), lambda b, i: (b, i, 0)),
                   pl.BlockSpec((1, ts, HEAD_SLAB), lambda b, i: (b, i, 0))],
        out_shape=[tab, tab],
        compiler_params=_params(2),
        name="rope_tables",
    )(positions.reshape(B, S, 1), invf, sgn)


def _ffn_kernel(x_ref, mod_ref, g_ref, wgu_ref, wd_ref, *rest, mod_base, final):
    if final:
        fg_ref, o_ref = rest
    else:
        (o_ref,) = rest
    x = x_ref[0]
    shift = mod_ref[0, mod_base:mod_base + 1, :]
    scale = mod_ref[0, mod_base + 1:mod_base + 2, :]
    gate = mod_ref[0, mod_base + 2:mod_base + 3, :]
    h = (_rms(x, g_ref[...]) * (1.0 + scale) + shift).astype(BF16)
    gu = jnp.dot(h, wgu_ref[...], preferred_element_type=F32)
    a = (_silu(gu[:, :D_FF]) * gu[:, D_FF:]).astype(BF16)
    y = jnp.dot(a, wd_ref[...], preferred_element_type=F32)
    out = x + (0.5 * (1.0 + gate)) * y
    if final:
        out = _rms(out, fg_ref[...])
    o_ref[0] = out


def _ffn(x, mod_l, g, wgu, wd, layer, mod_base, final_g, tm):
    B, S, D = x.shape
    final = final_g is not None
    in_specs = [pl.BlockSpec((1, tm, D), lambda b, i: (b, i, 0)),
                _mod_spec(layer),
                _row_spec(D, layer),
                _resident((None, D, 2 * D_FF), lambda b, i: (layer, 0, 0)),
                _resident((None, D_FF, D), lambda b, i: (layer, 0, 0))]
    args = [x, mod_l, g, wgu, wd]
    if final:
        in_specs.append(pl.BlockSpec((1, D), lambda b, i: (0, 0)))
        args.append(final_g)
    return pl.pallas_call(
        functools.partial(_ffn_kernel, mod_base=mod_base, final=final),
        grid=(B, S // tm),
        in_specs=in_specs,
        out_specs=pl.BlockSpec((1, tm, D), lambda b, i: (b, i, 0)),
        out_shape=jax.ShapeDtypeStruct((B, S, D), F32),
        compiler_params=_params(2),
        name="ffn_final" if final else "ffn",
    )(*args)


def _proj_kernel(x_ref, mod_ref, g_ref, win_ref, qg_ref, wuq_ref, kvg_ref, wk_ref, wv_ref,
                 cw_ref, cg_ref, cos_ref, sin_ref,
                 q_ref, k_ref, v_ref, conv_ref, hist_ref, *, tm):
    i = pl.program_id(1)
    x = x_ref[0]
    shift = mod_ref[0, 3:4, :]
    scale = mod_ref[0, 4:5, :]
    h = (_rms(x, g_ref[...]) * (1.0 + scale) + shift).astype(BF16)
    proj = jnp.dot(h, win_ref[...], preferred_element_type=F32)
    cos = cos_ref[0]
    sin = sin_ref[0]

    cqn = _rms(proj[:, P_CQ:P_CQ + Q_RANK], qg_ref[...]).astype(BF16)
    q2 = jnp.dot(cqn, wuq_ref[...], preferred_element_type=F32)
    ckvn = _rms(proj[:, P_CKV:P_CKV + KV_RANK], kvg_ref[...]).astype(BF16)
    kk = jnp.dot(ckvn, wk_ref[...], preferred_element_type=F32)
    vv = jnp.dot(ckvn, wv_ref[...], preferred_element_type=F32)
    kpe = proj[:, P_KPE:P_KPE + HEAD_SLAB] * cos + proj[:, P_KPE_SW:P_KPE_SW + HEAD_SLAB] * sin
    qw = N_HEADS * HEAD_SLAB
    for hd in range(N_HEADS):
        lo = hd * HEAD_SLAB
        qh = q2[:, lo:lo + HEAD_SLAB] * cos + q2[:, qw + lo:qw + lo + HEAD_SLAB] * sin
        q_ref[0, hd] = (qh * Q_SCALE).astype(BF16)
        k_ref[0, hd] = (kk[:, lo:lo + HEAD_SLAB] + kpe).astype(BF16)
    v_ref[0] = vv.astype(BF16)

    @pl.when(i == 0)
    def _():
        hist_ref[0:HALO, :] = jnp.zeros((HALO, CONV_W), F32)

    cv = proj[:, P_GC:P_GC + CONV_W] * proj[:, P_VAL:P_VAL + CONV_W]
    hist_ref[HALO:HALO + tm, :] = cv
    cv1 = hist_ref[HALO - 1:HALO - 1 + tm, :]
    cv2 = hist_ref[HALO - 2:HALO - 2 + tm, :]
    conv = cw_ref[0:1, :] * cv2 + cw_ref[1:2, :] * cv1 + cw_ref[2:3, :] * cv
    conv = proj[:, P_GB:P_GB + CONV_W] * conv
    conv_ref[0] = _rms(conv, cg_ref[...]).astype(BF16)
    hist_ref[0:HALO, :] = cv[tm - HALO:, :]


def _proj(x, mod_l, g, win, qg, wuq, kvg, wk, wv, cw, cg, cos, sin, layer, tm):
    B, S, D = x.shape
    tok = lambda w: pl.BlockSpec((1, tm, w), lambda b, i: (b, i, 0))
    row = lambda w: _row_spec(w, layer)
    wspec = lambda r, c: _resident((None, r, c), lambda b, i: (layer, 0, 0))
    heads = pl.BlockSpec((1, N_HEADS, tm, HEAD_SLAB), lambda b, i: (b, 0, i, 0))
    return pl.pallas_call(
        functools.partial(_proj_kernel, tm=tm),
        grid=(B, S // tm),
        in_specs=[tok(D),
                  _mod_spec(layer),
                  row(D), wspec(D, P_WIDTH),
                  row(Q_RANK), wspec(Q_RANK, 2 * N_HEADS * HEAD_SLAB),
                  row(KV_RANK), wspec(KV_RANK, N_HEADS * HEAD_SLAB), wspec(KV_RANK, ATTN_W),
                  pl.BlockSpec((None, 3, CONV_W), lambda b, i: (layer, 0, 0)),
                  row(CONV_W), tok(HEAD_SLAB), tok(HEAD_SLAB)],
        out_specs=[heads, heads, tok(ATTN_W), tok(CONV_W)],
        out_shape=[jax.ShapeDtypeStruct((B, N_HEADS, S, HEAD_SLAB), BF16),
                   jax.ShapeDtypeStruct((B, N_HEADS, S, HEAD_SLAB), BF16),
                   jax.ShapeDtypeStruct((B, S, ATTN_W), BF16),
                   jax.ShapeDtypeStruct((B, S, CONV_W), BF16)],
        scratch_shapes=[pltpu.VMEM((HALO + tm, CONV_W), F32)],
        compiler_params=_params(2),
        name="mixer_proj",
    )(x, mod_l, g, win, qg, wuq, kvg, wk, wv, cw, cg, cos, sin)


def _attn_kernel(q_ref, k_ref, v_ref, o_ref, *, tq):
    i = pl.program_id(2)
    row = lax.broadcasted_iota(jnp.int32, (tq, tq), 0)
    col = lax.broadcasted_iota(jnp.int32, (tq, tq), 1)
    causal = col <= row
    outs = []
    for hh in range(2):
        q = q_ref[0, hh]

        def step(j, carry, masked, hh=hh, q=q):
            m, l, acc = carry
            start = pl.multiple_of(j * tq, tq)
            kj = k_ref[0, hh, pl.ds(start, tq), :]
            vj = v_ref[0, pl.ds(start, tq), :]
            s = lax.dot_general(q, kj, (((1,), (1,)), ((), ())), preferred_element_type=F32)
            if masked:
                s = jnp.where(causal, s, -jnp.inf)
            m_new = jnp.maximum(m, jnp.max(s, axis=-1, keepdims=True))
            alpha = jnp.exp2(m - m_new)
            p = jnp.exp2(s - m_new)
            l = alpha * l + jnp.sum(p, axis=-1, keepdims=True)
            acc = alpha * acc + jnp.dot(p.astype(BF16), vj, preferred_element_type=F32)
            return m_new, l, acc

        init = (jnp.full((tq, 1), -jnp.inf, F32), jnp.zeros((tq, 1), F32),
                jnp.zeros((tq, 2 * V_DIM), F32))
        carry = lax.fori_loop(0, i, functools.partial(step, masked=False), init)
        _, l, acc = step(i, carry, True)
        outs.append(acc / l)
    lane = lax.broadcasted_iota(jnp.int32, (tq, 2 * V_DIM), 1)
    o_ref[0] = jnp.where(lane < V_DIM, outs[0], outs[1])


def _attention(q, k, v, tq):
    B, H, S, _ = q.shape
    return pl.pallas_call(
        functools.partial(_attn_kernel, tq=tq),
        grid=(B, H // 2, S // tq),
        in_specs=[pl.BlockSpec((1, 2, tq, HEAD_SLAB), lambda b, p, i: (b, p, i, 0)),
                  pl.BlockSpec((1, 2, S, HEAD_SLAB), lambda b, p, i: (b, p, 0, 0)),
                  pl.BlockSpec((1, S, 2 * V_DIM), lambda b, p, i: (b, 0, p))],
        out_specs=pl.BlockSpec((1, tq, 2 * V_DIM), lambda b, p, i: (b, i, p)),
        out_shape=jax.ShapeDtypeStruct((B, S, ATTN_W), F32),
        compiler_params=_params(3),
        name="mla_attention",
    )(q, k, v)


def _out_kernel(x_ref, mod_ref, attn_ref, conv_ref, ag_ref, wo_ref, o_ref):
    an = _rms(attn_ref[0], ag_ref[...]).astype(BF16)
    y = jnp.dot(an, wo_ref[0:ATTN_W, :], preferred_element_type=F32)
    y = y + jnp.dot(conv_ref[0], wo_ref[ATTN_W:, :], preferred_element_type=F32)
    o_ref[0] = x_ref[0] + (1.0 + mod_ref[0, 5:6, :]) * y


def _out_proj(x, mod_l, attn, conv, ag, wo, layer, tm):
    B, S, D = x.shape
    tok = lambda w: pl.BlockSpec((1, tm, w), lambda b, i: (b, i, 0))
    return pl.pallas_call(
        _out_kernel,
        grid=(B, S // tm),
        in_specs=[tok(D),
                  _mod_spec(layer),
                  tok(ATTN_W), tok(CONV_W),
                  _row_spec(ATTN_W, layer),
                  _resident((None, ATTN_W + CONV_W, D), lambda b, i: (layer, 0, 0))],
        out_specs=tok(D),
        out_shape=jax.ShapeDtypeStruct((B, S, D), F32),
        compiler_params=_params(2),
        name="mixer_out",
    )(x, mod_l, attn, conv, ag, wo)


def _layout_w_in(w_in):
    L, D, _ = w_in.shape
    o = 0
    cq = w_in[:, :, o:o + Q_RANK]; o += Q_RANK
    ckv = w_in[:, :, o:o + KV_RANK]; o += KV_RANK
    kpe = w_in[:, :, o:o + QK_ROPE]; o += QK_ROPE
    rest = w_in[:, :, o:]
    z_nope = jnp.zeros((L, D, QK_NOPE), w_in.dtype)
    z_pad = jnp.zeros((L, D, HEAD_SLAB - QK_NOPE - QK_ROPE), w_in.dtype)
    kpe_sw = jnp.concatenate([kpe[:, :, HALF_ROPE:], kpe[:, :, :HALF_ROPE]], axis=-1)
    return jnp.concatenate([cq, ckv, z_nope, kpe, z_pad, z_nope, kpe_sw, z_pad, rest], axis=-1).astype(BF16)


def _layout_w_uq(w_uq):
    L, R, _ = w_uq.shape
    w = w_uq.reshape(L, R, N_HEADS, QK_NOPE + QK_ROPE)
    nope, pe = w[..., :QK_NOPE], w[..., QK_NOPE:]
    pe_sw = jnp.concatenate([pe[..., HALF_ROPE:], pe[..., :HALF_ROPE]], axis=-1)
    z_pad = jnp.zeros((L, R, N_HEADS, HEAD_SLAB - QK_NOPE - QK_ROPE), w.dtype)
    q = jnp.concatenate([nope, pe, z_pad], axis=-1).reshape(L, R, N_HEADS * HEAD_SLAB)
    q_sw = jnp.concatenate([jnp.zeros_like(nope), pe_sw, z_pad], axis=-1).reshape(L, R, N_HEADS * HEAD_SLAB)
    return jnp.concatenate([q, q_sw], axis=-1).astype(BF16)


def _layout_w_ukv(w_ukv):
    L, R, _ = w_ukv.shape
    w = w_ukv.reshape(L, R, N_HEADS, QK_NOPE + V_DIM)
    k_nope, v = w[..., :QK_NOPE], w[..., QK_NOPE:]
    z = jnp.zeros((L, R, N_HEADS, HEAD_SLAB - QK_NOPE), w.dtype)
    wk = jnp.concatenate([k_nope, z], axis=-1).reshape(L, R, N_HEADS * HEAD_SLAB)
    wv = v.reshape(L, R, ATTN_W)
    return wk.astype(BF16), wv.astype(BF16)


def kernel(x, c, positions, w_ada, b_ada, ffn1_norm, ffn1_w_gu, ffn1_w_down, mix_norm, w_in, q_a_norm, w_uq, kv_a_norm, w_ukv, conv_w, attn_out_norm, conv_out_norm, w_o, ffn2_norm, ffn2_w_gu, ffn2_w_down, final_norm):
    B, S, D = x.shape
    L = w_ada.shape[0]
    tm = min(S, 512)
    tq = min(S, 512)

    mod = _modulation(c, w_ada, b_ada)
    cos, sin = _rope_tables(positions)

    wgu1, wd1 = ffn1_w_gu.astype(BF16), ffn1_w_down.astype(BF16)
    wgu2, wd2 = ffn2_w_gu.astype(BF16), ffn2_w_down.astype(BF16)
    win = _layout_w_in(w_in)
    wuq = _layout_w_uq(w_uq)
    wk, wv = _layout_w_ukv(w_ukv)
    wo = w_o.astype(BF16)
    final_g = final_norm.reshape(1, D)
    rows = lambda g: g.reshape(L, 1, g.shape[-1])
    g1, g2, gm = rows(ffn1_norm), rows(ffn2_norm), rows(mix_norm)
    gq, gkv = rows(q_a_norm), rows(kv_a_norm)
    ga, gc = rows(attn_out_norm), rows(conv_out_norm)

    for l in range(L):
        x = _ffn(x, mod, g1, wgu1, wd1, l, 0, None, tm)
        q, k, v, conv = _proj(x, mod, gm, win, gq, wuq, gkv, wk, wv, conv_w, gc, cos, sin, l, tm)
        attn = _attention(q, k, v, tq)
        x = _out_proj(x, mod, attn, conv, ga, wo, l, tm)
        x = _ffn(x, mod, g2, wgu2, wd2, l, 6, final_g if l == L - 1 else None, tm)
    return x
```

```python
import functools

import jax
import jax.numpy as jnp
from jax import lax
from jax.experimental import pallas as pl
from jax.experimental.pallas import tpu as pltpu

F32 = jnp.float32
BF16 = jnp.bfloat16

D_MODEL = 1024
N_HEADS = 8
QK_NOPE = 64
QK_ROPE = 32
V_DIM = 64
Q_RANK = 384
KV_RANK = 256
CONV_W = 512
ATTN_W = N_HEADS * V_DIM
D_FF = 2816
N_MOD = 9
EPS = 1e-6
ROPE_THETA = 10000.0
HEAD_SLAB = 128
HALF_ROPE = QK_ROPE // 2
P_CQ = 0
P_CKV = P_CQ + Q_RANK
P_KPE = P_CKV + KV_RANK
P_KPE_SW = P_KPE + HEAD_SLAB
P_GB = P_KPE_SW + HEAD_SLAB
P_GC = P_GB + CONV_W
P_VAL = P_GC + CONV_W
P_WIDTH = P_VAL + CONV_W
HALO = 8
Q_SCALE = float((QK_NOPE + QK_ROPE) ** -0.5 * 1.4426950408889634)
VMEM_LIMIT = 56 * 1024 * 1024
HEADS_PER_STEP = 2
NT_DIMS = (((1,), (1,)), ((), ()))


def _rms(x, g):
    return x * lax.rsqrt(jnp.mean(x * x, axis=-1, keepdims=True) + EPS) * g


def _silu(x):
    return x / (1.0 + jnp.exp(-x))


def _params(n_axes):
    return pltpu.CompilerParams(dimension_semantics=("arbitrary",) * n_axes,
                                vmem_limit_bytes=VMEM_LIMIT)


def _resident(shape, index_map):
    return pl.BlockSpec(shape, index_map, pipeline_mode=pl.Buffered(1))


def _row_spec(width, layer):
    return pl.BlockSpec((None, 1, width), lambda *_: (layer, 0, 0))


def _mod_spec(layer):
    return pl.BlockSpec((None, 1, N_MOD, D_MODEL), lambda b, *_: (layer, b, 0, 0))


def _mod_kernel(c_ref, w_ref, b_ref, o_ref):
    c = c_ref[...]
    ca = _silu(c).astype(BF16)
    o_ref[0] = jnp.dot(ca, w_ref[0].astype(BF16), preferred_element_type=F32) + b_ref[0]


def _modulation(c, w_ada, b_ada):
    L, D, N = w_ada.shape
    B = c.shape[0]
    tn = N // 8
    out = pl.pallas_call(
        _mod_kernel,
        grid=(L, N // tn),
        in_specs=[pl.BlockSpec((B, D), lambda l, n: (0, 0)),
                  pl.BlockSpec((1, D, tn), lambda l, n: (l, 0, n)),
                  pl.BlockSpec((1, 1, tn), lambda l, n: (l, 0, n))],
        out_specs=pl.BlockSpec((1, B, tn), lambda l, n: (l, 0, n)),
        out_shape=jax.ShapeDtypeStruct((L, B, N), F32),
        compiler_params=_params(2),
        name="adaln_mod",
    )(c, w_ada, b_ada.reshape(L, 1, N))
    return out.reshape(L, B, N_MOD, D)


def _rope_kernel(pos_ref, invf_ref, sgn_ref, cos_ref, sin_ref):
    ang = pos_ref[0].astype(F32) * invf_ref[...]
    cos_ref[0] = jnp.cos(ang)
    sin_ref[0] = jnp.sin(ang) * sgn_ref[...]


def _rope_tables(positions):
    B, S = positions.shape
    ts = min(S, 512)
    inv_freq = 1.0 / (ROPE_THETA ** (jnp.arange(0, QK_ROPE, 2, dtype=F32) / QK_ROPE))
    zeros_nope = jnp.zeros((QK_NOPE,), F32)
    zeros_pad = jnp.zeros((HEAD_SLAB - QK_NOPE - QK_ROPE,), F32)
    invf = jnp.concatenate([zeros_nope, inv_freq, inv_freq, zeros_pad])[None, :]
    ones = jnp.ones((HALF_ROPE,), F32)
    sgn = jnp.concatenate([zeros_nope, -ones, ones, zeros_pad])[None, :]
    tab = jax.ShapeDtypeStruct((B, S, HEAD_SLAB), F32)
    return pl.pallas_call(
        _rope_kernel,
        grid=(B, S // ts),
        in_specs=[pl.BlockSpec((1, ts, 1), lambda b, i: (b, i, 0)),
                  pl.BlockSpec((1, HEAD_SLAB), lambda b, i: (0, 0)),
                  pl.BlockSpec((1, HEAD_SLAB), lambda b, i: (0, 0))],
        out_specs=[pl.BlockSpec((1, ts, HEAD_SLAB), lambda b, i: (b, i, 0)),
                   pl.BlockSpec((1, ts, HEAD_SLAB), lambda b, i: (b, i, 0))],
        out_shape=[tab, tab],
        compiler_params=_params(2),
        name="rope_tables",
    )(positions.reshape(B, S, 1), invf, sgn)


def _ffn_kernel(x_ref, mod_ref, g_ref, wgu_ref, wd_ref, *rest, mod_base, final):
    if final:
        fg_ref, o_ref = rest
    else:
        (o_ref,) = rest
    x = x_ref[0]
    shift = mod_ref[0, mod_base:mod_base + 1, :]
    scale = mod_ref[0, mod_base + 1:mod_base + 2, :]
    gate = mod_ref[0, mod_base + 2:mod_base + 3, :]
    h = (_rms(x, g_ref[...]) * (1.0 + scale) + shift).astype(BF16)
    gu = jnp.dot(h, wgu_ref[...], preferred_element_type=F32)
    a = (_silu(gu[:, :D_FF]) * gu[:, D_FF:]).astype(BF16)
    y = jnp.dot(a, wd_ref[...], preferred_element_type=F32)
    out = x + (0.5 * (1.0 + gate)) * y
    if final:
        out = _rms(out, fg_ref[...])
    o_ref[0] = out


def _ffn(x, mod_l, g, wgu, wd, layer, mod_base, final_g, tm):
    B, S, D = x.shape
    final = final_g is not None
    in_specs = [pl.BlockSpec((1, tm, D), lambda b, i: (b, i, 0)),
                _mod_spec(layer),
                _row_spec(D, layer),
                _resident((None, D, 2 * D_FF), lambda b, i: (layer, 0, 0)),
                _resident((None, D_FF, D), lambda b, i: (layer, 0, 0))]
    args = [x, mod_l, g, wgu, wd]
    if final:
        in_specs.append(pl.BlockSpec((1, D), lambda b, i: (0, 0)))
        args.append(final_g)
    return pl.pallas_call(
        functools.partial(_ffn_kernel, mod_base=mod_base, final=final),
        grid=(B, S // tm),
        in_specs=in_specs,
        out_specs=pl.BlockSpec((1, tm, D), lambda b, i: (b, i, 0)),
        out_shape=jax.ShapeDtypeStruct((B, S, D), F32),
        compiler_params=_params(2),
        name="ffn_final" if final else "ffn",
    )(*args)


def _proj_kernel(x_ref, mod_ref, g_ref, win_ref, qg_ref, wuq_ref, kvg_ref, wk_ref, wv_ref,
                 cw_ref, cg_ref, cos_ref, sin_ref,
                 q_ref, k_ref, vt_ref, conv_ref, hist_ref, *, tm):
    i = pl.program_id(1)
    x = x_ref[0]
    shift = mod_ref[0, 3:4, :]
    scale = mod_ref[0, 4:5, :]
    h = (_rms(x, g_ref[...]) * (1.0 + scale) + shift).astype(BF16)
    proj = jnp.dot(h, win_ref[...], preferred_element_type=F32)
    cos = cos_ref[0]
    sin = sin_ref[0]

    cqn = _rms(proj[:, P_CQ:P_CQ + Q_RANK], qg_ref[...]).astype(BF16)
    q2 = jnp.dot(cqn, wuq_ref[...], preferred_element_type=F32)
    ckvn = _rms(proj[:, P_CKV:P_CKV + KV_RANK], kvg_ref[...]).astype(BF16)
    kk = jnp.dot(ckvn, wk_ref[...], preferred_element_type=F32)
    vt = lax.dot_general(wv_ref[...], ckvn, NT_DIMS, preferred_element_type=F32)
    kpe = proj[:, P_KPE:P_KPE + HEAD_SLAB] * cos + proj[:, P_KPE_SW:P_KPE_SW + HEAD_SLAB] * sin
    qw = N_HEADS * HEAD_SLAB
    for hd in range(N_HEADS):
        lo = hd * HEAD_SLAB
        qh = q2[:, lo:lo + HEAD_SLAB] * cos + q2[:, qw + lo:qw + lo + HEAD_SLAB] * sin
        q_ref[0, hd] = (qh * Q_SCALE).astype(BF16)
        k_ref[0, hd] = (kk[:, lo:lo + HEAD_SLAB] + kpe).astype(BF16)
    vt_ref[0] = vt.astype(BF16)

    @pl.when(i == 0)
    def _():
        hist_ref[0:HALO, :] = jnp.zeros((HALO, CONV_W), F32)

    cv = proj[:, P_GC:P_GC + CONV_W] * proj[:, P_VAL:P_VAL + CONV_W]
    hist_ref[HALO:HALO + tm, :] = cv
    cv1 = hist_ref[HALO - 1:HALO - 1 + tm, :]
    cv2 = hist_ref[HALO - 2:HALO - 2 + tm, :]
    conv = cw_ref[0:1, :] * cv2 + cw_ref[1:2, :] * cv1 + cw_ref[2:3, :] * cv
    conv = proj[:, P_GB:P_GB + CONV_W] * conv
    conv_ref[0] = _rms(conv, cg_ref[...]).astype(BF16)
    hist_ref[0:HALO, :] = cv[tm - HALO:, :]


def _proj(x, mod_l, g, win, qg, wuq, kvg, wk, wv, cw, cg, cos, sin, layer, tm):
    B, S, D = x.shape
    tok = lambda w: pl.BlockSpec((1, tm, w), lambda b, i: (b, i, 0))
    row = lambda w: _row_spec(w, layer)
    wspec = lambda r, c: _resident((None, r, c), lambda b, i: (layer, 0, 0))
    heads = pl.BlockSpec((1, N_HEADS, tm, HEAD_SLAB), lambda b, i: (b, 0, i, 0))
    return pl.pallas_call(
        functools.partial(_proj_kernel, tm=tm),
        grid=(B, S // tm),
        in_specs=[tok(D),
                  _mod_spec(layer),
                  row(D), wspec(D, P_WIDTH),
                  row(Q_RANK), wspec(Q_RANK, 2 * N_HEADS * HEAD_SLAB),
                  row(KV_RANK), wspec(KV_RANK, N_HEADS * HEAD_SLAB), wspec(ATTN_W, KV_RANK),
                  pl.BlockSpec((None, 3, CONV_W), lambda b, i: (layer, 0, 0)),
                  row(CONV_W), tok(HEAD_SLAB), tok(HEAD_SLAB)],
        out_specs=[heads, heads, pl.BlockSpec((1, ATTN_W, tm), lambda b, i: (b, 0, i)), tok(CONV_W)],
        out_shape=[jax.ShapeDtypeStruct((B, N_HEADS, S, HEAD_SLAB), BF16),
                   jax.ShapeDtypeStruct((B, N_HEADS, S, HEAD_SLAB), BF16),
                   jax.ShapeDtypeStruct((B, ATTN_W, S), BF16),
                   jax.ShapeDtypeStruct((B, S, CONV_W), BF16)],
        scratch_shapes=[pltpu.VMEM((HALO + tm, CONV_W), F32)],
        compiler_params=_params(2),
        name="mixer_proj",
    )(x, mod_l, g, win, qg, wuq, kvg, wk, wv, cw, cg, cos, sin)


def _attn_kernel(q_ref, k_ref, vt_ref, o_ref, s_ref, *, tq):
    i = pl.program_id(2)
    krow = lax.broadcasted_iota(jnp.int32, (tq, tq), 0)
    qcol = lax.broadcasted_iota(jnp.int32, (tq, tq), 1)
    causal = krow <= qcol
    qs = [q_ref[0, h] for h in range(HEADS_PER_STEP)]

    def scores(h, blk, masked):
        start = pl.multiple_of(blk * tq, tq)
        kj = k_ref[0, h, pl.ds(start, tq), :]
        st = lax.dot_general(kj, qs[h], NT_DIMS, preferred_element_type=F32)
        if masked:
            st = jnp.where(causal, st, -jnp.inf)
        return st

    def consume(h, blk, m, l, acc, cm):
        start = pl.multiple_of(blk * tq, tq)
        vtj = vt_ref[0, h * V_DIM:(h + 1) * V_DIM, pl.ds(start, tq)]
        m_new = jnp.maximum(m, cm)
        alpha = jnp.exp2(m - m_new)
        p = jnp.exp2(s_ref[h] - m_new)
        l = alpha * l + jnp.sum(p, axis=0, keepdims=True)
        acc = alpha * acc + jnp.dot(vtj, p.astype(BF16), preferred_element_type=F32)
        return m_new, l, acc

    state = []
    for h in range(HEADS_PER_STEP):
        st = scores(h, i, True)
        s_ref[h] = st
        state.append((jnp.full((1, tq), -jnp.inf, F32), jnp.zeros((1, tq), F32),
                      jnp.zeros((V_DIM, tq), F32), jnp.max(st, axis=0, keepdims=True)))

    def body(t, state):
        prev = jnp.where(t == 0, i, t - 1)
        sts = [scores(h, t, False) for h in range(HEADS_PER_STEP)]
        new = []
        for h in range(HEADS_PER_STEP):
            m, l, acc, cm = state[h]
            new.append(consume(h, prev, m, l, acc, cm))
        out = []
        for h in range(HEADS_PER_STEP):
            s_ref[h] = sts[h]
            out.append(new[h] + (jnp.max(sts[h], axis=0, keepdims=True),))
        return tuple(out)

    state = lax.fori_loop(0, i, body, tuple(state))
    last = jnp.where(i == 0, i, i - 1)
    outs = []
    for h in range(HEADS_PER_STEP):
        m, l, acc, cm = state[h]
        m, l, acc = consume(h, last, m, l, acc, cm)
        outs.append(acc / l)
    o_ref[0] = jnp.concatenate(outs, axis=0).T


def _attention(q, k, vt, tq):
    B, H, S, _ = q.shape
    hps = HEADS_PER_STEP
    return pl.pallas_call(
        functools.partial(_attn_kernel, tq=tq),
        grid=(B, H // hps, S // tq),
        in_specs=[pl.BlockSpec((1, hps, tq, HEAD_SLAB), lambda b, p, i: (b, p, i, 0)),
                  pl.BlockSpec((1, hps, S, HEAD_SLAB), lambda b, p, i: (b, p, 0, 0)),
                  pl.BlockSpec((1, hps * V_DIM, S), lambda b, p, i: (b, p, 0))],
        out_specs=pl.BlockSpec((1, tq, hps * V_DIM), lambda b, p, i: (b, i, p)),
        out_shape=jax.ShapeDtypeStruct((B, S, ATTN_W), F32),
        scratch_shapes=[pltpu.VMEM((hps, tq, tq), F32)],
        compiler_params=_params(3),
        name="mla_attention",
    )(q, k, vt)


def _out_kernel(x_ref, mod_ref, attn_ref, conv_ref, ag_ref, wo_ref, o_ref):
    an = _rms(attn_ref[0], ag_ref[...]).astype(BF16)
    y = jnp.dot(an, wo_ref[0:ATTN_W, :], preferred_element_type=F32)
    y = y + jnp.dot(conv_ref[0], wo_ref[ATTN_W:, :], preferred_element_type=F32)
    o_ref[0] = x_ref[0] + (1.0 + mod_ref[0, 5:6, :]) * y


def _out_proj(x, mod_l, attn, conv, ag, wo, layer, tm):
    B, S, D = x.shape
    tok = lambda w: pl.BlockSpec((1, tm, w), lambda b, i: (b, i, 0))
    return pl.pallas_call(
        _out_kernel,
        grid=(B, S // tm),
        in_specs=[tok(D),
                  _mod_spec(layer),
                  tok(ATTN_W), tok(CONV_W),
                  _row_spec(ATTN_W, layer),
                  _resident((None, ATTN_W + CONV_W, D), lambda b, i: (layer, 0, 0))],
        out_specs=tok(D),
        out_shape=jax.ShapeDtypeStruct((B, S, D), F32),
        compiler_params=_params(2),
        name="mixer_out",
    )(x, mod_l, attn, conv, ag, wo)


def _layout_w_in(w_in):
    L, D, _ = w_in.shape
    o = 0
    cq = w_in[:, :, o:o + Q_RANK]; o += Q_RANK
    ckv = w_in[:, :, o:o + KV_RANK]; o += KV_RANK
    kpe = w_in[:, :, o:o + QK_ROPE]; o += QK_ROPE
    rest = w_in[:, :, o:]
    z_nope = jnp.zeros((L, D, QK_NOPE), w_in.dtype)
    z_pad = jnp.zeros((L, D, HEAD_SLAB - QK_NOPE - QK_ROPE), w_in.dtype)
    kpe_sw = jnp.concatenate([kpe[:, :, HALF_ROPE:], kpe[:, :, :HALF_ROPE]], axis=-1)
    return jnp.concatenate([cq, ckv, z_nope, kpe, z_pad, z_nope, kpe_sw, z_pad, rest], axis=-1).astype(BF16)


def _layout_w_uq(w_uq):
    L, R, _ = w_uq.shape
    w = w_uq.reshape(L, R, N_HEADS, QK_NOPE + QK_ROPE)
    nope, pe = w[..., :QK_NOPE], w[..., QK_NOPE:]
    pe_sw = jnp.concatenate([pe[..., HALF_ROPE:], pe[..., :HALF_ROPE]], axis=-1)
    z_pad = jnp.zeros((L, R, N_HEADS, HEAD_SLAB - QK_NOPE - QK_ROPE), w.dtype)
    q = jnp.concatenate([nope, pe, z_pad], axis=-1).reshape(L, R, N_HEADS * HEAD_SLAB)
    q_sw = jnp.concatenate([jnp.zeros_like(nope), pe_sw, z_pad], axis=-1).reshape(L, R, N_HEADS * HEAD_SLAB)
    return jnp.concatenate([q, q_sw], axis=-1).astype(BF16)


def _layout_w_ukv(w_ukv):
    L, R, _ = w_ukv.shape
    w = w_ukv.reshape(L, R, N_HEADS, QK_NOPE + V_DIM)
    k_nope, v = w[..., :QK_NOPE], w[..., QK_NOPE:]
    z = jnp.zeros((L, R, N_HEADS, HEAD_SLAB - QK_NOPE), w.dtype)
    wk = jnp.concatenate([k_nope, z], axis=-1).reshape(L, R, N_HEADS * HEAD_SLAB)
    wv_t = jnp.swapaxes(v.reshape(L, R, ATTN_W), 1, 2)
    return wk.astype(BF16), wv_t.astype(BF16)


def kernel(x, c, positions, w_ada, b_ada, ffn1_norm, ffn1_w_gu, ffn1_w_down, mix_norm, w_in, q_a_norm, w_uq, kv_a_norm, w_ukv, conv_w, attn_out_norm, conv_out_norm, w_o, ffn2_norm, ffn2_w_gu, ffn2_w_down, final_norm):
    B, S, D = x.shape
    L = w_ada.shape[0]
    tm = min(S, 512)
    tq = min(S, 512)

    mod = _modulation(c, w_ada, b_ada)
    cos, sin = _rope_tables(positions)

    wgu1, wd1 = ffn1_w_gu.astype(BF16), ffn1_w_down.astype(BF16)
    wgu2, wd2 = ffn2_w_gu.astype(BF16), ffn2_w_down.astype(BF16)
    win = _layout_w_in(w_in)
    wuq = _layout_w_uq(w_uq)
    wk, wv_t = _layout_w_ukv(w_ukv)
    wo = w_o.astype(BF16)
    final_g = final_norm.reshape(1, D)
    rows = lambda g: g.reshape(L, 1, g.shape[-1])
    g1, g2, gm = rows(ffn1_norm), rows(ffn2_norm), rows(mix_norm)
    gq, gkv = rows(q_a_norm), rows(kv_a_norm)
    ga, gc = rows(attn_out_norm), rows(conv_out_norm)

    for l in range(L):
        x = _ffn(x, mod, g1, wgu1, wd1, l, 0, None, tm)
        q, k, vt, conv = _proj(x, mod, gm, win, gq, wuq, gkv, wk, wv_t, conv_w, gc, cos, sin, l, tm)
        attn = _attention(q, k, vt, tq)
        x = _out_proj(x, mod, attn, conv, ga, wo, l, tm)
        x = _ffn(x, mod, g2, wgu2, wd2, l, 6, final_g if l == L - 1 else None, tm)
    return x
```

```python
import functools

import jax
import jax.numpy as jnp
from jax import lax
from jax.experimental import pallas as pl
from jax.experimental.pallas import tpu as pltpu

F32 = jnp.float32
BF16 = jnp.bfloat16

D_MODEL = 1024
N_HEADS = 8
QK_NOPE = 64
QK_ROPE = 32
V_DIM = 64
Q_RANK = 384
KV_RANK = 256
CONV_W = 512
ATTN_W = N_HEADS * V_DIM
D_FF = 2816
N_MOD = 9
EPS = 1e-6
ROPE_THETA = 10000.0
HEAD_SLAB = 128
HALF_ROPE = QK_ROPE // 2
P_CQ = 0
P_CKV = P_CQ + Q_RANK
P_KPE = P_CKV + KV_RANK
P_KPE_SW = P_KPE + HEAD_SLAB
P_GB = P_KPE_SW + HEAD_SLAB
P_GC = P_GB + CONV_W
P_VAL = P_GC + CONV_W
P_WIDTH = P_VAL + CONV_W
HALO = 8
Q_SCALE = float((QK_NOPE + QK_ROPE) ** -0.5 * 1.4426950408889634)
VMEM_LIMIT = 56 * 1024 * 1024
HEADS_PER_STEP = 2
NT_DIMS = (((1,), (1,)), ((), ()))


def _rms(x, g):
    return x * lax.rsqrt(jnp.mean(x * x, axis=-1, keepdims=True) + EPS) * g


def _silu(x):
    return x / (1.0 + jnp.exp(-x))


def _params(n_axes):
    return pltpu.CompilerParams(dimension_semantics=("arbitrary",) * n_axes,
                                vmem_limit_bytes=VMEM_LIMIT)


def _resident(shape, index_map):
    return pl.BlockSpec(shape, index_map, pipeline_mode=pl.Buffered(1))


def _row_spec(width, layer):
    return pl.BlockSpec((None, 1, width), lambda *_: (layer, 0, 0))


def _mod_spec(layer):
    return pl.BlockSpec((None, 1, N_MOD, D_MODEL), lambda b, *_: (layer, b, 0, 0))


def _mod_kernel(c_ref, w_ref, b_ref, o_ref):
    c = c_ref[...]
    ca = _silu(c).astype(BF16)
    o_ref[0] = jnp.dot(ca, w_ref[0].astype(BF16), preferred_element_type=F32) + b_ref[0]


def _modulation(c, w_ada, b_ada):
    L, D, N = w_ada.shape
    B = c.shape[0]
    tn = N // 8
    out = pl.pallas_call(
        _mod_kernel,
        grid=(L, N // tn),
        in_specs=[pl.BlockSpec((B, D), lambda l, n: (0, 0)),
                  pl.BlockSpec((1, D, tn), lambda l, n: (l, 0, n)),
                  pl.BlockSpec((1, 1, tn), lambda l, n: (l, 0, n))],
        out_specs=pl.BlockSpec((1, B, tn), lambda l, n: (l, 0, n)),
        out_shape=jax.ShapeDtypeStruct((L, B, N), F32),
        compiler_params=_params(2),
        name="adaln_mod",
    )(c, w_ada, b_ada.reshape(L, 1, N))
    return out.reshape(L, B, N_MOD, D)


def _rope_kernel(pos_ref, invf_ref, sgn_ref, cos_ref, sin_ref):
    ang = pos_ref[0].astype(F32) * invf_ref[...]
    cos_ref[0] = jnp.cos(ang)
    sin_ref[0] = jnp.sin(ang) * sgn_ref[...]


def _rope_tables(positions):
    B, S = positions.shape
    ts = min(S, 512)
    inv_freq = 1.0 / (ROPE_THETA ** (jnp.arange(0, QK_ROPE, 2, dtype=F32) / QK_ROPE))
    zeros_nope = jnp.zeros((QK_NOPE,), F32)
    zeros_pad = jnp.zeros((HEAD_SLAB - QK_NOPE - QK_ROPE,), F32)
    invf = jnp.concatenate([zeros_nope, inv_freq, inv_freq, zeros_pad])[None, :]
    ones = jnp.ones((HALF_ROPE,), F32)
    sgn = jnp.concatenate([zeros_nope, -ones, ones, zeros_pad])[None, :]
    tab = jax.ShapeDtypeStruct((B, S, HEAD_SLAB), F32)
    return pl.pallas_call(
        _rope_kernel,
        grid=(B, S // ts),
        in_specs=[pl.BlockSpec((1, ts, 1), lambda b, i: (b, i, 0)),
                  pl.BlockSpec((1, HEAD_SLAB), lambda b, i: (0, 0)),
                  pl.BlockSpec((1, HEAD_SLAB), lambda b, i: (0, 0))],
        out_specs=[pl.BlockSpec((1, ts, HEAD_SLAB), lambda b, i: (b, i, 0)),
                   pl.BlockSpec((1, ts, HEAD_SLAB), lambda b, i: (b, i, 0))],
        out_shape=[tab, tab],
        compiler_params=_params(2),
        name="rope_tables",
    )(positions.reshape(B, S, 1), invf, sgn)


def _ffn_kernel(*refs, mod_base, merge, final):
    refs = list(refs)
    o_ref = refs.pop()
    x_ref, mod_ref, g_ref, wgu_ref, wd_ref = refs[:5]
    extra = refs[5:]
    x = x_ref[0]
    if merge:
        attn_ref, conv_ref, ag_ref, wo_ref = extra[:4]
        extra = extra[4:]
        an = _rms(attn_ref[0], ag_ref[...]).astype(BF16)
        ym = jnp.dot(an, wo_ref[0:ATTN_W, :], preferred_element_type=F32)
        ym = ym + jnp.dot(conv_ref[0], wo_ref[ATTN_W:, :], preferred_element_type=F32)
        x = x + (1.0 + mod_ref[0, 5:6, :]) * ym
    if final:
        (fg_ref,) = extra
    shift = mod_ref[0, mod_base:mod_base + 1, :]
    scale = mod_ref[0, mod_base + 1:mod_base + 2, :]
    gate = mod_ref[0, mod_base + 2:mod_base + 3, :]
    h = (_rms(x, g_ref[...]) * (1.0 + scale) + shift).astype(BF16)
    gu = jnp.dot(h, wgu_ref[...], preferred_element_type=F32)
    a = (_silu(gu[:, :D_FF]) * gu[:, D_FF:]).astype(BF16)
    y = jnp.dot(a, wd_ref[...], preferred_element_type=F32)
    out = x + (0.5 * (1.0 + gate)) * y
    if final:
        out = _rms(out, fg_ref[...])
    o_ref[0] = out


def _ffn(x, mod_l, g, wgu, wd, layer, mod_base, tm, mixer=None, final_g=None):
    B, S, D = x.shape
    tok = lambda w: pl.BlockSpec((1, tm, w), lambda b, i: (b, i, 0))
    in_specs = [tok(D),
                _mod_spec(layer),
                _row_spec(D, layer),
                _resident((None, D, 2 * D_FF), lambda b, i: (layer, 0, 0)),
                _resident((None, D_FF, D), lambda b, i: (layer, 0, 0))]
    args = [x, mod_l, g, wgu, wd]
    if mixer is not None:
        in_specs += [tok(ATTN_W), tok(CONV_W), _row_spec(ATTN_W, layer),
                     _resident((None, ATTN_W + CONV_W, D), lambda b, i: (layer, 0, 0))]
        args += list(mixer)
    if final_g is not None:
        in_specs.append(pl.BlockSpec((1, D), lambda b, i: (0, 0)))
        args.append(final_g)
    return pl.pallas_call(
        functools.partial(_ffn_kernel, mod_base=mod_base, merge=mixer is not None, final=final_g is not None),
        grid=(B, S // tm),
        in_specs=in_specs,
        out_specs=tok(D),
        out_shape=jax.ShapeDtypeStruct((B, S, D), F32),
        compiler_params=_params(2),
        name="ffn" if mixer is None else ("merge_ffn_final" if final_g is not None else "merge_ffn"),
    )(*args)


def _proj_kernel(x_ref, mod_ref, g_ref, win_ref, qg_ref, wuq_ref, kvg_ref, wk_ref, wv_ref,
                 cw_ref, cg_ref, cos_ref, sin_ref,
                 q_ref, k_ref, vt_ref, conv_ref, hist_ref, *, tm):
    @pl.when(pl.program_id(1) == 0)
    def _():
        hist_ref[0:HALO, :] = jnp.zeros((HALO, CONV_W), F32)

    x = x_ref[0]
    shift = mod_ref[0, 3:4, :]
    scale = mod_ref[0, 4:5, :]
    h = (_rms(x, g_ref[...]) * (1.0 + scale) + shift).astype(BF16)
    proj = jnp.dot(h, win_ref[:, 0:P_GB], preferred_element_type=F32)
    cos = cos_ref[0]
    sin = sin_ref[0]

    cqn = _rms(proj[:, P_CQ:P_CQ + Q_RANK], qg_ref[...]).astype(BF16)
    q2 = jnp.dot(cqn, wuq_ref[...], preferred_element_type=F32)
    gates = jnp.dot(h, win_ref[:, P_GB:P_WIDTH], preferred_element_type=F32)
    ckvn = _rms(proj[:, P_CKV:P_CKV + KV_RANK], kvg_ref[...]).astype(BF16)
    kk = jnp.dot(ckvn, wk_ref[...], preferred_element_type=F32)
    vt = lax.dot_general(wv_ref[...], ckvn, NT_DIMS, preferred_element_type=F32)
    kpe = proj[:, P_KPE:P_KPE + HEAD_SLAB] * cos + proj[:, P_KPE_SW:P_KPE_SW + HEAD_SLAB] * sin
    qw = N_HEADS * HEAD_SLAB
    for hd in range(N_HEADS):
        lo = hd * HEAD_SLAB
        qh = q2[:, lo:lo + HEAD_SLAB] * cos + q2[:, qw + lo:qw + lo + HEAD_SLAB] * sin
        q_ref[0, hd] = (qh * Q_SCALE).astype(BF16)
        k_ref[0, hd] = (kk[:, lo:lo + HEAD_SLAB] + kpe).astype(BF16)
    vt_ref[0] = vt.astype(BF16)

    cv =gates[:, CONV_W:2 * CONV_W] * gates[:, 2 * CONV_W:3 * CONV_W]
    hist_ref[HALO:HALO + tm, :] = cv
    cv1 = hist_ref[HALO - 1:HALO - 1 + tm, :]
    cv2 = hist_ref[HALO - 2:HALO - 2 + tm, :]
    conv = cw_ref[0:1, :] * cv2 + cw_ref[1:2, :] * cv1 + cw_ref[2:3, :] * cv
    conv = gates[:, 0:CONV_W] * conv
    conv_ref[0] = _rms(conv, cg_ref[...]).astype(BF16)
    hist_ref[0:HALO, :] = cv[tm - HALO:, :]


def _proj(x, mod_l, g, win, qg, wuq, kvg, wk, wv, cw, cg, cos, sin, layer, tm):
    B, S, D = x.shape
    tok = lambda w: pl.BlockSpec((1, tm, w), lambda b, i: (b, i, 0))
    row = lambda w: _row_spec(w, layer)
    wspec = lambda r, c: _resident((None, r, c), lambda b, i: (layer, 0, 0))
    heads = pl.BlockSpec((1, N_HEADS, tm, HEAD_SLAB), lambda b, i: (b, 0, i, 0))
    return pl.pallas_call(
        functools.partial(_proj_kernel, tm=tm),
        grid=(B, S // tm),
        in_specs=[tok(D),
                  _mod_spec(layer),
                  row(D), wspec(D, P_WIDTH),
                  row(Q_RANK), wspec(Q_RANK, 2 * N_HEADS * HEAD_SLAB),
                  row(KV_RANK), wspec(KV_RANK, N_HEADS * HEAD_SLAB), wspec(ATTN_W, KV_RANK),
                  pl.BlockSpec((None, 3, CONV_W), lambda b, i: (layer, 0, 0)),
                  row(CONV_W), tok(HEAD_SLAB), tok(HEAD_SLAB)],
        out_specs=[heads, heads, pl.BlockSpec((1, ATTN_W, tm), lambda b, i: (b, 0, i)), tok(CONV_W)],
        out_shape=[jax.ShapeDtypeStruct((B, N_HEADS, S, HEAD_SLAB), BF16),
                   jax.ShapeDtypeStruct((B, N_HEADS, S, HEAD_SLAB), BF16),
                   jax.ShapeDtypeStruct((B, ATTN_W, S), BF16),
                   jax.ShapeDtypeStruct((B, S, CONV_W), BF16)],
        scratch_shapes=[pltpu.VMEM((HALO + tm, CONV_W), F32)],
        compiler_params=_params(2),
        name="mixer_proj",
    )(x, mod_l, g, win, qg, wuq, kvg, wk, wv, cw, cg, cos, sin)


def _attn_kernel(q_ref, k_ref, vt_ref, o_ref, s_ref, *, tq):
    i = pl.program_id(2)
    krow = lax.broadcasted_iota(jnp.int32, (tq, tq), 0)
    qcol = lax.broadcasted_iota(jnp.int32, (tq, tq), 1)
    causal = krow <= qcol
    qs = [q_ref[0, h] for h in range(HEADS_PER_STEP)]

    def scores(h, blk, masked):
        start = pl.multiple_of(blk * tq, tq)
        kj = k_ref[0, h, pl.ds(start, tq), :]
        st = lax.dot_general(kj, qs[h], NT_DIMS, preferred_element_type=F32)
        if masked:
            st = jnp.where(causal, st, -jnp.inf)
        return st

    def consume(h, blk, m, l, acc, cm):
        start = pl.multiple_of(blk * tq, tq)
        vtj = vt_ref[0, h * V_DIM:(h + 1) * V_DIM, pl.ds(start, tq)]
        m_new = jnp.maximum(m, cm)
        alpha = jnp.exp2(m - m_new)
        p = jnp.exp2(s_ref[h] - m_new)
        l = alpha * l + jnp.sum(p, axis=0, keepdims=True)
        acc = alpha * acc + jnp.dot(vtj, p.astype(BF16), preferred_element_type=F32)
        return m_new, l, acc

    state = []
    for h in range(HEADS_PER_STEP):
        st = scores(h, i, True)
        s_ref[h] = st
        state.append((jnp.full((1, tq), -jnp.inf, F32), jnp.zeros((1, tq), F32),
                      jnp.zeros((V_DIM, tq), F32), jnp.max(st, axis=0, keepdims=True)))

    def body(t, state):
        prev = jnp.where(t == 0, i, t - 1)
        sts = [scores(h, t, False) for h in range(HEADS_PER_STEP)]
        new = []
        for h in range(HEADS_PER_STEP):
            m, l, acc, cm = state[h]
            new.append(consume(h, prev, m, l, acc, cm))
        out = []
        for h in range(HEADS_PER_STEP):
            s_ref[h] = sts[h]
            out.append(new[h] + (jnp.max(sts[h], axis=0, keepdims=True),))
        return tuple(out)

    state = lax.fori_loop(0, i, body, tuple(state))
    last = jnp.where(i == 0, i, i - 1)
    outs = []
    for h in range(HEADS_PER_STEP):
        m, l, acc, cm = state[h]
        m, l, acc = consume(h, last, m, l, acc, cm)
        outs.append(acc / l)
    o_ref[0] = jnp.concatenate(outs, axis=0).T


def _attention(q, k, vt, tq):
    B, H, S, _ = q.shape
    hps = HEADS_PER_STEP
    return pl.pallas_call(
        functools.partial(_attn_kernel, tq=tq),
        grid=(B, H // hps, S // tq),
        in_specs=[pl.BlockSpec((1, hps, tq, HEAD_SLAB), lambda b, p, i: (b, p, i, 0)),
                  pl.BlockSpec((1, hps, S, HEAD_SLAB), lambda b, p, i: (b, p, 0, 0)),
                  pl.BlockSpec((1, hps * V_DIM, S), lambda b, p, i: (b, p, 0))],
        out_specs=pl.BlockSpec((1, tq, hps * V_DIM), lambda b, p, i: (b, i, p)),
        out_shape=jax.ShapeDtypeStruct((B, S, ATTN_W), F32),
        scratch_shapes=[pltpu.VMEM((hps, tq, tq), F32)],
        compiler_params=_params(3),
        name="mla_attention",
    )(q, k, vt)


def _layout_w_in(w_in):
    L, D, _ = w_in.shape
    o = 0
    cq = w_in[:, :, o:o + Q_RANK]; o += Q_RANK
    ckv = w_in[:, :, o:o + KV_RANK]; o += KV_RANK
    kpe = w_in[:, :, o:o + QK_ROPE]; o += QK_ROPE
    rest = w_in[:, :, o:]
    z_nope = jnp.zeros((L, D, QK_NOPE), w_in.dtype)
    z_pad = jnp.zeros((L, D, HEAD_SLAB - QK_NOPE - QK_ROPE), w_in.dtype)
    kpe_sw = jnp.concatenate([kpe[:, :, HALF_ROPE:], kpe[:, :, :HALF_ROPE]], axis=-1)
    return jnp.concatenate([cq, ckv, z_nope, kpe, z_pad, z_nope, kpe_sw, z_pad, rest], axis=-1).astype(BF16)


def _layout_w_uq(w_uq):
    L, R, _ = w_uq.shape
    w = w_uq.reshape(L, R, N_HEADS, QK_NOPE + QK_ROPE)
    nope, pe = w[..., :QK_NOPE], w[..., QK_NOPE:]
    pe_sw = jnp.concatenate([pe[..., HALF_ROPE:], pe[..., :HALF_ROPE]], axis=-1)
    z_pad = jnp.zeros((L, R, N_HEADS, HEAD_SLAB - QK_NOPE - QK_ROPE), w.dtype)
    q = jnp.concatenate([nope, pe, z_pad], axis=-1).reshape(L, R, N_HEADS * HEAD_SLAB)
    q_sw = jnp.concatenate([jnp.zeros_like(nope), pe_sw, z_pad], axis=-1).reshape(L, R, N_HEADS * HEAD_SLAB)
    return jnp.concatenate([q, q_sw], axis=-1).astype(BF16)


def _layout_w_ukv(w_ukv):
    L, R, _ = w_ukv.shape
    w = w_ukv.reshape(L, R, N_HEADS, QK_NOPE + V_DIM)
    k_nope, v = w[..., :QK_NOPE], w[..., QK_NOPE:]
    z = jnp.zeros((L, R, N_HEADS, HEAD_SLAB - QK_NOPE), w.dtype)
    wk = jnp.concatenate([k_nope, z], axis=-1).reshape(L, R, N_HEADS * HEAD_SLAB)
    wv_t = jnp.swapaxes(v.reshape(L, R, ATTN_W), 1, 2)
    return wk.astype(BF16), wv_t.astype(BF16)


def kernel(x, c, positions, w_ada, b_ada, ffn1_norm, ffn1_w_gu, ffn1_w_down, mix_norm, w_in, q_a_norm, w_uq, kv_a_norm, w_ukv, conv_w, attn_out_norm, conv_out_norm, w_o, ffn2_norm, ffn2_w_gu, ffn2_w_down, final_norm):
    B, S, D = x.shape
    L = w_ada.shape[0]
    tm = min(S, 512)
    tq = min(S, 512)

    mod = _modulation(c, w_ada, b_ada)
    cos, sin = _rope_tables(positions)

    wgu1, wd1 = ffn1_w_gu.astype(BF16), ffn1_w_down.astype(BF16)
    wgu2, wd2 = ffn2_w_gu.astype(BF16), ffn2_w_down.astype(BF16)
    win = _layout_w_in(w_in)
    wuq = _layout_w_uq(w_uq)
    wk, wv_t = _layout_w_ukv(w_ukv)
    wo = w_o.astype(BF16)
    final_g = final_norm.reshape(1, D)
    rows = lambda g: g.reshape(L, 1, g.shape[-1])
    g1, g2, gm = rows(ffn1_norm), rows(ffn2_norm), rows(mix_norm)
    gq, gkv = rows(q_a_norm), rows(kv_a_norm)
    ga, gc = rows(attn_out_norm), rows(conv_out_norm)

    for l in range(L):
        x = _ffn(x, mod, g1, wgu1, wd1, l, 0, tm)
        q, k, vt, conv = _proj(x, mod, gm, win, gq, wuq, gkv, wk, wv_t, conv_w, gc, cos, sin, l, tm)
        attn = _attention(q, k, vt, tq)
        x = _ffn(x, mod, g2, wgu2, wd2, l, 6, tm, mixer=(attn, conv, ga, wo),
                 final_g=final_g if l == L - 1 else None)
    return x
```

```python
import functools

import jax
import jax.numpy as jnp
from jax import lax
from jax.experimental import pallas as pl
from jax.experimental.pallas import tpu as pltpu

F32 = jnp.float32
BF16 = jnp.bfloat16

D_MODEL = 1024
N_HEADS = 8
QK_NOPE = 64
QK_ROPE = 32
V_DIM = 64
Q_RANK = 384
KV_RANK = 256
CONV_W = 512
ATTN_W = N_HEADS * V_DIM
D_FF = 2816
N_MOD = 9
EPS = 1e-6
ROPE_THETA = 10000.0
HEAD_SLAB = 128
HALF_ROPE = QK_ROPE // 2
P_CQ = 0
P_CKV = P_CQ + Q_RANK
P_KPE = P_CKV + KV_RANK
P_KPE_SW = P_KPE + HEAD_SLAB
P_GB = P_KPE_SW + HEAD_SLAB
P_GC = P_GB + CONV_W
P_VAL = P_GC + CONV_W
P_WIDTH = P_VAL + CONV_W
HALO = 8
Q_SCALE = float((QK_NOPE + QK_ROPE) ** -0.5 * 1.4426950408889634)
VMEM_LIMIT = 56 * 1024 * 1024
HEADS_PER_STEP = 2
MXU_TILE = 256
VALUE_ROWS = V_DIM + 16
SCORE_SLOTS = (0, MXU_TILE // 4, 2 * MXU_TILE // 4)
OUT_SLOTS = (3 * MXU_TILE // 4, 3 * MXU_TILE // 4 + VALUE_ROWS // 4)
NT_DIMS = (((1,), (1,)), ((), ()))


def _rms(x, g):
    return x * lax.rsqrt(jnp.mean(x * x, axis=-1, keepdims=True) + EPS) * g


def _silu(x):
    return x / (1.0 + jnp.exp(-x))


def _params(n_axes):
    return pltpu.CompilerParams(dimension_semantics=("arbitrary",) * n_axes,
                                vmem_limit_bytes=VMEM_LIMIT)


def _resident(shape, index_map):
    return pl.BlockSpec(shape, index_map, pipeline_mode=pl.Buffered(1))


def _row_spec(width, layer):
    return pl.BlockSpec((None, 1, width), lambda *_: (layer, 0, 0))


def _mod_spec(layer):
    return pl.BlockSpec((None, 1, N_MOD, D_MODEL), lambda b, *_: (layer, b, 0, 0))


def _mod_kernel(c_ref, w_ref, b_ref, o_ref):
    c = c_ref[...]
    ca = _silu(c).astype(BF16)
    o_ref[0] = jnp.dot(ca, w_ref[0].astype(BF16), preferred_element_type=F32) + b_ref[0]


def _modulation(c, w_ada, b_ada):
    L, D, N = w_ada.shape
    B = c.shape[0]
    tn = N // 8
    out = pl.pallas_call(
        _mod_kernel,
        grid=(L, N // tn),
        in_specs=[pl.BlockSpec((B, D), lambda l, n: (0, 0)),
                  pl.BlockSpec((1, D, tn), lambda l, n: (l, 0, n)),
                  pl.BlockSpec((1, 1, tn), lambda l, n: (l, 0, n))],
        out_specs=pl.BlockSpec((1, B, tn), lambda l, n: (l, 0, n)),
        out_shape=jax.ShapeDtypeStruct((L, B, N), F32),
        compiler_params=_params(2),
        name="adaln_mod",
    )(c, w_ada, b_ada.reshape(L, 1, N))
    return out.reshape(L, B, N_MOD, D)


def _rope_kernel(pos_ref, invf_ref, sgn_ref, cos_ref, sin_ref):
    ang = pos_ref[0].astype(F32) * invf_ref[...]
    cos_ref[0] = jnp.cos(ang)
    sin_ref[0] = jnp.sin(ang) * sgn_ref[...]


def _rope_tables(positions):
    B, S = positions.shape
    ts = min(S, 512)
    inv_freq = 1.0 / (ROPE_THETA ** (jnp.arange(0, QK_ROPE, 2, dtype=F32) / QK_ROPE))
    zeros_nope = jnp.zeros((QK_NOPE,), F32)
    zeros_pad = jnp.zeros((HEAD_SLAB - QK_NOPE - QK_ROPE,), F32)
    invf = jnp.concatenate([zeros_nope, inv_freq, inv_freq, zeros_pad])[None, :]
    ones = jnp.ones((HALF_ROPE,), F32)
    sgn = jnp.concatenate([zeros_nope, -ones, ones, zeros_pad])[None, :]
    tab = jax.ShapeDtypeStruct((B, S, HEAD_SLAB), F32)
    return pl.pallas_call(
        _rope_kernel,
        grid=(B, S // ts),
        in_specs=[pl.BlockSpec((1, ts, 1), lambda b, i: (b, i, 0)),
                  pl.BlockSpec((1, HEAD_SLAB), lambda b, i: (0, 0)),
                  pl.BlockSpec((1, HEAD_SLAB), lambda b, i: (0, 0))],
        out_specs=[pl.BlockSpec((1, ts, HEAD_SLAB), lambda b, i: (b, i, 0)),
                   pl.BlockSpec((1, ts, HEAD_SLAB), lambda b, i: (b, i, 0))],
        out_shape=[tab, tab],
        compiler_params=_params(2),
        name="rope_tables",
    )(positions.reshape(B, S, 1), invf, sgn)


def _ffn_kernel(*refs, mod_base, merge, final):
    refs = list(refs)
    o_ref = refs.pop()
    x_ref, mod_ref, g_ref, wgu_ref, wd_ref = refs[:5]
    extra = refs[5:]
    x = x_ref[0]
    if merge:
        attn_ref, conv_ref, ag_ref, wo_ref = extra[:4]
        extra = extra[4:]
        an = _rms(attn_ref[0], ag_ref[...]).astype(BF16)
        ym = jnp.dot(an, wo_ref[0:ATTN_W, :], preferred_element_type=F32)
        ym = ym + jnp.dot(conv_ref[0], wo_ref[ATTN_W:, :], preferred_element_type=F32)
        x = x + (1.0 + mod_ref[0, 5:6, :]) * ym
    if final:
        (fg_ref,) = extra
    shift = mod_ref[0, mod_base:mod_base + 1, :]
    scale = mod_ref[0, mod_base + 1:mod_base + 2, :]
    gate = mod_ref[0, mod_base + 2:mod_base + 3, :]
    h = (_rms(x, g_ref[...]) * (1.0 + scale) + shift).astype(BF16)
    gu = jnp.dot(h, wgu_ref[...], preferred_element_type=F32)
    a = (_silu(gu[:, :D_FF]) * gu[:, D_FF:]).astype(BF16)
    y = jnp.dot(a, wd_ref[...], preferred_element_type=F32)
    out = x + (0.5 * (1.0 + gate)) * y
    if final:
        out = _rms(out, fg_ref[...])
    o_ref[0] = out


def _ffn(x, mod_l, g, wgu, wd, layer, mod_base, tm, mixer=None, final_g=None):
    B, S, D = x.shape
    tok = lambda w: pl.BlockSpec((1, tm, w), lambda b, i: (b, i, 0))
    in_specs = [tok(D),
                _mod_spec(layer),
                _row_spec(D, layer),
                _resident((None, D, 2 * D_FF), lambda b, i: (layer, 0, 0)),
                _resident((None, D_FF, D), lambda b, i: (layer, 0, 0))]
    args = [x, mod_l, g, wgu, wd]
    if mixer is not None:
        in_specs += [tok(ATTN_W), tok(CONV_W), _row_spec(ATTN_W, layer),
                     _resident((None, ATTN_W + CONV_W, D), lambda b, i: (layer, 0, 0))]
        args += list(mixer)
    if final_g is not None:
        in_specs.append(pl.BlockSpec((1, D), lambda b, i: (0, 0)))
        args.append(final_g)
    return pl.pallas_call(
        functools.partial(_ffn_kernel, mod_base=mod_base, merge=mixer is not None, final=final_g is not None),
        grid=(B, S // tm),
        in_specs=in_specs,
        out_specs=tok(D),
        out_shape=jax.ShapeDtypeStruct((B, S, D), F32),
        compiler_params=_params(2),
        name="ffn" if mixer is None else ("merge_ffn_final" if final_g is not None else "merge_ffn"),
    )(*args)


def _proj_kernel(x_ref, mod_ref, g_ref, win_ref, qg_ref, wuq_ref, kvg_ref, wk_ref, wv_ref,
                 cw_ref, cg_ref, cos_ref, sin_ref,
                 q_ref, k_ref, vt_ref, conv_ref, hist_ref, *, tm):
    @pl.when(pl.program_id(1) == 0)
    def _():
        hist_ref[0:HALO, :] = jnp.zeros((HALO, CONV_W), F32)

    x = x_ref[0]
    shift = mod_ref[0, 3:4, :]
    scale = mod_ref[0, 4:5, :]
    h = (_rms(x, g_ref[...]) * (1.0 + scale) + shift).astype(BF16)
    proj = jnp.dot(h, win_ref[:, 0:P_GB], preferred_element_type=F32)
    cos = cos_ref[0]
    sin = sin_ref[0]

    cqn = _rms(proj[:, P_CQ:P_CQ + Q_RANK], qg_ref[...]).astype(BF16)
    q2 = jnp.dot(cqn, wuq_ref[...], preferred_element_type=F32)
    gates = jnp.dot(h, win_ref[:, P_GB:P_WIDTH], preferred_element_type=F32)
    ckvn = _rms(proj[:, P_CKV:P_CKV + KV_RANK], kvg_ref[...]).astype(BF16)
    kk = jnp.dot(ckvn, wk_ref[...], preferred_element_type=F32)
    vt = lax.dot_general(wv_ref[...], ckvn, NT_DIMS, preferred_element_type=F32)
    kpe = proj[:, P_KPE:P_KPE + HEAD_SLAB] * cos + proj[:, P_KPE_SW:P_KPE_SW + HEAD_SLAB] * sin
    qw = N_HEADS * HEAD_SLAB
    for hd in range(N_HEADS):
        lo = hd * HEAD_SLAB
        qh = q2[:, lo:lo + HEAD_SLAB] * cos + q2[:, qw + lo:qw + lo + HEAD_SLAB] * sin
        q_ref[0, hd] = (qh * Q_SCALE).astype(BF16)
        k_ref[0, hd] = (kk[:, lo:lo + HEAD_SLAB] + kpe).astype(BF16)
    vt_ref[0] = vt.astype(BF16)

    cv =gates[:, CONV_W:2 * CONV_W] * gates[:, 2 * CONV_W:3 * CONV_W]
    hist_ref[HALO:HALO + tm, :] = cv
    cv1 = hist_ref[HALO - 1:HALO - 1 + tm, :]
    cv2 = hist_ref[HALO - 2:HALO - 2 + tm, :]
    conv = cw_ref[0:1, :] * cv2 + cw_ref[1:2, :] * cv1 + cw_ref[2:3, :] * cv
    conv = gates[:, 0:CONV_W] * conv
    conv_ref[0] = _rms(conv, cg_ref[...]).astype(BF16)
    hist_ref[0:HALO, :] = cv[tm - HALO:, :]


def _proj(x, mod_l, g, win, qg, wuq, kvg, wk, wv, cw, cg, cos, sin, layer, tm):
    B, S, D = x.shape
    tok = lambda w: pl.BlockSpec((1, tm, w), lambda b, i: (b, i, 0))
    row = lambda w: _row_spec(w, layer)
    wspec = lambda r, c: _resident((None, r, c), lambda b, i: (layer, 0, 0))
    heads = pl.BlockSpec((1, N_HEADS, tm, HEAD_SLAB), lambda b, i: (b, 0, i, 0))
    return pl.pallas_call(
        functools.partial(_proj_kernel, tm=tm),
        grid=(B, S // tm),
        in_specs=[tok(D),
                  _mod_spec(layer),
                  row(D), wspec(D, P_WIDTH),
                  row(Q_RANK), wspec(Q_RANK, 2 * N_HEADS * HEAD_SLAB),
                  row(KV_RANK), wspec(KV_RANK, N_HEADS * HEAD_SLAB), wspec(ATTN_W, KV_RANK),
                  pl.BlockSpec((None, 3, CONV_W), lambda b, i: (layer, 0, 0)),
                  row(CONV_W), tok(HEAD_SLAB), tok(HEAD_SLAB)],
        out_specs=[heads, heads, pl.BlockSpec((1, ATTN_W, tm), lambda b, i: (b, 0, i)), tok(CONV_W)],
        out_shape=[jax.ShapeDtypeStruct((B, N_HEADS, S, HEAD_SLAB), BF16),
                   jax.ShapeDtypeStruct((B, N_HEADS, S, HEAD_SLAB), BF16),
                   jax.ShapeDtypeStruct((B, ATTN_W, S), BF16),
                   jax.ShapeDtypeStruct((B, S, CONV_W), BF16)],
        scratch_shapes=[pltpu.VMEM((HALO + tm, CONV_W), F32)],
        compiler_params=_params(2),
        name="mixer_proj",
    )(x, mod_l, g, win, qg, wuq, kvg, wk, wv, cw, cg, cos, sin)


def _attn_kernel(q_ref, k_ref, vt_ref, o_ref, s_ref, acc_ref, *, tq):
    i = pl.program_id(2)
    hb = MXU_TILE
    krow = lax.broadcasted_iota(jnp.int32, (hb, hb), 0)
    qcol = lax.broadcasted_iota(jnp.int32, (hb, hb), 1)
    tri = krow <= qcol
    zpad = jnp.zeros((hb, hb - HEAD_SLAB), BF16)
    ones = jnp.ones((VALUE_ROWS - V_DIM, hb), BF16)
    heads = range(HEADS_PER_STEP)

    def q_half(h, qh):
        return jnp.concatenate([q_ref[0, h, qh * hb:(qh + 1) * hb, :], zpad], axis=1)

    def k_half(h, blk, kh):
        start = pl.multiple_of(blk * tq + kh * hb, hb)
        return jnp.concatenate([k_ref[0, h, pl.ds(start, hb), :], zpad], axis=1)

    def v_half(h, blk, kh):
        start = pl.multiple_of(blk * tq + kh * hb, hb)
        return jnp.concatenate([vt_ref[0, h * V_DIM:(h + 1) * V_DIM, pl.ds(start, hb)], ones], axis=0)

    def quad_ref(h, kh, qh):
        return s_ref.at[h, kh * hb:(kh + 1) * hb, qh * hb:(qh + 1) * hb]

    def prestage(h):
        pltpu.matmul_push_rhs(q_half(h, 0), 0, h, transpose=True)

    def issue3(h, blk):
        pltpu.matmul_acc_lhs(SCORE_SLOTS[0], k_half(h, blk, 0), h, 0)
        pltpu.matmul_acc_lhs(SCORE_SLOTS[1], k_half(h, blk, 1), h, None)
        pltpu.matmul_push_rhs(q_half(h, 1), 0, h, transpose=True)
        pltpu.matmul_acc_lhs(SCORE_SLOTS[2], k_half(h, blk, 0), h, 0)

    def issue4(h, blk):
        pltpu.matmul_acc_lhs(SCORE_SLOTS[0], k_half(h, blk, 1), h, None)

    def pop_quad(h, slot, kh, qh, diag):
        st = pltpu.matmul_pop(slot, (hb, hb), F32, h)
        if diag:
            if kh > qh:
                st = jnp.full((hb, hb), -jnp.inf, F32)
            elif kh == qh:
                st = jnp.where(tri, st, -jnp.inf)
        quad_ref(h, kh, qh)[...] = st
        return jnp.max(st, axis=0, keepdims=True)

    def numer(h, kh, qh, m_new, reg):
        mq = m_new[:, qh * hb:(qh + 1) * hb]
        p = jnp.exp2(quad_ref(h, kh, qh)[...] - mq).astype(BF16)
        pltpu.matmul_push_rhs(p, reg, h)

    def pv(h, blk, qh):
        pltpu.matmul_acc_lhs(OUT_SLOTS[qh], v_half(h, blk, 0), h, 1)
        pltpu.matmul_acc_lhs(OUT_SLOTS[qh], v_half(h, blk, 1), h, 0)

    def accumulate(h, alpha):
        out = jnp.concatenate([pltpu.matmul_pop(OUT_SLOTS[qh], (VALUE_ROWS, hb), F32, h) for qh in range(2)], axis=1)
        acc_ref[h] = alpha * acc_ref[h] + out

    def col_max(c):
        return jnp.concatenate([jnp.maximum(c[0], c[1]), jnp.maximum(c[2], c[3])], axis=1)

    def order(t):
        return jnp.where(t == 0, i, t - 1)

    for h in heads:
        prestage(h)
        issue3(h, i)
    state = []
    for h in heads:
        c = [pop_quad(h, SCORE_SLOTS[0], 0, 0, True)]
        issue4(h, i)
        c.append(pop_quad(h, SCORE_SLOTS[1], 1, 0, True))
        c.append(pop_quad(h, SCORE_SLOTS[2], 0, 1, True))
        c.append(pop_quad(h, SCORE_SLOTS[0], 1, 1, True))
        acc_ref[h] = jnp.zeros((VALUE_ROWS, tq), F32)
        prestage(h)
        state.append((jnp.full((1, tq), -jnp.inf, F32), col_max(c), jnp.ones((1, tq), F32)))

    def step(t, state, with_scores):
        blk = order(t)
        for h in heads:
            if with_scores:
                issue3(h, t)
            else:
                pltpu.matmul_acc_lhs(SCORE_SLOTS[1], jnp.zeros((16, hb), BF16), h, 0)
                pltpu.matmul_pop(SCORE_SLOTS[1], (16, hb), F32, h)
        ms, alphas, cs = [], [], [[] for _ in heads]
        for h in heads:
            m, cm, _ = state[h]
            m_new = jnp.maximum(m, cm)
            ms.append(m_new)
            alphas.append(jnp.exp2(m - m_new))
        for h in heads:
            numer(h, 0, 0, ms[h], 1)
        for h in heads:
            if with_scores:
                cs[h].append(pop_quad(h, SCORE_SLOTS[0], 0, 0, False))
                issue4(h, t)
            accumulate(h, state[h][2])
        for h in heads:
            numer(h, 1, 0, ms[h], 0)
            pv(h, blk, 0)
        if with_scores:
            for h in heads:
                cs[h].append(pop_quad(h, SCORE_SLOTS[1], 1, 0, False))
        for h in heads:
            numer(h, 0, 1, ms[h], 1)
            numer(h, 1, 1, ms[h], 0)
            pv(h, blk, 1)
        out = []
        for h in heads:
            if with_scores:
                cs[h].append(pop_quad(h, SCORE_SLOTS[2], 0, 1, False))
                cs[h].append(pop_quad(h, SCORE_SLOTS[0], 1, 1, False))
                prestage(h)
                out.append((ms[h], col_max(cs[h]), alphas[h]))
            else:
                out.append((ms[h], state[h][1], alphas[h]))
        return tuple(out)

    state = lax.fori_loop(0, i, lambda t, st: step(t, st, True), tuple(state))
    state = step(i, state, False)
    outs = []
    for h in heads:
        accumulate(h, state[h][2])
        acc = acc_ref[h]
        outs.append(acc[0:V_DIM, :] / acc[V_DIM:V_DIM + 1, :])
    o_ref[0] = jnp.concatenate(outs, axis=0).T


def _attention(q, k, vt, tq):
    B, H, S, _ = q.shape
    hps = HEADS_PER_STEP
    assert tq == 2 * MXU_TILE and HEAD_SLAB <= MXU_TILE
    return pl.pallas_call(
        functools.partial(_attn_kernel, tq=tq),
        grid=(B, H // hps, S // tq),
        in_specs=[pl.BlockSpec((1, hps, tq, HEAD_SLAB), lambda b, p, i: (b, p, i, 0)),
                  pl.BlockSpec((1, hps, S, HEAD_SLAB), lambda b, p, i: (b, p, 0, 0)),
                  pl.BlockSpec((1, hps * V_DIM, S), lambda b, p, i: (b, p, 0))],
        out_specs=pl.BlockSpec((1, tq, hps * V_DIM), lambda b, p, i: (b, i, p)),
        out_shape=jax.ShapeDtypeStruct((B, S, ATTN_W), F32),
        scratch_shapes=[pltpu.VMEM((hps, tq, tq), F32),
                        pltpu.VMEM((hps, VALUE_ROWS, tq), F32)],
        compiler_params=_params(3),
        name="mla_attention",
    )(q, k, vt)


def _layout_w_in(w_in):
    L, D, _ = w_in.shape
    o = 0
    cq = w_in[:, :, o:o + Q_RANK]; o += Q_RANK
    ckv = w_in[:, :, o:o + KV_RANK]; o += KV_RANK
    kpe = w_in[:, :, o:o + QK_ROPE]; o += QK_ROPE
    rest = w_in[:, :, o:]
    z_nope = jnp.zeros((L, D, QK_NOPE), w_in.dtype)
    z_pad = jnp.zeros((L, D, HEAD_SLAB - QK_NOPE - QK_ROPE), w_in.dtype)
    kpe_sw = jnp.concatenate([kpe[:, :, HALF_ROPE:], kpe[:, :, :HALF_ROPE]], axis=-1)
    return jnp.concatenate([cq, ckv, z_nope, kpe, z_pad, z_nope, kpe_sw, z_pad, rest], axis=-1).astype(BF16)


def _layout_w_uq(w_uq):
    L, R, _ = w_uq.shape
    w = w_uq.reshape(L, R, N_HEADS, QK_NOPE + QK_ROPE)
    nope, pe = w[..., :QK_NOPE], w[..., QK_NOPE:]
    pe_sw = jnp.concatenate([pe[..., HALF_ROPE:], pe[..., :HALF_ROPE]], axis=-1)
    z_pad = jnp.zeros((L, R, N_HEADS, HEAD_SLAB - QK_NOPE - QK_ROPE), w.dtype)
    q = jnp.concatenate([nope, pe, z_pad], axis=-1).reshape(L, R, N_HEADS * HEAD_SLAB)
    q_sw = jnp.concatenate([jnp.zeros_like(nope), pe_sw, z_pad], axis=-1).reshape(L, R, N_HEADS * HEAD_SLAB)
    return jnp.concatenate([q, q_sw], axis=-1).astype(BF16)


def _layout_w_ukv(w_ukv):
    L, R, _ = w_ukv.shape
    w = w_ukv.reshape(L, R, N_HEADS, QK_NOPE + V_DIM)
    k_nope, v = w[..., :QK_NOPE], w[..., QK_NOPE:]
    z = jnp.zeros((L, R, N_HEADS, HEAD_SLAB - QK_NOPE), w.dtype)
    wk = jnp.concatenate([k_nope, z], axis=-1).reshape(L, R, N_HEADS * HEAD_SLAB)
    wv_t = jnp.swapaxes(v.reshape(L, R, ATTN_W), 1, 2)
    return wk.astype(BF16), wv_t.astype(BF16)


def kernel(x, c, positions, w_ada, b_ada, ffn1_norm, ffn1_w_gu, ffn1_w_down, mix_norm, w_in, q_a_norm, w_uq, kv_a_norm, w_ukv, conv_w, attn_out_norm, conv_out_norm, w_o, ffn2_norm, ffn2_w_gu, ffn2_w_down, final_norm):
    B, S, D = x.shape
    L = w_ada.shape[0]
    tm = min(S, 512)
    tq = min(S, 512)

    mod = _modulation(c, w_ada, b_ada)
    cos, sin = _rope_tables(positions)

    wgu1, wd1 = ffn1_w_gu.astype(BF16), ffn1_w_down.astype(BF16)
    wgu2, wd2 = ffn2_w_gu.astype(BF16), ffn2_w_down.astype(BF16)
    win = _layout_w_in(w_in)
    wuq = _layout_w_uq(w_uq)
    wk, wv_t = _layout_w_ukv(w_ukv)
    wo = w_o.astype(BF16)
    final_g = final_norm.reshape(1, D)
    rows = lambda g: g.reshape(L, 1, g.shape[-1])
    g1, g2, gm = rows(ffn1_norm), rows(ffn2_norm), rows(mix_norm)
    gq, gkv = rows(q_a_norm), rows(kv_a_norm)
    ga, gc = rows(attn_out_norm), rows(conv_out_norm)

    for l in range(L):
        x = _ffn(x, mod, g1, wgu1, wd1, l, 0, tm)
        q, k, vt, conv = _proj(x, mod, gm, win, gq, wuq, gkv, wk, wv_t, conv_w, gc, cos, sin, l, tm)
        attn = _attention(q, k, vt, tq)
        x = _ffn(x, mod, g2, wgu2, wd2, l, 6, tm, mixer=(attn, conv, ga, wo),
                 final_g=final_g if l == L - 1 else None)
    return x
```

```python
import functools

import jax
import jax.numpy as jnp
from jax import lax
from jax.experimental import pallas as pl
from jax.experimental.pallas import tpu as pltpu

F32 = jnp.float32
BF16 = jnp.bfloat16

D_MODEL = 1024
N_HEADS = 8
QK_NOPE = 64
QK_ROPE = 32
V_DIM = 64
Q_RANK = 384
KV_RANK = 256
CONV_W = 512
ATTN_W = N_HEADS * V_DIM
D_FF = 2816
N_MOD = 9
EPS = 1e-6
ROPE_THETA = 10000.0
HEAD_SLAB = 128
HALF_ROPE = QK_ROPE // 2
P_CQ = 0
P_CKV = P_CQ + Q_RANK
P_KPE = P_CKV + KV_RANK
P_KPE_SW = P_KPE + HEAD_SLAB
P_GB = P_KPE_SW + HEAD_SLAB
P_GC = P_GB + CONV_W
P_VAL = P_GC + CONV_W
P_WIDTH = P_VAL + CONV_W
HALO = 8
Q_SCALE = float((QK_NOPE + QK_ROPE) ** -0.5 * 1.4426950408889634)
VMEM_LIMIT = 56 * 1024 * 1024
FFN_ROW_GROUPS = 2
HEADS_PER_STEP = 2
MXU_TILE = 256
VALUE_ROWS = V_DIM + 16
SCORE_SLOTS = (0, MXU_TILE // 4, 2 * MXU_TILE // 4)
OUT_SLOTS = (3 * MXU_TILE // 4, 3 * MXU_TILE // 4 + VALUE_ROWS // 4)
NT_DIMS = (((1,), (1,)), ((), ()))


def _rms(x, g):
    return x * lax.rsqrt(jnp.mean(x * x, axis=-1, keepdims=True) + EPS) * g


def _silu(x):
    return x / (1.0 + jnp.exp(-x))


def _params(n_axes):
    return pltpu.CompilerParams(dimension_semantics=("arbitrary",) * n_axes,
                                vmem_limit_bytes=VMEM_LIMIT)


def _resident(shape, index_map):
    return pl.BlockSpec(shape, index_map, pipeline_mode=pl.Buffered(1))


def _row_spec(width, layer):
    return pl.BlockSpec((None, 1, width), lambda *_: (layer, 0, 0))


def _mod_spec(layer):
    return pl.BlockSpec((None, 1, N_MOD, D_MODEL), lambda b, *_: (layer, b, 0, 0))


def _mod_kernel(c_ref, w_ref, b_ref, o_ref):
    c = c_ref[...]
    ca = _silu(c).astype(BF16)
    o_ref[0] = jnp.dot(ca, w_ref[0].astype(BF16), preferred_element_type=F32) + b_ref[0]


def _modulation(c, w_ada, b_ada):
    L, D, N = w_ada.shape
    B = c.shape[0]
    tn = N // 8
    out = pl.pallas_call(
        _mod_kernel,
        grid=(L, N // tn),
        in_specs=[pl.BlockSpec((B, D), lambda l, n: (0, 0)),
                  pl.BlockSpec((1, D, tn), lambda l, n: (l, 0, n)),
                  pl.BlockSpec((1, 1, tn), lambda l, n: (l, 0, n))],
        out_specs=pl.BlockSpec((1, B, tn), lambda l, n: (l, 0, n)),
        out_shape=jax.ShapeDtypeStruct((L, B, N), F32),
        compiler_params=_params(2),
        name="adaln_mod",
    )(c, w_ada, b_ada.reshape(L, 1, N))
    return out.reshape(L, B, N_MOD, D)


def _rope_kernel(pos_ref, invf_ref, sgn_ref, cos_ref, sin_ref):
    ang = pos_ref[0].astype(F32) * invf_ref[...]
    cos_ref[0] = jnp.cos(ang)
    sin_ref[0] = jnp.sin(ang) * sgn_ref[...]


def _rope_tables(positions):
    B, S = positions.shape
    ts = min(S, 512)
    inv_freq = 1.0 / (ROPE_THETA ** (jnp.arange(0, QK_ROPE, 2, dtype=F32) / QK_ROPE))
    zeros_nope = jnp.zeros((QK_NOPE,), F32)
    zeros_pad = jnp.zeros((HEAD_SLAB - QK_NOPE - QK_ROPE,), F32)
    invf = jnp.concatenate([zeros_nope, inv_freq, inv_freq, zeros_pad])[None, :]
    ones = jnp.ones((HALF_ROPE,), F32)
    sgn = jnp.concatenate([zeros_nope, -ones, ones, zeros_pad])[None, :]
    tab = jax.ShapeDtypeStruct((B, S, HEAD_SLAB), F32)
    return pl.pallas_call(
        _rope_kernel,
        grid=(B, S // ts),
        in_specs=[pl.BlockSpec((1, ts, 1), lambda b, i: (b, i, 0)),
                  pl.BlockSpec((1, HEAD_SLAB), lambda b, i: (0, 0)),
                  pl.BlockSpec((1, HEAD_SLAB), lambda b, i: (0, 0))],
        out_specs=[pl.BlockSpec((1, ts, HEAD_SLAB), lambda b, i: (b, i, 0)),
                   pl.BlockSpec((1, ts, HEAD_SLAB), lambda b, i: (b, i, 0))],
        out_shape=[tab, tab],
        compiler_params=_params(2),
        name="rope_tables",
    )(positions.reshape(B, S, 1), invf, sgn)


def _ffn_kernel(*refs, mod_base, merge, final):
    refs = list(refs)
    o_ref = refs.pop()
    x_ref, mod_ref, g_ref, wgu_ref, wd_ref = refs[:5]
    extra = refs[5:]
    if merge:
        attn_ref, conv_ref, ag_ref, wo_ref = extra[:4]
        extra = extra[4:]
    if final:
        (fg_ref,) = extra
    shift = mod_ref[0, mod_base:mod_base + 1, :]
    scale = mod_ref[0, mod_base + 1:mod_base + 2, :]
    gate = mod_ref[0, mod_base + 2:mod_base + 3, :]
    tm = x_ref.shape[1]
    for r in range(FFN_ROW_GROUPS):
        rows = slice(r * tm // FFN_ROW_GROUPS, (r + 1) * tm // FFN_ROW_GROUPS)
        x = x_ref[0, rows, :]
        if merge:
            an = _rms(attn_ref[0, rows, :], ag_ref[...]).astype(BF16)
            ym = jnp.dot(an, wo_ref[0:ATTN_W, :], preferred_element_type=F32)
            ym = ym + jnp.dot(conv_ref[0, rows, :], wo_ref[ATTN_W:, :], preferred_element_type=F32)
            x = x + (1.0 + mod_ref[0, 5:6, :]) * ym
        h = (_rms(x, g_ref[...]) * (1.0 + scale) + shift).astype(BF16)
        gu = jnp.dot(h, wgu_ref[...], preferred_element_type=F32)
        a = (_silu(gu[:, :D_FF]) * gu[:, D_FF:]).astype(BF16)
        y = jnp.dot(a, wd_ref[...], preferred_element_type=F32)
        out = x + (0.5 * (1.0 + gate)) * y
        if final:
            out = _rms(out, fg_ref[...])
        o_ref[0, rows, :] = out


def _ffn(x, mod_l, g, wgu, wd, layer, mod_base, tm, mixer=None, final_g=None):
    B, S, D = x.shape
    tok = lambda w: pl.BlockSpec((1, tm, w), lambda b, i: (b, i, 0))
    in_specs = [tok(D),
                _mod_spec(layer),
                _row_spec(D, layer),
                _resident((None, D, 2 * D_FF), lambda b, i: (layer, 0, 0)),
                _resident((None, D_FF, D), lambda b, i: (layer, 0, 0))]
    args = [x, mod_l, g, wgu, wd]
    if mixer is not None:
        in_specs += [tok(ATTN_W), tok(CONV_W), _row_spec(ATTN_W, layer),
                     _resident((None, ATTN_W + CONV_W, D), lambda b, i: (layer, 0, 0))]
        args += list(mixer)
    if final_g is not None:
        in_specs.append(pl.BlockSpec((1, D), lambda b, i: (0, 0)))
        args.append(final_g)
    return pl.pallas_call(
        functools.partial(_ffn_kernel, mod_base=mod_base, merge=mixer is not None, final=final_g is not None),
        grid=(B, S // tm),
        in_specs=in_specs,
        out_specs=tok(D),
        out_shape=jax.ShapeDtypeStruct((B, S, D), F32),
        compiler_params=_params(2),
        name="ffn" if mixer is None else ("merge_ffn_final" if final_g is not None else "merge_ffn"),
    )(*args)


def _proj_kernel(x_ref, mod_ref, g_ref, win_ref, qg_ref, wuq_ref, kvg_ref, wk_ref, wv_ref,
                 cw_ref, cg_ref, cos_ref, sin_ref,
                 q_ref, k_ref, vt_ref, conv_ref, hist_ref, *, tm):
    @pl.when(pl.program_id(1) == 0)
    def _():
        hist_ref[0:HALO, :] = jnp.zeros((HALO, CONV_W), F32)

    x = x_ref[0]
    shift = mod_ref[0, 3:4, :]
    scale = mod_ref[0, 4:5, :]
    h = (_rms(x, g_ref[...]) * (1.0 + scale) + shift).astype(BF16)
    proj = jnp.dot(h, win_ref[:, 0:P_GB], preferred_element_type=F32)
    cos = cos_ref[0]
    sin = sin_ref[0]

    cqn = _rms(proj[:, P_CQ:P_CQ + Q_RANK], qg_ref[...]).astype(BF16)
    q2 = jnp.dot(cqn, wuq_ref[...], preferred_element_type=F32)
    gates = jnp.dot(h, win_ref[:, P_GB:P_WIDTH], preferred_element_type=F32)
    ckvn = _rms(proj[:, P_CKV:P_CKV + KV_RANK], kvg_ref[...]).astype(BF16)
    kk = jnp.dot(ckvn, wk_ref[...], preferred_element_type=F32)
    vt = lax.dot_general(wv_ref[...], ckvn, NT_DIMS, preferred_element_type=F32)
    kpe = proj[:, P_KPE:P_KPE + HEAD_SLAB] * cos + proj[:, P_KPE_SW:P_KPE_SW + HEAD_SLAB] * sin
    qw = N_HEADS * HEAD_SLAB
    for hd in range(N_HEADS):
        lo = hd * HEAD_SLAB
        qh = q2[:, lo:lo + HEAD_SLAB] * cos + q2[:, qw + lo:qw + lo + HEAD_SLAB] * sin
        q_ref[0, hd] = (qh * Q_SCALE).astype(BF16)
        k_ref[0, hd] = (kk[:, lo:lo + HEAD_SLAB] + kpe).astype(BF16)
    vt_ref[0] = vt.astype(BF16)

    cv =gates[:, CONV_W:2 * CONV_W] * gates[:, 2 * CONV_W:3 * CONV_W]
    hist_ref[HALO:HALO + tm, :] = cv
    cv1 = hist_ref[HALO - 1:HALO - 1 + tm, :]
    cv2 = hist_ref[HALO - 2:HALO - 2 + tm, :]
    conv = cw_ref[0:1, :] * cv2 + cw_ref[1:2, :] * cv1 + cw_ref[2:3, :] * cv
    conv = gates[:, 0:CONV_W] * conv
    conv_ref[0] = _rms(conv, cg_ref[...]).astype(BF16)
    hist_ref[0:HALO, :] = cv[tm - HALO:, :]


def _proj(x, mod_l, g, win, qg, wuq, kvg, wk, wv, cw, cg, cos, sin, layer, tm):
    B, S, D = x.shape
    tok = lambda w: pl.BlockSpec((1, tm, w), lambda b, i: (b, i, 0))
    row = lambda w: _row_spec(w, layer)
    wspec = lambda r, c: _resident((None, r, c), lambda b, i: (layer, 0, 0))
    heads = pl.BlockSpec((1, N_HEADS, tm, HEAD_SLAB), lambda b, i: (b, 0, i, 0))
    return pl.pallas_call(
        functools.partial(_proj_kernel, tm=tm),
        grid=(B, S // tm),
        in_specs=[tok(D),
                  _mod_spec(layer),
                  row(D), wspec(D, P_WIDTH),
                  row(Q_RANK), wspec(Q_RANK, 2 * N_HEADS * HEAD_SLAB),
                  row(KV_RANK), wspec(KV_RANK, N_HEADS * HEAD_SLAB), wspec(ATTN_W, KV_RANK),
                  pl.BlockSpec((None, 3, CONV_W), lambda b, i: (layer, 0, 0)),
                  row(CONV_W), tok(HEAD_SLAB), tok(HEAD_SLAB)],
        out_specs=[heads, heads, pl.BlockSpec((1, ATTN_W, tm), lambda b, i: (b, 0, i)), tok(CONV_W)],
        out_shape=[jax.ShapeDtypeStruct((B, N_HEADS, S, HEAD_SLAB), BF16),
                   jax.ShapeDtypeStruct((B, N_HEADS, S, HEAD_SLAB), BF16),
                   jax.ShapeDtypeStruct((B, ATTN_W, S), BF16),
                   jax.ShapeDtypeStruct((B, S, CONV_W), BF16)],
        scratch_shapes=[pltpu.VMEM((HALO + tm, CONV_W), F32)],
        compiler_params=_params(2),
        name="mixer_proj",
    )(x, mod_l, g, win, qg, wuq, kvg, wk, wv, cw, cg, cos, sin)


def _attn_kernel(q_ref, k_ref, vt_ref, o_ref, s_ref, acc_ref, cm_ref, *, tq, n_tiles):
    i = pl.program_id(2)
    hb = MXU_TILE
    krow = lax.broadcasted_iota(jnp.int32, (hb, hb), 0)
    qcol = lax.broadcasted_iota(jnp.int32, (hb, hb), 1)
    tri = krow <= qcol
    zpad = jnp.zeros((hb, hb - HEAD_SLAB), BF16)
    ones = jnp.ones((VALUE_ROWS - V_DIM, hb), BF16)
    heads = range(HEADS_PER_STEP)

    def q_half(h, tile, qh):
        start = pl.multiple_of(tile * tq + qh * hb, hb)
        return jnp.concatenate([q_ref[0, h, pl.ds(start, hb), :], zpad], axis=1)

    def k_half(h, blk, kh):
        start = pl.multiple_of(blk * tq + kh * hb, hb)
        return jnp.concatenate([k_ref[0, h, pl.ds(start, hb), :], zpad], axis=1)

    def v_half(h, blk, kh):
        start = pl.multiple_of(blk * tq + kh * hb, hb)
        return jnp.concatenate([vt_ref[0, h * V_DIM:(h + 1) * V_DIM, pl.ds(start, hb)], ones], axis=0)

    def quad_ref(h, kh, qh):
        return s_ref.at[h, kh * hb:(kh + 1) * hb, qh * hb:(qh + 1) * hb]

    def prestage(h, tile):
        pltpu.matmul_push_rhs(q_half(h, tile, 0), 0, h, transpose=True)

    def drain_stage(h):
        pltpu.matmul_acc_lhs(SCORE_SLOTS[1], jnp.zeros((16, hb), BF16), h, 0)
        pltpu.matmul_pop(SCORE_SLOTS[1], (16, hb), F32, h)

    def issue3(h, tile, blk):
        pltpu.matmul_acc_lhs(SCORE_SLOTS[0], k_half(h, blk, 0), h, 0)
        pltpu.matmul_acc_lhs(SCORE_SLOTS[1], k_half(h, blk, 1), h, None)
        pltpu.matmul_push_rhs(q_half(h, tile, 1), 0, h, transpose=True)
        pltpu.matmul_acc_lhs(SCORE_SLOTS[2], k_half(h, blk, 0), h, 0)

    def issue4(h, blk):
        pltpu.matmul_acc_lhs(SCORE_SLOTS[0], k_half(h, blk, 1), h, None)

    def pop_quad(h, slot, kh, qh, diag):
        st = pltpu.matmul_pop(slot, (hb, hb), F32, h)
        if diag:
            if kh > qh:
                st = jnp.full((hb, hb), -jnp.inf, F32)
            elif kh == qh:
                st = jnp.where(tri, st, -jnp.inf)
        quad_ref(h, kh, qh)[...] = st
        return jnp.max(st, axis=0, keepdims=True)

    def numer(h, kh, qh, m_new, reg):
        mq = m_new[:, qh * hb:(qh + 1) * hb]
        p = jnp.exp2(quad_ref(h, kh, qh)[...] - mq).astype(BF16)
        pltpu.matmul_push_rhs(p, reg, h)

    def pv(h, blk, qh):
        pltpu.matmul_acc_lhs(OUT_SLOTS[qh], v_half(h, blk, 0), h, 1)
        pltpu.matmul_acc_lhs(OUT_SLOTS[qh], v_half(h, blk, 1), h, 0)

    def accumulate(h, alpha):
        out = jnp.concatenate([pltpu.matmul_pop(OUT_SLOTS[qh], (VALUE_ROWS, hb), F32, h) for qh in range(2)], axis=1)
        acc_ref[h] = alpha * acc_ref[h] + out

    def col_max(c):
        return jnp.concatenate([jnp.maximum(c[0], c[1]), jnp.maximum(c[2], c[3])], axis=1)

    def order(t):
        return jnp.where(t == 0, i, t - 1)

    @pl.when(i == 0)
    def _():
        for h in heads:
            drain_stage(h)
            prestage(h, 0)
            issue3(h, 0, 0)
        for h in heads:
            c = [pop_quad(h, SCORE_SLOTS[0], 0, 0, True)]
            issue4(h, 0)
            c.append(pop_quad(h, SCORE_SLOTS[1], 1, 0, True))
            c.append(pop_quad(h, SCORE_SLOTS[2], 0, 1, True))
            c.append(pop_quad(h, SCORE_SLOTS[0], 1, 1, True))
            cm_ref[h] = col_max(c)
            prestage(h, min(1, n_tiles - 1))

    def step(t, state, nxt, staged_tile=None):
        blk = order(t)
        for h in heads:
            if nxt is not None:
                issue3(h, nxt[0], nxt[1])
            else:
                drain_stage(h)
        ms, alphas, cs = [], [], [[] for _ in heads]
        for h in heads:
            m, cm, _ = state[h]
            m_new = jnp.maximum(m, cm)
            ms.append(m_new)
            alphas.append(jnp.exp2(m - m_new))
        for h in heads:
            numer(h, 0, 0, ms[h], 1)
        for h in heads:
            if nxt is not None:
                cs[h].append(pop_quad(h, SCORE_SLOTS[0], 0, 0, nxt[2]))
                issue4(h, nxt[1])
            accumulate(h, state[h][2])
        for h in heads:
            numer(h, 1, 0, ms[h], 0)
            pv(h, blk, 0)
        if nxt is not None:
            for h in heads:
                cs[h].append(pop_quad(h, SCORE_SLOTS[1], 1, 0, nxt[2]))
        for h in heads:
            numer(h, 0, 1, ms[h], 1)
            numer(h, 1, 1, ms[h], 0)
            pv(h, blk, 1)
        out = []
        for h in heads:
            if nxt is not None:
                cs[h].append(pop_quad(h, SCORE_SLOTS[2], 0, 1, nxt[2]))
                cs[h].append(pop_quad(h, SCORE_SLOTS[0], 1, 1, nxt[2]))
                prestage(h, nxt[0] if staged_tile is None else staged_tile)
                out.append((ms[h], col_max(cs[h]), alphas[h]))
            else:
                out.append((ms[h], state[h][1], alphas[h]))
        return tuple(out)

    for h in heads:
        acc_ref[h] = jnp.zeros((VALUE_ROWS, tq), F32)
    state = tuple((jnp.full((1, tq), -jnp.inf, F32), cm_ref[h], jnp.ones((1, tq), F32)) for h in heads)
    next_tile = jnp.minimum(i + 1, n_tiles - 1)
    state = lax.fori_loop(
        0, i, lambda t, st: step(t, st, (i, t, False), jnp.where(t == i - 1, next_tile, i)), state)

    def finish(nxt):
        last = step(i, state, nxt)
        outs = []
        for h in heads:
            if nxt is not None:
                cm_ref[h] = last[h][1]
            accumulate(h, last[h][2])
            acc = acc_ref[h]
            outs.append(acc[0:V_DIM, :] / acc[V_DIM:V_DIM + 1, :])
        o_ref[0] = jnp.concatenate(outs, axis=0).T

    @pl.when(i < n_tiles - 1)
    def _():
        finish((i + 1, i + 1, True))

    @pl.when(i == n_tiles - 1)
    def _():
        finish(None)


def _attention(q, k, vt, tq):
    B, H, S, _ = q.shape
    hps = HEADS_PER_STEP
    assert tq == 2 * MXU_TILE and HEAD_SLAB <= MXU_TILE
    return pl.pallas_call(
        functools.partial(_attn_kernel, tq=tq, n_tiles=S // tq),
        grid=(B, H // hps, S // tq),
        in_specs=[pl.BlockSpec((1, hps, S, HEAD_SLAB), lambda b, p, i: (b, p, 0, 0)),
                  pl.BlockSpec((1, hps, S, HEAD_SLAB), lambda b, p, i: (b, p, 0, 0)),
                  pl.BlockSpec((1, hps * V_DIM, S), lambda b, p, i: (b, p, 0))],
        out_specs=pl.BlockSpec((1, tq, hps * V_DIM), lambda b, p, i: (b, i, p)),
        out_shape=jax.ShapeDtypeStruct((B, S, ATTN_W), F32),
        scratch_shapes=[pltpu.VMEM((hps, tq, tq), F32),
                        pltpu.VMEM((hps, VALUE_ROWS, tq), F32),
                        pltpu.VMEM((hps, 1, tq), F32)],
        compiler_params=_params(3),
        name="mla_attention",
    )(q, k, vt)


def _layout_w_in(w_in):
    L, D, _ = w_in.shape
    o = 0
    cq = w_in[:, :, o:o + Q_RANK]; o += Q_RANK
    ckv = w_in[:, :, o:o + KV_RANK]; o += KV_RANK
    kpe = w_in[:, :, o:o + QK_ROPE]; o += QK_ROPE
    rest = w_in[:, :, o:]
    z_nope = jnp.zeros((L, D, QK_NOPE), w_in.dtype)
    z_pad = jnp.zeros((L, D, HEAD_SLAB - QK_NOPE - QK_ROPE), w_in.dtype)
    kpe_sw = jnp.concatenate([kpe[:, :, HALF_ROPE:], kpe[:, :, :HALF_ROPE]], axis=-1)
    return jnp.concatenate([cq, ckv, z_nope, kpe, z_pad, z_nope, kpe_sw, z_pad, rest], axis=-1).astype(BF16)


def _layout_w_uq(w_uq):
    L, R, _ = w_uq.shape
    w = w_uq.reshape(L, R, N_HEADS, QK_NOPE + QK_ROPE)
    nope, pe = w[..., :QK_NOPE], w[..., QK_NOPE:]
    pe_sw = jnp.concatenate([pe[..., HALF_ROPE:], pe[..., :HALF_ROPE]], axis=-1)
    z_pad = jnp.zeros((L, R, N_HEADS, HEAD_SLAB - QK_NOPE - QK_ROPE), w.dtype)
    q = jnp.concatenate([nope, pe, z_pad], axis=-1).reshape(L, R, N_HEADS * HEAD_SLAB)
    q_sw = jnp.concatenate([jnp.zeros_like(nope), pe_sw, z_pad], axis=-1).reshape(L, R, N_HEADS * HEAD_SLAB)
    return jnp.concatenate([q, q_sw], axis=-1).astype(BF16)


def _layout_w_ukv(w_ukv):
    L, R, _ = w_ukv.shape
    w = w_ukv.reshape(L, R, N_HEADS, QK_NOPE + V_DIM)
    k_nope, v = w[..., :QK_NOPE], w[..., QK_NOPE:]
    z = jnp.zeros((L, R, N_HEADS, HEAD_SLAB - QK_NOPE), w.dtype)
    wk = jnp.concatenate([k_nope, z], axis=-1).reshape(L, R, N_HEADS * HEAD_SLAB)
    wv_t = jnp.swapaxes(v.reshape(L, R, ATTN_W), 1, 2)
    return wk.astype(BF16), wv_t.astype(BF16)


def kernel(x, c, positions, w_ada, b_ada, ffn1_norm, ffn1_w_gu, ffn1_w_down, mix_norm, w_in, q_a_norm, w_uq, kv_a_norm, w_ukv, conv_w, attn_out_norm, conv_out_norm, w_o, ffn2_norm, ffn2_w_gu, ffn2_w_down, final_norm):
    B, S, D = x.shape
    L = w_ada.shape[0]
    tm = min(S, 512)
    tq = min(S, 512)

    mod = _modulation(c, w_ada, b_ada)
    cos, sin = _rope_tables(positions)

    wgu1, wd1 = ffn1_w_gu.astype(BF16), ffn1_w_down.astype(BF16)
    wgu2, wd2 = ffn2_w_gu.astype(BF16), ffn2_w_down.astype(BF16)
    win = _layout_w_in(w_in)
    wuq = _layout_w_uq(w_uq)
    wk, wv_t = _layout_w_ukv(w_ukv)
    wo = w_o.astype(BF16)
    final_g = final_norm.reshape(1, D)
    rows = lambda g: g.reshape(L, 1, g.shape[-1])
    g1, g2, gm = rows(ffn1_norm), rows(ffn2_norm), rows(mix_norm)
    gq, gkv = rows(q_a_norm), rows(kv_a_norm)
    ga, gc = rows(attn_out_norm), rows(conv_out_norm)

    for l in range(L):
        x = _ffn(x, mod, g1, wgu1, wd1, l, 0, tm)
        q, k, vt, conv = _proj(x, mod, gm, win, gq, wuq, gkv, wk, wv_t, conv_w, gc, cos, sin, l, tm)
        attn = _attention(q, k, vt, tq)
        x = _ffn(x, mod, g2, wgu2, wd2, l, 6, tm, mixer=(attn, conv, ga, wo),
                 final_g=final_g if l == L - 1 else None)
    return x
```

```python
import functools

import jax
import jax.numpy as jnp
from jax import lax
from jax.experimental import pallas as pl
from jax.experimental.pallas import tpu as pltpu

F32 = jnp.float32
BF16 = jnp.bfloat16

D_MODEL = 1024
N_HEADS = 8
QK_NOPE = 64
QK_ROPE = 32
V_DIM = 64
Q_RANK = 384
KV_RANK = 256
CONV_W = 512
ATTN_W = N_HEADS * V_DIM
D_FF = 2816
N_MOD = 9
EPS = 1e-6
ROPE_THETA = 10000.0
HEAD_SLAB = 128
HALF_ROPE = QK_ROPE // 2
P_CQ = 0
P_CKV = P_CQ + Q_RANK
P_GB = P_CKV + KV_RANK
P_WIDTH = P_GB + 3 * CONV_W
HALO = 8
Q_SCALE = float((QK_NOPE + QK_ROPE) ** -0.5 * 1.4426950408889634)
VMEM_LIMIT = 56 * 1024 * 1024
FFN_ROW_GROUPS = 2
HEADS_PER_STEP = 2
MXU_TILE = 256
VALUE_ROWS = V_DIM + 16
SCORE_SLOTS = (0, MXU_TILE // 4, 2 * MXU_TILE // 4)
OUT_SLOTS = (3 * MXU_TILE // 4, 3 * MXU_TILE // 4 + VALUE_ROWS // 4)
NT_DIMS = (((1,), (1,)), ((), ()))


def _rms(x, g):
    return x * lax.rsqrt(jnp.mean(x * x, axis=-1, keepdims=True) + EPS) * g


def _silu(x):
    return x / (1.0 + jnp.exp(-x))


def _params(n_axes):
    return pltpu.CompilerParams(dimension_semantics=("arbitrary",) * n_axes,
                                vmem_limit_bytes=VMEM_LIMIT)


def _resident(shape, index_map):
    return pl.BlockSpec(shape, index_map, pipeline_mode=pl.Buffered(1))


def _row_spec(width, layer):
    return pl.BlockSpec((None, 1, width), lambda *_: (layer, 0, 0))


def _mod_spec(layer):
    return pl.BlockSpec((None, 1, N_MOD, D_MODEL), lambda b, *_: (layer, b, 0, 0))


def _mod_kernel(c_ref, w_ref, b_ref, o_ref):
    c = c_ref[...]
    ca = _silu(c).astype(BF16)
    o_ref[0] = jnp.dot(ca, w_ref[0].astype(BF16), preferred_element_type=F32) + b_ref[0]


def _modulation(c, w_ada, b_ada):
    L, D, N = w_ada.shape
    B = c.shape[0]
    tn = N // 8
    out = pl.pallas_call(
        _mod_kernel,
        grid=(L, N // tn),
        in_specs=[pl.BlockSpec((B, D), lambda l, n: (0, 0)),
                  pl.BlockSpec((1, D, tn), lambda l, n: (l, 0, n)),
                  pl.BlockSpec((1, 1, tn), lambda l, n: (l, 0, n))],
        out_specs=pl.BlockSpec((1, B, tn), lambda l, n: (l, 0, n)),
        out_shape=jax.ShapeDtypeStruct((L, B, N), F32),
        compiler_params=_params(2),
        name="adaln_mod",
    )(c, w_ada, b_ada.reshape(L, 1, N))
    return out.reshape(L, B, N_MOD, D)


def _rope_kernel(pos_ref, invf_ref, sgn_ref, cos_ref, sin_ref):
    ang = invf_ref[...] * pos_ref[0].astype(F32)
    cos_ref[0] = jnp.cos(ang)
    sin_ref[0] = jnp.sin(ang) * sgn_ref[...]


def _rope_tables(positions):
    B, S = positions.shape
    ts = min(S, 2048)
    inv_freq = 1.0 / (ROPE_THETA ** (jnp.arange(0, QK_ROPE, 2, dtype=F32) / QK_ROPE))
    invf = jnp.concatenate([inv_freq, inv_freq])[:, None]
    ones = jnp.ones((HALF_ROPE,), F32)
    sgn = jnp.concatenate([-ones, ones])[:, None]
    tab = jax.ShapeDtypeStruct((B, QK_ROPE, S), F32)
    col = pl.BlockSpec((QK_ROPE, 1), lambda b, i: (0, 0))
    out = pl.BlockSpec((1, QK_ROPE, ts), lambda b, i: (b, 0, i))
    return pl.pallas_call(
        _rope_kernel,
        grid=(B, S // ts),
        in_specs=[pl.BlockSpec((1, 1, ts), lambda b, i: (b, 0, i)), col, col],
        out_specs=[out, out],
        out_shape=[tab, tab],
        compiler_params=_params(2),
        name="rope_tables",
    )(positions.reshape(B, 1, S), invf, sgn)


def _ffn_kernel(*refs, mod_base, merge, final):
    refs = list(refs)
    o_ref = refs.pop()
    x_ref, mod_ref, g_ref, wgu_ref, wd_ref = refs[:5]
    extra = refs[5:]
    if merge:
        attn_ref, conv_ref, ag_ref, wo_ref = extra[:4]
        extra = extra[4:]
    if final:
        (fg_ref,) = extra
    shift = mod_ref[0, mod_base:mod_base + 1, :]
    scale = mod_ref[0, mod_base + 1:mod_base + 2, :]
    gate = mod_ref[0, mod_base + 2:mod_base + 3, :]
    tm = x_ref.shape[1]
    groups = 1 if merge else FFN_ROW_GROUPS
    for r in range(groups):
        rows = slice(r * tm // groups, (r + 1) * tm // groups)
        x = x_ref[0, rows, :]
        if merge:
            an = _rms(attn_ref[0, rows, :], ag_ref[...]).astype(BF16)
            ym = jnp.dot(an, wo_ref[0:ATTN_W, :], preferred_element_type=F32)
            ym = ym + jnp.dot(conv_ref[0, rows, :], wo_ref[ATTN_W:, :], preferred_element_type=F32)
            x = x + (1.0 + mod_ref[0, 5:6, :]) * ym
        h = (_rms(x, g_ref[...]) * (1.0 + scale) + shift).astype(BF16)
        gu = jnp.dot(h, wgu_ref[...], preferred_element_type=F32)
        a = (_silu(gu[:, :D_FF]) * gu[:, D_FF:]).astype(BF16)
        y = jnp.dot(a, wd_ref[...], preferred_element_type=F32)
        out = x + (0.5 * (1.0 + gate)) * y
        if final:
            out = _rms(out, fg_ref[...])
        o_ref[0, rows, :] = out


def _ffn(x, mod_l, g, wgu, wd, layer, mod_base, tm, mixer=None, final_g=None):
    B, S, D = x.shape
    tok = lambda w: pl.BlockSpec((1, tm, w), lambda b, i: (b, i, 0))
    in_specs = [tok(D),
                _mod_spec(layer),
                _row_spec(D, layer),
                _resident((None, D, 2 * D_FF), lambda b, i: (layer, 0, 0)),
                _resident((None, D_FF, D), lambda b, i: (layer, 0, 0))]
    args = [x, mod_l, g, wgu, wd]
    if mixer is not None:
        in_specs += [tok(ATTN_W), tok(CONV_W), _row_spec(ATTN_W, layer),
                     _resident((None, ATTN_W + CONV_W, D), lambda b, i: (layer, 0, 0))]
        args += list(mixer)
    if final_g is not None:
        in_specs.append(pl.BlockSpec((1, D), lambda b, i: (0, 0)))
        args.append(final_g)
    return pl.pallas_call(
        functools.partial(_ffn_kernel, mod_base=mod_base, merge=mixer is not None, final=final_g is not None),
        grid=(B, S // tm),
        in_specs=in_specs,
        out_specs=tok(D),
        out_shape=jax.ShapeDtypeStruct((B, S, D), F32),
        compiler_params=_params(2),
        name="ffn" if mixer is None else ("merge_ffn_final" if final_g is not None else "merge_ffn"),
    )(*args)


def _rope_t(rows, cos, sin):
    partner = jnp.concatenate([rows[HALF_ROPE:], rows[:HALF_ROPE]], axis=0)
    return rows * cos + partner * sin


def _proj_kernel(x_ref, mod_ref, g_ref, win_ref, wkpe_ref, qg_ref, wuq_ref, kvg_ref, wk_ref, wv_ref,
                 cw_ref, cg_ref, cos_ref, sin_ref,
                 qt_ref, k_ref, vt_ref, conv_ref, hist_ref, *, tm):
    @pl.when(pl.program_id(1) == 0)
    def _():
        hist_ref[0:HALO, :] = jnp.zeros((HALO, CONV_W), F32)

    x = x_ref[0]
    shift = mod_ref[0, 3:4, :]
    scale = mod_ref[0, 4:5, :]
    h = (_rms(x, g_ref[...]) * (1.0 + scale) + shift).astype(BF16)
    proj = jnp.dot(h, win_ref[:, 0:P_GB], preferred_element_type=F32)
    cos = cos_ref[0]
    sin = sin_ref[0]
    rope_lo, rope_hi = QK_NOPE, QK_NOPE + QK_ROPE

    cqn = _rms(proj[:, P_CQ:P_CQ + Q_RANK], qg_ref[...]).astype(BF16)
    qt = lax.dot_general(wuq_ref[...], cqn, NT_DIMS, preferred_element_type=F32)
    gates = jnp.dot(h, win_ref[:, P_GB:P_WIDTH], preferred_element_type=F32)
    ckvn = _rms(proj[:, P_CKV:P_CKV + KV_RANK], kvg_ref[...]).astype(BF16)
    kk = jnp.dot(ckvn, wk_ref[...], preferred_element_type=F32)
    vt = lax.dot_general(wv_ref[...], ckvn, NT_DIMS, preferred_element_type=F32)
    kpe_t = lax.dot_general(wkpe_ref[...], h, NT_DIMS, preferred_element_type=F32)
    kpe_t = jnp.concatenate([jnp.zeros((rope_lo, tm), F32), _rope_t(kpe_t, cos, sin),
                             jnp.zeros((HEAD_SLAB - rope_hi, tm), F32)], axis=0)
    kpe = kpe_t.T
    for hd in range(N_HEADS):
        lo = hd * HEAD_SLAB
        qh = qt[lo:lo + HEAD_SLAB]
        qh = jnp.concatenate([qh[:rope_lo], _rope_t(qh[rope_lo:rope_hi], cos, sin), qh[rope_hi:]], axis=0)
        qt_ref[0, hd] = (qh * Q_SCALE).astype(BF16)
        k_ref[0, hd] = (kk[:, lo:lo + HEAD_SLAB] + kpe).astype(BF16)
    vt_ref[0] = vt.astype(BF16)

    cv =gates[:, CONV_W:2 * CONV_W] * gates[:, 2 * CONV_W:3 * CONV_W]
    hist_ref[HALO:HALO + tm, :] = cv
    cv1 = hist_ref[HALO - 1:HALO - 1 + tm, :]
    cv2 = hist_ref[HALO - 2:HALO - 2 + tm, :]
    conv = cw_ref[0:1, :] * cv2 + cw_ref[1:2, :] * cv1 + cw_ref[2:3, :] * cv
    conv = gates[:, 0:CONV_W] * conv
    conv_ref[0] = _rms(conv, cg_ref[...]).astype(BF16)
    hist_ref[0:HALO, :] = cv[tm - HALO:, :]


def _proj(x, mod_l, g, win, wkpe, qg, wuq, kvg, wk, wv, cw, cg, cos, sin, layer, tm):
    B, S, D = x.shape
    tok = lambda w: pl.BlockSpec((1, tm, w), lambda b, i: (b, i, 0))
    tok_t = lambda r: pl.BlockSpec((1, r, tm), lambda b, i: (b, 0, i))
    row = lambda w: _row_spec(w, layer)
    wspec = lambda r, c: _resident((None, r, c), lambda b, i: (layer, 0, 0))
    return pl.pallas_call(
        functools.partial(_proj_kernel, tm=tm),
        grid=(B, S // tm),
        in_specs=[tok(D),
                  _mod_spec(layer),
                  row(D), wspec(D, P_WIDTH), wspec(QK_ROPE, D),
                  row(Q_RANK), wspec(N_HEADS * HEAD_SLAB, Q_RANK),
                  row(KV_RANK), wspec(KV_RANK, N_HEADS * HEAD_SLAB), wspec(ATTN_W, KV_RANK),
                  pl.BlockSpec((None, 3, CONV_W), lambda b, i: (layer, 0, 0)),
                  row(CONV_W), tok_t(QK_ROPE), tok_t(QK_ROPE)],
        out_specs=[pl.BlockSpec((1, N_HEADS, HEAD_SLAB, tm), lambda b, i: (b, 0, 0, i)),
                   pl.BlockSpec((1, N_HEADS, tm, HEAD_SLAB), lambda b, i: (b, 0, i, 0)),
                   tok_t(ATTN_W), tok(CONV_W)],
        out_shape=[jax.ShapeDtypeStruct((B, N_HEADS, HEAD_SLAB, S), BF16),
                   jax.ShapeDtypeStruct((B, N_HEADS, S, HEAD_SLAB), BF16),
                   jax.ShapeDtypeStruct((B, ATTN_W, S), BF16),
                   jax.ShapeDtypeStruct((B, S, CONV_W), BF16)],
        scratch_shapes=[pltpu.VMEM((HALO + tm, CONV_W), F32)],
        compiler_params=_params(2),
        name="mixer_proj",
    )(x, mod_l, g, win, wkpe, qg, wuq, kvg, wk, wv, cw, cg, cos, sin)


def _attn_kernel(qt_ref, k_ref, vt_ref, o_ref, s_ref, acc_ref, cm_ref, *, tq, n_tiles):
    i = pl.program_id(2)
    hb = MXU_TILE
    krow = lax.broadcasted_iota(jnp.int32, (hb, hb), 0)
    qcol = lax.broadcasted_iota(jnp.int32, (hb, hb), 1)
    tri = krow <= qcol
    zpad = jnp.zeros((hb, hb - HEAD_SLAB), BF16)
    zrows = jnp.zeros((hb - HEAD_SLAB, hb), BF16)
    ones = jnp.ones((VALUE_ROWS - V_DIM, hb), BF16)
    heads = range(HEADS_PER_STEP)

    def q_half(h, tile, qh):
        start = pl.multiple_of(tile * tq + qh * hb, hb)
        return jnp.concatenate([qt_ref[0, h, :, pl.ds(start, hb)], zrows], axis=0)

    def k_half(h, blk, kh):
        start = pl.multiple_of(blk * tq + kh * hb, hb)
        return jnp.concatenate([k_ref[0, h, pl.ds(start, hb), :], zpad], axis=1)

    def v_half(h, blk, kh):
        start = pl.multiple_of(blk * tq + kh * hb, hb)
        return jnp.concatenate([vt_ref[0, h * V_DIM:(h + 1) * V_DIM, pl.ds(start, hb)], ones], axis=0)

    def quad_ref(h, kh, qh):
        return s_ref.at[h, kh * hb:(kh + 1) * hb, qh * hb:(qh + 1) * hb]

    def prestage(h, tile):
        pltpu.matmul_push_rhs(q_half(h, tile, 0), 0, h)

    def drain_stage(h):
        pltpu.matmul_acc_lhs(SCORE_SLOTS[1], jnp.zeros((16, hb), BF16), h, 0)
        pltpu.matmul_pop(SCORE_SLOTS[1], (16, hb), F32, h)

    def issue3(h, tile, blk):
        pltpu.matmul_acc_lhs(SCORE_SLOTS[0], k_half(h, blk, 0), h, 0)
        pltpu.matmul_acc_lhs(SCORE_SLOTS[1], k_half(h, blk, 1), h, None)
        pltpu.matmul_push_rhs(q_half(h, tile, 1), 0, h)
        pltpu.matmul_acc_lhs(SCORE_SLOTS[2], k_half(h, blk, 0), h, 0)

    def issue4(h, blk):
        pltpu.matmul_acc_lhs(SCORE_SLOTS[0], k_half(h, blk, 1), h, None)

    def pop_quad(h, slot, kh, qh, diag):
        st = pltpu.matmul_pop(slot, (hb, hb), F32, h)
        if diag:
            if kh > qh:
                st = jnp.full((hb, hb), -jnp.inf, F32)
            elif kh == qh:
                st = jnp.where(tri, st, -jnp.inf)
        quad_ref(h, kh, qh)[...] = st
        return jnp.max(st, axis=0, keepdims=True)

    def numer(h, kh, qh, m_new, reg):
        mq = m_new[:, qh * hb:(qh + 1) * hb]
        p = jnp.exp2(quad_ref(h, kh, qh)[...] - mq).astype(BF16)
        pltpu.matmul_push_rhs(p, reg, h)

    def pv(h, blk, qh):
        pltpu.matmul_acc_lhs(OUT_SLOTS[qh], v_half(h, blk, 0), h, 1)
        pltpu.matmul_acc_lhs(OUT_SLOTS[qh], v_half(h, blk, 1), h, 0)

    def accumulate(h, alpha):
        out = jnp.concatenate([pltpu.matmul_pop(OUT_SLOTS[qh], (VALUE_ROWS, hb), F32, h) for qh in range(2)], axis=1)
        acc_ref[h] = alpha * acc_ref[h] + out

    def col_max(c):
        return jnp.concatenate([jnp.maximum(c[0], c[1]), jnp.maximum(c[2], c[3])], axis=1)

    def order(t):
        return jnp.where(t == 0, i, t - 1)

    @pl.when(i == 0)
    def _():
        for h in heads:
            drain_stage(h)
            prestage(h, 0)
            issue3(h, 0, 0)
        for h in heads:
            c = [pop_quad(h, SCORE_SLOTS[0], 0, 0, True)]
            issue4(h, 0)
            c.append(pop_quad(h, SCORE_SLOTS[1], 1, 0, True))
            c.append(pop_quad(h, SCORE_SLOTS[2], 0, 1, True))
            c.append(pop_quad(h, SCORE_SLOTS[0], 1, 1, True))
            cm_ref[h] = col_max(c)
            prestage(h, min(1, n_tiles - 1))

    def step(t, state, nxt, staged_tile=None):
        blk = order(t)
        for h in heads:
            if nxt is not None:
                issue3(h, nxt[0], nxt[1])
            else:
                drain_stage(h)
        ms, alphas, cs = [], [], [[] for _ in heads]
        for h in heads:
            m, cm, _ = state[h]
            m_new = jnp.maximum(m, cm)
            ms.append(m_new)
            alphas.append(jnp.exp2(m - m_new))
        for h in heads:
            numer(h, 0, 0, ms[h], 1)
        for h in heads:
            if nxt is not None:
                cs[h].append(pop_quad(h, SCORE_SLOTS[0], 0, 0, nxt[2]))
                issue4(h, nxt[1])
            accumulate(h, state[h][2])
        for h in heads:
            numer(h, 1, 0, ms[h], 0)
            pv(h, blk, 0)
        if nxt is not None:
            for h in heads:
                cs[h].append(pop_quad(h, SCORE_SLOTS[1], 1, 0, nxt[2]))
        for h in heads:
            numer(h, 0, 1, ms[h], 1)
            numer(h, 1, 1, ms[h], 0)
            pv(h, blk, 1)
        out = []
        for h in heads:
            if nxt is not None:
                cs[h].append(pop_quad(h, SCORE_SLOTS[2], 0, 1, nxt[2]))
                cs[h].append(pop_quad(h, SCORE_SLOTS[0], 1, 1, nxt[2]))
                prestage(h, nxt[0] if staged_tile is None else staged_tile)
                out.append((ms[h], col_max(cs[h]), alphas[h]))
            else:
                out.append((ms[h], state[h][1], alphas[h]))
        return tuple(out)

    for h in heads:
        acc_ref[h] = jnp.zeros((VALUE_ROWS, tq), F32)
    state = tuple((jnp.full((1, tq), -jnp.inf, F32), cm_ref[h], jnp.ones((1, tq), F32)) for h in heads)
    next_tile = jnp.minimum(i + 1, n_tiles - 1)
    state = lax.fori_loop(
        0, i, lambda t, st: step(t, st, (i, t, False), jnp.where(t == i - 1, next_tile, i)), state)

    def finish(nxt):
        last = step(i, state, nxt)
        outs = []
        for h in heads:
            if nxt is not None:
                cm_ref[h] = last[h][1]
            accumulate(h, last[h][2])
            acc = acc_ref[h]
            outs.append(acc[0:V_DIM, :] / acc[V_DIM:V_DIM + 1, :])
        o_ref[0] = jnp.concatenate(outs, axis=0).T

    @pl.when(i < n_tiles - 1)
    def _():
        finish((i + 1, i + 1, True))

    @pl.when(i == n_tiles - 1)
    def _():
        finish(None)


def _attention(qt, k, vt, tq):
    B, H, S, _ = k.shape
    hps = HEADS_PER_STEP
    assert tq == 2 * MXU_TILE and HEAD_SLAB <= MXU_TILE
    return pl.pallas_call(
        functools.partial(_attn_kernel, tq=tq, n_tiles=S // tq),
        grid=(B, H // hps, S // tq),
        in_specs=[pl.BlockSpec((1, hps, HEAD_SLAB, S), lambda b, p, i: (b, p, 0, 0)),
                  pl.BlockSpec((1, hps, S, HEAD_SLAB), lambda b, p, i: (b, p, 0, 0)),
                  pl.BlockSpec((1, hps * V_DIM, S), lambda b, p, i: (b, p, 0))],
        out_specs=pl.BlockSpec((1, tq, hps * V_DIM), lambda b, p, i: (b, i, p)),
        out_shape=jax.ShapeDtypeStruct((B, S, ATTN_W), F32),
        scratch_shapes=[pltpu.VMEM((hps, tq, tq), F32),
                        pltpu.VMEM((hps, VALUE_ROWS, tq), F32),
                        pltpu.VMEM((hps, 1, tq), F32)],
        compiler_params=_params(3),
        name="mla_attention",
    )(qt, k, vt)


def _layout_w_in(w_in):
    lat = Q_RANK + KV_RANK
    main = jnp.concatenate([w_in[:, :, :lat], w_in[:, :, lat + QK_ROPE:]], axis=-1)
    kpe_t = jnp.swapaxes(w_in[:, :, lat:lat + QK_ROPE], 1, 2)
    return main.astype(BF16), kpe_t.astype(BF16)


def _layout_w_uq(w_uq):
    L, R, _ = w_uq.shape
    w = w_uq.reshape(L, R, N_HEADS, QK_NOPE + QK_ROPE)
    z_pad = jnp.zeros((L, R, N_HEADS, HEAD_SLAB - QK_NOPE - QK_ROPE), w.dtype)
    q = jnp.concatenate([w, z_pad], axis=-1).reshape(L, R, N_HEADS * HEAD_SLAB)
    return jnp.swapaxes(q, 1, 2).astype(BF16)


def _layout_w_ukv(w_ukv):
    L, R, _ = w_ukv.shape
    w = w_ukv.reshape(L, R, N_HEADS, QK_NOPE + V_DIM)
    k_nope, v = w[..., :QK_NOPE], w[..., QK_NOPE:]
    z = jnp.zeros((L, R, N_HEADS, HEAD_SLAB - QK_NOPE), w.dtype)
    wk = jnp.concatenate([k_nope, z], axis=-1).reshape(L, R, N_HEADS * HEAD_SLAB)
    wv_t = jnp.swapaxes(v.reshape(L, R, ATTN_W), 1, 2)
    return wk.astype(BF16), wv_t.astype(BF16)


def kernel(x, c, positions, w_ada, b_ada, ffn1_norm, ffn1_w_gu, ffn1_w_down, mix_norm, w_in, q_a_norm, w_uq, kv_a_norm, w_ukv, conv_w, attn_out_norm, conv_out_norm, w_o, ffn2_norm, ffn2_w_gu, ffn2_w_down, final_norm):
    B, S, D = x.shape
    L = w_ada.shape[0]
    tm = min(S, 512)
    tq = min(S, 512)

    mod = _modulation(c, w_ada, b_ada)
    cos, sin = _rope_tables(positions)

    wgu1, wd1 = ffn1_w_gu.astype(BF16), ffn1_w_down.astype(BF16)
    wgu2, wd2 = ffn2_w_gu.astype(BF16), ffn2_w_down.astype(BF16)
    win, wkpe_t = _layout_w_in(w_in)
    wuq_t = _layout_w_uq(w_uq)
    wk, wv_t = _layout_w_ukv(w_ukv)
    wo = w_o.astype(BF16)
    final_g = final_norm.reshape(1, D)
    rows = lambda g: g.reshape(L, 1, g.shape[-1])
    g1, g2, gm = rows(ffn1_norm), rows(ffn2_norm), rows(mix_norm)
    gq, gkv = rows(q_a_norm), rows(kv_a_norm)
    ga, gc = rows(attn_out_norm), rows(conv_out_norm)

    for l in range(L):
        x = _ffn(x, mod, g1, wgu1, wd1, l, 0, tm)
        qt, k, vt, conv = _proj(x, mod, gm, win, wkpe_t, gq, wuq_t, gkv, wk, wv_t, conv_w, gc, cos, sin, l, tm)
        attn = _attention(qt, k, vt, tq)
        x = _ffn(x, mod, g2, wgu2, wd2, l, 6, tm, mixer=(attn, conv, ga, wo),
                 final_g=final_g if l == L - 1 else None)
    return x
```

```python
import functools

import jax
import jax.numpy as jnp
from jax import lax
from jax.experimental import pallas as pl
from jax.experimental.pallas import tpu as pltpu

F32 = jnp.float32
BF16 = jnp.bfloat16

D_MODEL = 1024
N_HEADS = 8
QK_NOPE = 64
QK_ROPE = 32
V_DIM = 64
Q_RANK = 384
KV_RANK = 256
CONV_W = 512
ATTN_W = N_HEADS * V_DIM
D_FF = 2816
N_MOD = 9
EPS = 1e-6
ROPE_THETA = 10000.0
HEAD_SLAB = 128
HALF_ROPE = QK_ROPE // 2
P_CQ = 0
P_CKV = P_CQ + Q_RANK
P_GB = P_CKV + KV_RANK
P_WIDTH = P_GB + 3 * CONV_W
HALO = 8
Q_SCALE = float((QK_NOPE + QK_ROPE) ** -0.5 * 1.4426950408889634)
VMEM_LIMIT = 56 * 1024 * 1024
FFN_ROW_GROUPS = 2
HEADS_PER_STEP = 2
MXU_TILE = 256
VALUE_ROWS = V_DIM + 16
SCORE_SLOTS = (0, MXU_TILE // 4, 2 * MXU_TILE // 4)
OUT_SLOTS = (3 * MXU_TILE // 4, 3 * MXU_TILE // 4 + VALUE_ROWS // 4)
NT_DIMS = (((1,), (1,)), ((), ()))


def _rms(x, g):
    return x * lax.rsqrt(jnp.mean(x * x, axis=-1, keepdims=True) + EPS) * g


def _silu(x):
    return x / (1.0 + jnp.exp(-x))


def _params(n_axes):
    return pltpu.CompilerParams(dimension_semantics=("arbitrary",) * n_axes,
                                vmem_limit_bytes=VMEM_LIMIT)


def _resident(shape, index_map):
    return pl.BlockSpec(shape, index_map, pipeline_mode=pl.Buffered(1))


def _row_spec(width, layer):
    return pl.BlockSpec((None, 1, width), lambda *_: (layer, 0, 0))


def _mod_spec(layer):
    return pl.BlockSpec((None, 1, N_MOD, D_MODEL), lambda b, *_: (layer, b, 0, 0))


def _mod_kernel(c_ref, w_ref, b_ref, o_ref):
    c = c_ref[...]
    ca = _silu(c).astype(BF16)
    o_ref[0] = jnp.dot(ca, w_ref[0].astype(BF16), preferred_element_type=F32) + b_ref[0]


def _modulation(c, w_ada, b_ada):
    L, D, N = w_ada.shape
    B = c.shape[0]
    tn = N // 8
    out = pl.pallas_call(
        _mod_kernel,
        grid=(L, N // tn),
        in_specs=[pl.BlockSpec((B, D), lambda l, n: (0, 0)),
                  pl.BlockSpec((1, D, tn), lambda l, n: (l, 0, n)),
                  pl.BlockSpec((1, 1, tn), lambda l, n: (l, 0, n))],
        out_specs=pl.BlockSpec((1, B, tn), lambda l, n: (l, 0, n)),
        out_shape=jax.ShapeDtypeStruct((L, B, N), F32),
        compiler_params=_params(2),
        name="adaln_mod",
    )(c, w_ada, b_ada.reshape(L, 1, N))
    return out.reshape(L, B, N_MOD, D)


def _rope_kernel(pos_ref, invf_ref, sgn_ref, cos_ref, sin_ref):
    ang = invf_ref[...] * pos_ref[0].astype(F32)
    cos_ref[0] = jnp.cos(ang)
    sin_ref[0] = jnp.sin(ang) * sgn_ref[...]


def _rope_tables(positions):
    B, S = positions.shape
    ts = min(S, 2048)
    inv_freq = 1.0 / (ROPE_THETA ** (jnp.arange(0, QK_ROPE, 2, dtype=F32) / QK_ROPE))
    invf = jnp.concatenate([inv_freq, inv_freq])[:, None]
    ones = jnp.ones((HALF_ROPE,), F32)
    sgn = jnp.concatenate([-ones, ones])[:, None]
    tab = jax.ShapeDtypeStruct((B, QK_ROPE, S), F32)
    col = pl.BlockSpec((QK_ROPE, 1), lambda b, i: (0, 0))
    out = pl.BlockSpec((1, QK_ROPE, ts), lambda b, i: (b, 0, i))
    return pl.pallas_call(
        _rope_kernel,
        grid=(B, S // ts),
        in_specs=[pl.BlockSpec((1, 1, ts), lambda b, i: (b, 0, i)), col, col],
        out_specs=[out, out],
        out_shape=[tab, tab],
        compiler_params=_params(2),
        name="rope_tables",
    )(positions.reshape(B, 1, S), invf, sgn)


def _swiglu_rows(x, mod_ref, mod_base, g_ref, wgu_ref, wd_ref):
    shift = mod_ref[0, mod_base:mod_base + 1, :]
    scale = mod_ref[0, mod_base + 1:mod_base + 2, :]
    gate = mod_ref[0, mod_base + 2:mod_base + 3, :]
    h = (_rms(x, g_ref[...]) * (1.0 + scale) + shift).astype(BF16)
    gu = jnp.dot(h, wgu_ref[...], preferred_element_type=F32)
    a = (_silu(gu[:, :D_FF]) * gu[:, D_FF:]).astype(BF16)
    y = jnp.dot(a, wd_ref[...], preferred_element_type=F32)
    return x + (0.5 * (1.0 + gate)) * y


def _merge_ffn_kernel(x_ref, mod_ref, attn_ref, conv_ref, ag_ref, wo_ref, g_ref, wgu_ref, wd_ref, *rest, final):
    o_ref = rest[-1]
    an = _rms(attn_ref[0], ag_ref[...]).astype(BF16)
    ym = jnp.dot(an, wo_ref[0:ATTN_W, :], preferred_element_type=F32)
    ym = ym + jnp.dot(conv_ref[0], wo_ref[ATTN_W:, :], preferred_element_type=F32)
    x = x_ref[0] + (1.0 + mod_ref[0, 5:6, :]) * ym
    out = _swiglu_rows(x, mod_ref, 6, g_ref, wgu_ref, wd_ref)
    if final:
        out = _rms(out, rest[0][...])
    o_ref[0] = out


def _merge_ffn(x, mod_l, attn, conv, ag, wo, g, wgu, wd, layer, tm, final_g=None):
    B, S, D = x.shape
    tok = lambda w: pl.BlockSpec((1, tm, w), lambda b, i: (b, i, 0))
    wspec = lambda r, c: _resident((None, r, c), lambda b, i: (layer, 0, 0))
    in_specs = [tok(D), _mod_spec(layer), tok(ATTN_W), tok(CONV_W), _row_spec(ATTN_W, layer),
                wspec(ATTN_W + CONV_W, D), _row_spec(D, layer), wspec(D, 2 * D_FF), wspec(D_FF, D)]
    args = [x, mod_l, attn, conv, ag, wo, g, wgu, wd]
    if final_g is not None:
        in_specs.append(pl.BlockSpec((1, D), lambda b, i: (0, 0)))
        args.append(final_g)
    return pl.pallas_call(
        functools.partial(_merge_ffn_kernel, final=final_g is not None),
        grid=(B, S // tm),
        in_specs=in_specs,
        out_specs=tok(D),
        out_shape=jax.ShapeDtypeStruct((B, S, D), F32),
        compiler_params=_params(2),
        name="merge_ffn_final" if final_g is not None else "merge_ffn",
    )(*args)


def _rope_t(rows, cos, sin):
    partner = jnp.concatenate([rows[HALF_ROPE:], rows[:HALF_ROPE]], axis=0)
    return rows * cos + partner * sin


def _ffn_proj_kernel(x_ref, mod_ref, g1_ref, wgu_ref, wd_ref,
                     g_ref, win_ref, wkpe_ref, qg_ref, wuq_ref, kvg_ref, wk_ref, wv_ref,
                     cw_ref, cg_ref, cos_ref, sin_ref,
                     x1_ref, qt_ref, k_ref, vt_ref, conv_ref, hist_ref, *, tm):
    @pl.when(pl.program_id(1) == 0)
    def _():
        hist_ref[0:HALO, :] = jnp.zeros((HALO, CONV_W), F32)

    for r in range(FFN_ROW_GROUPS):
        rows = slice(r * tm // FFN_ROW_GROUPS, (r + 1) * tm // FFN_ROW_GROUPS)
        x1_ref[0, rows, :] = _swiglu_rows(x_ref[0, rows, :], mod_ref, 0, g1_ref, wgu_ref, wd_ref)

    x = x1_ref[0]
    shift = mod_ref[0, 3:4, :]
    scale = mod_ref[0, 4:5, :]
    h = (_rms(x, g_ref[...]) * (1.0 + scale) + shift).astype(BF16)
    proj = jnp.dot(h, win_ref[:, 0:P_GB], preferred_element_type=F32)
    cos = cos_ref[0]
    sin = sin_ref[0]
    rope_lo, rope_hi = QK_NOPE, QK_NOPE + QK_ROPE

    cqn = _rms(proj[:, P_CQ:P_CQ + Q_RANK], qg_ref[...]).astype(BF16)
    qt = lax.dot_general(wuq_ref[...], cqn, NT_DIMS, preferred_element_type=F32)
    gates = jnp.dot(h, win_ref[:, P_GB:P_WIDTH], preferred_element_type=F32)
    ckvn = _rms(proj[:, P_CKV:P_CKV + KV_RANK], kvg_ref[...]).astype(BF16)
    kk = jnp.dot(ckvn, wk_ref[...], preferred_element_type=F32)
    vt = lax.dot_general(wv_ref[...], ckvn, NT_DIMS, preferred_element_type=F32)
    kpe_t = lax.dot_general(wkpe_ref[...], h, NT_DIMS, preferred_element_type=F32)
    kpe_t = jnp.concatenate([jnp.zeros((rope_lo, tm), F32), _rope_t(kpe_t, cos, sin),
                             jnp.zeros((HEAD_SLAB - rope_hi, tm), F32)], axis=0)
    kpe = kpe_t.T
    for hd in range(N_HEADS):
        lo = hd * HEAD_SLAB
        qh = qt[lo:lo + HEAD_SLAB]
        qh = jnp.concatenate([qh[:rope_lo], _rope_t(qh[rope_lo:rope_hi], cos, sin), qh[rope_hi:]], axis=0)
        qt_ref[0, hd] = (qh * Q_SCALE).astype(BF16)
        k_ref[0, hd] = (kk[:, lo:lo + HEAD_SLAB] + kpe).astype(BF16)
    vt_ref[0] = vt.astype(BF16)

    cv =gates[:, CONV_W:2 * CONV_W] * gates[:, 2 * CONV_W:3 * CONV_W]
    hist_ref[HALO:HALO + tm, :] = cv
    cv1 = hist_ref[HALO - 1:HALO - 1 + tm, :]
    cv2 = hist_ref[HALO - 2:HALO - 2 + tm, :]
    conv = cw_ref[0:1, :] * cv2 + cw_ref[1:2, :] * cv1 + cw_ref[2:3, :] * cv
    conv = gates[:, 0:CONV_W] * conv
    conv_ref[0] = _rms(conv, cg_ref[...]).astype(BF16)
    hist_ref[0:HALO, :] = cv[tm - HALO:, :]


def _ffn_proj(x, mod_l, g1, wgu, wd, g, win, wkpe, qg, wuq, kvg, wk, wv, cw, cg, cos, sin, layer, tm):
    B, S, D = x.shape
    tok = lambda w: pl.BlockSpec((1, tm, w), lambda b, i: (b, i, 0))
    tok_t = lambda r: pl.BlockSpec((1, r, tm), lambda b, i: (b, 0, i))
    row = lambda w: _row_spec(w, layer)
    wspec = lambda r, c: _resident((None, r, c), lambda b, i: (layer, 0, 0))
    return pl.pallas_call(
        functools.partial(_ffn_proj_kernel, tm=tm),
        grid=(B, S // tm),
        in_specs=[tok(D),
                  _mod_spec(layer),
                  row(D), wspec(D, 2 * D_FF), wspec(D_FF, D),
                  row(D), wspec(D, P_WIDTH), wspec(QK_ROPE, D),
                  row(Q_RANK), wspec(N_HEADS * HEAD_SLAB, Q_RANK),
                  row(KV_RANK), wspec(KV_RANK, N_HEADS * HEAD_SLAB), wspec(ATTN_W, KV_RANK),
                  pl.BlockSpec((None, 3, CONV_W), lambda b, i: (layer, 0, 0)),
                  row(CONV_W), tok_t(QK_ROPE), tok_t(QK_ROPE)],
        out_specs=[tok(D),
                   pl.BlockSpec((1, N_HEADS, HEAD_SLAB, tm), lambda b, i: (b, 0, 0, i)),
                   pl.BlockSpec((1, N_HEADS, tm, HEAD_SLAB), lambda b, i: (b, 0, i, 0)),
                   tok_t(ATTN_W), tok(CONV_W)],
        out_shape=[jax.ShapeDtypeStruct((B, S, D), F32),
                   jax.ShapeDtypeStruct((B, N_HEADS, HEAD_SLAB, S), BF16),
                   jax.ShapeDtypeStruct((B, N_HEADS, S, HEAD_SLAB), BF16),
                   jax.ShapeDtypeStruct((B, ATTN_W, S), BF16),
                   jax.ShapeDtypeStruct((B, S, CONV_W), BF16)],
        scratch_shapes=[pltpu.VMEM((HALO + tm, CONV_W), F32)],
        compiler_params=_params(2),
        name="ffn_proj",
    )(x, mod_l, g1, wgu, wd, g, win, wkpe, qg, wuq, kvg, wk, wv, cw, cg, cos, sin)


def _attn_kernel(qt_ref, k_ref, vt_ref, o_ref, s_ref, acc_ref, cm_ref, *, tq, n_tiles):
    i = pl.program_id(2)
    hb = MXU_TILE
    krow = lax.broadcasted_iota(jnp.int32, (hb, hb), 0)
    qcol = lax.broadcasted_iota(jnp.int32, (hb, hb), 1)
    tri = krow <= qcol
    zpad = jnp.zeros((hb, hb - HEAD_SLAB), BF16)
    zrows = jnp.zeros((hb - HEAD_SLAB, hb), BF16)
    ones = jnp.ones((VALUE_ROWS - V_DIM, hb), BF16)
    heads = range(HEADS_PER_STEP)

    def q_half(h, tile, qh):
        start = pl.multiple_of(tile * tq + qh * hb, hb)
        return jnp.concatenate([qt_ref[0, h, :, pl.ds(start, hb)], zrows], axis=0)

    def k_half(h, blk, kh):
        start = pl.multiple_of(blk * tq + kh * hb, hb)
        return jnp.concatenate([k_ref[0, h, pl.ds(start, hb), :], zpad], axis=1)

    def v_half(h, blk, kh):
        start = pl.multiple_of(blk * tq + kh * hb, hb)
        return jnp.concatenate([vt_ref[0, h * V_DIM:(h + 1) * V_DIM, pl.ds(start, hb)], ones], axis=0)

    def quad_ref(h, kh, qh):
        return s_ref.at[h, kh * hb:(kh + 1) * hb, qh * hb:(qh + 1) * hb]

    def prestage(h, tile):
        pltpu.matmul_push_rhs(q_half(h, tile, 0), 0, h)

    def drain_stage(h):
        pltpu.matmul_acc_lhs(SCORE_SLOTS[1], jnp.zeros((16, hb), BF16), h, 0)
        pltpu.matmul_pop(SCORE_SLOTS[1], (16, hb), F32, h)

    def issue3(h, tile, blk):
        pltpu.matmul_acc_lhs(SCORE_SLOTS[0], k_half(h, blk, 0), h, 0)
        pltpu.matmul_acc_lhs(SCORE_SLOTS[1], k_half(h, blk, 1), h, None)
        pltpu.matmul_push_rhs(q_half(h, tile, 1), 0, h)
        pltpu.matmul_acc_lhs(SCORE_SLOTS[2], k_half(h, blk, 0), h, 0)

    def issue4(h, blk):
        pltpu.matmul_acc_lhs(SCORE_SLOTS[0], k_half(h, blk, 1), h, None)

    def pop_quad(h, slot, kh, qh, diag):
        st = pltpu.matmul_pop(slot, (hb, hb), F32, h)
        if diag:
            if kh > qh:
                st = jnp.full((hb, hb), -jnp.inf, F32)
            elif kh == qh:
                st = jnp.where(tri, st, -jnp.inf)
        quad_ref(h, kh, qh)[...] = st
        return jnp.max(st, axis=0, keepdims=True)

    def numer(h, kh, qh, m_new, reg):
        mq = m_new[:, qh * hb:(qh + 1) * hb]
        p = jnp.exp2(quad_ref(h, kh, qh)[...] - mq).astype(BF16)
        pltpu.matmul_push_rhs(p, reg, h)

    def pv(h, blk, qh):
        pltpu.matmul_acc_lhs(OUT_SLOTS[qh], v_half(h, blk, 0), h, 1)
        pltpu.matmul_acc_lhs(OUT_SLOTS[qh], v_half(h, blk, 1), h, 0)

    def accumulate(h, alpha):
        out = jnp.concatenate([pltpu.matmul_pop(OUT_SLOTS[qh], (VALUE_ROWS, hb), F32, h) for qh in range(2)], axis=1)
        acc_ref[h] = alpha * acc_ref[h] + out

    def col_max(c):
        return jnp.concatenate([jnp.maximum(c[0], c[1]), jnp.maximum(c[2], c[3])], axis=1)

    def order(t):
        return jnp.where(t == 0, i, t - 1)

    @pl.when(i == 0)
    def _():
        for h in heads:
            drain_stage(h)
            prestage(h, 0)
            issue3(h, 0, 0)
        for h in heads:
            c = [pop_quad(h, SCORE_SLOTS[0], 0, 0, True)]
            issue4(h, 0)
            c.append(pop_quad(h, SCORE_SLOTS[1], 1, 0, True))
            c.append(pop_quad(h, SCORE_SLOTS[2], 0, 1, True))
            c.append(pop_quad(h, SCORE_SLOTS[0], 1, 1, True))
            cm_ref[h] = col_max(c)
            prestage(h, min(1, n_tiles - 1))

    def step(t, state, nxt, staged_tile=None):
        blk = order(t)
        ms, alphas, cs = [], [], [[] for _ in heads]
        for h in heads:
            m, cm, _ = state[h]
            m_new = jnp.maximum(m, cm)
            ms.append(m_new)
            alphas.append(jnp.exp2(m - m_new))
        for h in heads:
            numer(h, 0, 0, ms[h], 1)
        for h in heads:
            if nxt is not None:
                issue3(h, nxt[0], nxt[1])
            else:
                drain_stage(h)
        for h in heads:
            if nxt is not None:
                cs[h].append(pop_quad(h, SCORE_SLOTS[0], 0, 0, nxt[2]))
                issue4(h, nxt[1])
        for h in heads:
            numer(h, 1, 0, ms[h], 0)
        if nxt is not None:
            for h in heads:
                cs[h].append(pop_quad(h, SCORE_SLOTS[1], 1, 0, nxt[2]))
        for h in heads:
            accumulate(h, state[h][2])
            pv(h, blk, 0)
        for h in heads:
            numer(h, 0, 1, ms[h], 1)
            numer(h, 1, 1, ms[h], 0)
            pv(h, blk, 1)
        out = []
        for h in heads:
            if nxt is not None:
                cs[h].append(pop_quad(h, SCORE_SLOTS[2], 0, 1, nxt[2]))
                cs[h].append(pop_quad(h, SCORE_SLOTS[0], 1, 1, nxt[2]))
                prestage(h, nxt[0] if staged_tile is None else staged_tile)
                out.append((ms[h], col_max(cs[h]), alphas[h]))
            else:
                out.append((ms[h], state[h][1], alphas[h]))
        return tuple(out)

    for h in heads:
        acc_ref[h] = jnp.zeros((VALUE_ROWS, tq), F32)
    state = tuple((jnp.full((1, tq), -jnp.inf, F32), cm_ref[h], jnp.ones((1, tq), F32)) for h in heads)
    next_tile = jnp.minimum(i + 1, n_tiles - 1)
    state = lax.fori_loop(
        0, i, lambda t, st: step(t, st, (i, t, False), jnp.where(t == i - 1, next_tile, i)), state)

    def finish(nxt):
        last = step(i, state, nxt)
        outs = []
        for h in heads:
            if nxt is not None:
                cm_ref[h] = last[h][1]
            accumulate(h, last[h][2])
            acc = acc_ref[h]
            outs.append(acc[0:V_DIM, :] / acc[V_DIM:V_DIM + 1, :])
        o_ref[0] = jnp.concatenate(outs, axis=0).T

    @pl.when(i < n_tiles - 1)
    def _():
        finish((i + 1, i + 1, True))

    @pl.when(i == n_tiles - 1)
    def _():
        finish(None)


def _attention(qt, k, vt, tq):
    B, H, S, _ = k.shape
    hps = HEADS_PER_STEP
    assert tq == 2 * MXU_TILE and HEAD_SLAB <= MXU_TILE
    return pl.pallas_call(
        functools.partial(_attn_kernel, tq=tq, n_tiles=S // tq),
        grid=(B, H // hps, S // tq),
        in_specs=[pl.BlockSpec((1, hps, HEAD_SLAB, S), lambda b, p, i: (b, p, 0, 0)),
                  pl.BlockSpec((1, hps, S, HEAD_SLAB), lambda b, p, i: (b, p, 0, 0)),
                  pl.BlockSpec((1, hps * V_DIM, S), lambda b, p, i: (b, p, 0))],
        out_specs=pl.BlockSpec((1, tq, hps * V_DIM), lambda b, p, i: (b, i, p)),
        out_shape=jax.ShapeDtypeStruct((B, S, ATTN_W), F32),
        scratch_shapes=[pltpu.VMEM((hps, tq, tq), F32),
                        pltpu.VMEM((hps, VALUE_ROWS, tq), F32),
                        pltpu.VMEM((hps, 1, tq), F32)],
        compiler_params=_params(3),
        name="mla_attention",
    )(qt, k, vt)


def _layout_w_in(w_in):
    lat = Q_RANK + KV_RANK
    main = jnp.concatenate([w_in[:, :, :lat], w_in[:, :, lat + QK_ROPE:]], axis=-1)
    kpe_t = jnp.swapaxes(w_in[:, :, lat:lat + QK_ROPE], 1, 2)
    return main.astype(BF16), kpe_t.astype(BF16)


def _layout_w_uq(w_uq):
    L, R, _ = w_uq.shape
    w = w_uq.reshape(L, R, N_HEADS, QK_NOPE + QK_ROPE)
    z_pad = jnp.zeros((L, R, N_HEADS, HEAD_SLAB - QK_NOPE - QK_ROPE), w.dtype)
    q = jnp.concatenate([w, z_pad], axis=-1).reshape(L, R, N_HEADS * HEAD_SLAB)
    return jnp.swapaxes(q, 1, 2).astype(BF16)


def _layout_w_ukv(w_ukv):
    L, R, _ = w_ukv.shape
    w = w_ukv.reshape(L, R, N_HEADS, QK_NOPE + V_DIM)
    k_nope, v = w[..., :QK_NOPE], w[..., QK_NOPE:]
    z = jnp.zeros((L, R, N_HEADS, HEAD_SLAB - QK_NOPE), w.dtype)
    wk = jnp.concatenate([k_nope, z], axis=-1).reshape(L, R, N_HEADS * HEAD_SLAB)
    wv_t = jnp.swapaxes(v.reshape(L, R, ATTN_W), 1, 2)
    return wk.astype(BF16), wv_t.astype(BF16)


def kernel(x, c, positions, w_ada, b_ada, ffn1_norm, ffn1_w_gu, ffn1_w_down, mix_norm, w_in, q_a_norm, w_uq, kv_a_norm, w_ukv, conv_w, attn_out_norm, conv_out_norm, w_o, ffn2_norm, ffn2_w_gu, ffn2_w_down, final_norm):
    B, S, D = x.shape
    L = w_ada.shape[0]
    tm = min(S, 512)
    tq = min(S, 512)

    mod = _modulation(c, w_ada, b_ada)
    cos, sin = _rope_tables(positions)

    wgu1, wd1 = ffn1_w_gu.astype(BF16), ffn1_w_down.astype(BF16)
    wgu2, wd2 = ffn2_w_gu.astype(BF16), ffn2_w_down.astype(BF16)
    win, wkpe_t = _layout_w_in(w_in)
    wuq_t = _layout_w_uq(w_uq)
    wk, wv_t = _layout_w_ukv(w_ukv)
    wo = w_o.astype(BF16)
    final_g = final_norm.reshape(1, D)
    rows = lambda g: g.reshape(L, 1, g.shape[-1])
    g1, g2, gm = rows(ffn1_norm), rows(ffn2_norm), rows(mix_norm)
    gq, gkv = rows(q_a_norm), rows(kv_a_norm)
    ga, gc = rows(attn_out_norm), rows(conv_out_norm)

    for l in range(L):
        x, qt, k, vt, conv = _ffn_proj(x, mod, g1, wgu1, wd1, gm, win, wkpe_t, gq, wuq_t, gkv, wk, wv_t,
                                       conv_w, gc, cos, sin, l, tm)
        attn = _attention(qt, k, vt, tq)
        x = _merge_ffn(x, mod, attn, conv, ga, wo, g2, wgu2, wd2, l, tm,
                       final_g=final_g if l == L - 1 else None)
    return x
```

```python
import functools

import jax
import jax.numpy as jnp
from jax import lax
from jax.experimental import pallas as pl
from jax.experimental.pallas import tpu as pltpu

F32 = jnp.float32
BF16 = jnp.bfloat16

D_MODEL = 1024
N_HEADS = 8
QK_NOPE = 64
QK_ROPE = 32
V_DIM = 64
Q_RANK = 384
KV_RANK = 256
CONV_W = 512
ATTN_W = N_HEADS * V_DIM
D_FF = 2816
N_MOD = 9
EPS = 1e-6
ROPE_THETA = 10000.0
HEAD_SLAB = 128
HALF_ROPE = QK_ROPE // 2
P_CQ = 0
P_CKV = P_CQ + Q_RANK
P_GB = P_CKV + KV_RANK
P_WIDTH = P_GB + 3 * CONV_W
HALO = 8
Q_SCALE = float((QK_NOPE + QK_ROPE) ** -0.5 * 1.4426950408889634)
VMEM_LIMIT = 56 * 1024 * 1024
FFN_ROW_GROUPS = 2
HEADS_PER_STEP = 2
TILES_PER_STEP = 2
MXU_TILE = 256
VALUE_ROWS = V_DIM + 16
SCORE_SLOTS = (0, MXU_TILE // 4, 2 * MXU_TILE // 4)
OUT_SLOTS = (3 * MXU_TILE // 4, 3 * MXU_TILE // 4 + VALUE_ROWS // 4)
NT_DIMS = (((1,), (1,)), ((), ()))


def _rms(x, g):
    return x * lax.rsqrt(jnp.mean(x * x, axis=-1, keepdims=True) + EPS) * g


def _silu(x):
    return x / (1.0 + jnp.exp(-x))


def _params(n_axes):
    return pltpu.CompilerParams(dimension_semantics=("arbitrary",) * n_axes,
                                vmem_limit_bytes=VMEM_LIMIT)


def _resident(shape, index_map):
    return pl.BlockSpec(shape, index_map, pipeline_mode=pl.Buffered(1))


def _row_spec(width, layer):
    return pl.BlockSpec((None, 1, width), lambda *_: (layer, 0, 0))


def _mod_spec(layer):
    return pl.BlockSpec((None, 1, N_MOD, D_MODEL), lambda b, *_: (layer, b, 0, 0))


def _mod_kernel(c_ref, w_ref, b_ref, o_ref):
    c = c_ref[...]
    ca = _silu(c).astype(BF16)
    o_ref[0] = jnp.dot(ca, w_ref[0].astype(BF16), preferred_element_type=F32) + b_ref[0]


def _modulation(c, w_ada, b_ada):
    L, D, N = w_ada.shape
    B = c.shape[0]
    tn = N // 8
    out = pl.pallas_call(
        _mod_kernel,
        grid=(L, N // tn),
        in_specs=[pl.BlockSpec((B, D), lambda l, n: (0, 0)),
                  pl.BlockSpec((1, D, tn), lambda l, n: (l, 0, n)),
                  pl.BlockSpec((1, 1, tn), lambda l, n: (l, 0, n))],
        out_specs=pl.BlockSpec((1, B, tn), lambda l, n: (l, 0, n)),
        out_shape=jax.ShapeDtypeStruct((L, B, N), F32),
        compiler_params=_params(2),
        name="adaln_mod",
    )(c, w_ada, b_ada.reshape(L, 1, N))
    return out.reshape(L, B, N_MOD, D)


def _rope_kernel(pos_ref, invf_ref, sgn_ref, cos_ref, sin_ref):
    ang = invf_ref[...] * pos_ref[0].astype(F32)
    cos_ref[0] = jnp.cos(ang)
    sin_ref[0] = jnp.sin(ang) * sgn_ref[...]


def _rope_tables(positions):
    B, S = positions.shape
    ts = min(S, 2048)
    inv_freq = 1.0 / (ROPE_THETA ** (jnp.arange(0, QK_ROPE, 2, dtype=F32) / QK_ROPE))
    invf = jnp.concatenate([inv_freq, inv_freq])[:, None]
    ones = jnp.ones((HALF_ROPE,), F32)
    sgn = jnp.concatenate([-ones, ones])[:, None]
    tab = jax.ShapeDtypeStruct((B, QK_ROPE, S), F32)
    col = pl.BlockSpec((QK_ROPE, 1), lambda b, i: (0, 0))
    out = pl.BlockSpec((1, QK_ROPE, ts), lambda b, i: (b, 0, i))
    return pl.pallas_call(
        _rope_kernel,
        grid=(B, S // ts),
        in_specs=[pl.BlockSpec((1, 1, ts), lambda b, i: (b, 0, i)), col, col],
        out_specs=[out, out],
        out_shape=[tab, tab],
        compiler_params=_params(2),
        name="rope_tables",
    )(positions.reshape(B, 1, S), invf, sgn)


def _swiglu_rows(x, mod_ref, mod_base, g_ref, wgu_ref, wd_ref):
    shift = mod_ref[0, mod_base:mod_base + 1, :]
    scale = mod_ref[0, mod_base + 1:mod_base + 2, :]
    gate = mod_ref[0, mod_base + 2:mod_base + 3, :]
    h = (_rms(x, g_ref[...]) * (1.0 + scale) + shift).astype(BF16)
    gu = jnp.dot(h, wgu_ref[...], preferred_element_type=F32)
    a = (_silu(gu[:, :D_FF]) * gu[:, D_FF:]).astype(BF16)
    y = jnp.dot(a, wd_ref[...], preferred_element_type=F32)
    return x + (0.5 * (1.0 + gate)) * y


def _merge_ffn_kernel(x_ref, mod_ref, attn_ref, conv_ref, ag_ref, wo_ref, g_ref, wgu_ref, wd_ref, *rest, final):
    o_ref = rest[-1]
    an = _rms(attn_ref[0], ag_ref[...]).astype(BF16)
    ym = jnp.dot(an, wo_ref[0:ATTN_W, :], preferred_element_type=F32)
    ym = ym + jnp.dot(conv_ref[0], wo_ref[ATTN_W:, :], preferred_element_type=F32)
    x = x_ref[0] + (1.0 + mod_ref[0, 5:6, :]) * ym
    out = _swiglu_rows(x, mod_ref, 6, g_ref, wgu_ref, wd_ref)
    if final:
        out = _rms(out, rest[0][...])
    o_ref[0] = out


def _merge_ffn(x, mod_l, attn, conv, ag, wo, g, wgu, wd, layer, tm, final_g=None):
    B, S, D = x.shape
    tok = lambda w: pl.BlockSpec((1, tm, w), lambda b, i: (b, i, 0))
    wspec = lambda r, c: _resident((None, r, c), lambda b, i: (layer, 0, 0))
    in_specs = [tok(D), _mod_spec(layer), tok(ATTN_W), tok(CONV_W), _row_spec(ATTN_W, layer),
                wspec(ATTN_W + CONV_W, D), _row_spec(D, layer), wspec(D, 2 * D_FF), wspec(D_FF, D)]
    args = [x, mod_l, attn, conv, ag, wo, g, wgu, wd]
    if final_g is not None:
        in_specs.append(pl.BlockSpec((1, D), lambda b, i: (0, 0)))
        args.append(final_g)
    return pl.pallas_call(
        functools.partial(_merge_ffn_kernel, final=final_g is not None),
        grid=(B, S // tm),
        in_specs=in_specs,
        out_specs=tok(D),
        out_shape=jax.ShapeDtypeStruct((B, S, D), F32),
        compiler_params=_params(2),
        name="merge_ffn_final" if final_g is not None else "merge_ffn",
    )(*args)


def _rope_t(rows, cos, sin):
    partner = jnp.concatenate([rows[HALF_ROPE:], rows[:HALF_ROPE]], axis=0)
    return rows * cos + partner * sin


def _ffn_proj_kernel(x_ref, mod_ref, g1_ref, wgu_ref, wd_ref,
                     g_ref, win_ref, wkpe_ref, qg_ref, wuq_ref, kvg_ref, wk_ref, wv_ref,
                     cw_ref, cg_ref, cos_ref, sin_ref,
                     x1_ref, qt_ref, k_ref, vt_ref, conv_ref, hist_ref, *, tm):
    @pl.when(pl.program_id(1) == 0)
    def _():
        hist_ref[0:HALO, :] = jnp.zeros((HALO, CONV_W), F32)

    for r in range(FFN_ROW_GROUPS):
        rows = slice(r * tm // FFN_ROW_GROUPS, (r + 1) * tm // FFN_ROW_GROUPS)
        x1_ref[0, rows, :] = _swiglu_rows(x_ref[0, rows, :], mod_ref, 0, g1_ref, wgu_ref, wd_ref)

    x = x1_ref[0]
    shift = mod_ref[0, 3:4, :]
    scale = mod_ref[0, 4:5, :]
    h = (_rms(x, g_ref[...]) * (1.0 + scale) + shift).astype(BF16)
    proj = jnp.dot(h, win_ref[:, 0:P_GB], preferred_element_type=F32)
    cos = cos_ref[0]
    sin = sin_ref[0]
    rope_lo, rope_hi = QK_NOPE, QK_NOPE + QK_ROPE

    cqn = _rms(proj[:, P_CQ:P_CQ + Q_RANK], qg_ref[...]).astype(BF16)
    qt = lax.dot_general(wuq_ref[...], cqn, NT_DIMS, preferred_element_type=F32)
    gates = jnp.dot(h, win_ref[:, P_GB:P_WIDTH], preferred_element_type=F32)
    ckvn = _rms(proj[:, P_CKV:P_CKV + KV_RANK], kvg_ref[...]).astype(BF16)
    kk = jnp.dot(ckvn, wk_ref[...], preferred_element_type=F32)
    vt = lax.dot_general(wv_ref[...], ckvn, NT_DIMS, preferred_element_type=F32)
    kpe_t = lax.dot_general(wkpe_ref[...], h, NT_DIMS, preferred_element_type=F32)
    kpe_t = jnp.concatenate([jnp.zeros((rope_lo, tm), F32), _rope_t(kpe_t, cos, sin),
                             jnp.zeros((HEAD_SLAB - rope_hi, tm), F32)], axis=0)
    kpe = kpe_t.T
    for hd in range(N_HEADS):
        lo = hd * HEAD_SLAB
        qh = qt[lo:lo + HEAD_SLAB]
        qh = jnp.concatenate([qh[:rope_lo], _rope_t(qh[rope_lo:rope_hi], cos, sin), qh[rope_hi:]], axis=0)
        qt_ref[0, hd] = (qh * Q_SCALE).astype(BF16)
        k_ref[0, hd] = (kk[:, lo:lo + HEAD_SLAB] + kpe).astype(BF16)
    vt_ref[0] = vt.astype(BF16)

    cv =gates[:, CONV_W:2 * CONV_W] * gates[:, 2 * CONV_W:3 * CONV_W]
    hist_ref[HALO:HALO + tm, :] = cv
    cv1 = hist_ref[HALO - 1:HALO - 1 + tm, :]
    cv2 = hist_ref[HALO - 2:HALO - 2 + tm, :]
    conv = cw_ref[0:1, :] * cv2 + cw_ref[1:2, :] * cv1 + cw_ref[2:3, :] * cv
    conv = gates[:, 0:CONV_W] * conv
    conv_ref[0] = _rms(conv, cg_ref[...]).astype(BF16)
    hist_ref[0:HALO, :] = cv[tm - HALO:, :]


def _ffn_proj(x, mod_l, g1, wgu, wd, g, win, wkpe, qg, wuq, kvg, wk, wv, cw, cg, cos, sin, layer, tm):
    B, S, D = x.shape
    tok = lambda w: pl.BlockSpec((1, tm, w), lambda b, i: (b, i, 0))
    tok_t = lambda r: pl.BlockSpec((1, r, tm), lambda b, i: (b, 0, i))
    row = lambda w: _row_spec(w, layer)
    wspec = lambda r, c: _resident((None, r, c), lambda b, i: (layer, 0, 0))
    return pl.pallas_call(
        functools.partial(_ffn_proj_kernel, tm=tm),
        grid=(B, S // tm),
        in_specs=[tok(D),
                  _mod_spec(layer),
                  row(D), wspec(D, 2 * D_FF), wspec(D_FF, D),
                  row(D), wspec(D, P_WIDTH), wspec(QK_ROPE, D),
                  row(Q_RANK), wspec(N_HEADS * HEAD_SLAB, Q_RANK),
                  row(KV_RANK), wspec(KV_RANK, N_HEADS * HEAD_SLAB), wspec(ATTN_W, KV_RANK),
                  pl.BlockSpec((None, 3, CONV_W), lambda b, i: (layer, 0, 0)),
                  row(CONV_W), tok_t(QK_ROPE), tok_t(QK_ROPE)],
        out_specs=[tok(D),
                   pl.BlockSpec((1, N_HEADS, HEAD_SLAB, tm), lambda b, i: (b, 0, 0, i)),
                   pl.BlockSpec((1, N_HEADS, tm, HEAD_SLAB), lambda b, i: (b, 0, i, 0)),
                   tok_t(ATTN_W), tok(CONV_W)],
        out_shape=[jax.ShapeDtypeStruct((B, S, D), F32),
                   jax.ShapeDtypeStruct((B, N_HEADS, HEAD_SLAB, S), BF16),
                   jax.ShapeDtypeStruct((B, N_HEADS, S, HEAD_SLAB), BF16),
                   jax.ShapeDtypeStruct((B, ATTN_W, S), BF16),
                   jax.ShapeDtypeStruct((B, S, CONV_W), BF16)],
        scratch_shapes=[pltpu.VMEM((HALO + tm, CONV_W), F32)],
        compiler_params=_params(2),
        name="ffn_proj",
    )(x, mod_l, g1, wgu, wd, g, win, wkpe, qg, wuq, kvg, wk, wv, cw, cg, cos, sin)


def _attn_kernel(qt_ref, k_ref, vt_ref, o_ref, s_ref, acc_ref, cm_ref, *, tq, n_tiles):
    g = pl.program_id(2)
    hb = MXU_TILE
    krow = lax.broadcasted_iota(jnp.int32, (hb, hb), 0)
    qcol = lax.broadcasted_iota(jnp.int32, (hb, hb), 1)
    tri = krow <= qcol
    zpad = jnp.zeros((hb, hb - HEAD_SLAB), BF16)
    zrows = jnp.zeros((hb - HEAD_SLAB, hb), BF16)
    ones = jnp.ones((VALUE_ROWS - V_DIM, hb), BF16)
    heads = range(HEADS_PER_STEP)

    def q_half(h, tile, qh):
        start = pl.multiple_of(tile * tq + qh * hb, hb)
        return jnp.concatenate([qt_ref[0, h, :, pl.ds(start, hb)], zrows], axis=0)

    def k_half(h, blk, kh):
        start = pl.multiple_of(blk * tq + kh * hb, hb)
        return jnp.concatenate([k_ref[0, h, pl.ds(start, hb), :], zpad], axis=1)

    def v_half(h, blk, kh):
        start = pl.multiple_of(blk * tq + kh * hb, hb)
        return jnp.concatenate([vt_ref[0, h * V_DIM:(h + 1) * V_DIM, pl.ds(start, hb)], ones], axis=0)

    def quad_ref(h, kh, qh):
        return s_ref.at[h, kh * hb:(kh + 1) * hb, qh * hb:(qh + 1) * hb]

    def prestage(h, tile):
        pltpu.matmul_push_rhs(q_half(h, tile, 0), 0, h)

    def drain_stage(h):
        pltpu.matmul_acc_lhs(SCORE_SLOTS[1], jnp.zeros((16, hb), BF16), h, 0)
        pltpu.matmul_pop(SCORE_SLOTS[1], (16, hb), F32, h)

    def issue3(h, tile, blk):
        pltpu.matmul_acc_lhs(SCORE_SLOTS[0], k_half(h, blk, 0), h, 0)
        pltpu.matmul_acc_lhs(SCORE_SLOTS[1], k_half(h, blk, 1), h, None)
        pltpu.matmul_push_rhs(q_half(h, tile, 1), 0, h)
        pltpu.matmul_acc_lhs(SCORE_SLOTS[2], k_half(h, blk, 0), h, 0)

    def issue4(h, blk):
        pltpu.matmul_acc_lhs(SCORE_SLOTS[0], k_half(h, blk, 1), h, None)

    def pop_quad(h, slot, kh, qh, diag):
        st = pltpu.matmul_pop(slot, (hb, hb), F32, h)
        if diag:
            if kh > qh:
                st = jnp.full((hb, hb), -jnp.inf, F32)
            elif kh == qh:
                st = jnp.where(tri, st, -jnp.inf)
        quad_ref(h, kh, qh)[...] = st
        return jnp.max(st, axis=0, keepdims=True)

    def numer(h, kh, qh, m_new, reg):
        mq = m_new[:, qh * hb:(qh + 1) * hb]
        p = jnp.exp2(quad_ref(h, kh, qh)[...] - mq).astype(BF16)
        pltpu.matmul_push_rhs(p, reg, h)

    def pv(h, blk, qh):
        pltpu.matmul_acc_lhs(OUT_SLOTS[qh], v_half(h, blk, 0), h, 1)
        pltpu.matmul_acc_lhs(OUT_SLOTS[qh], v_half(h, blk, 1), h, 0)

    def accumulate(h, alpha):
        out = jnp.concatenate([pltpu.matmul_pop(OUT_SLOTS[qh], (VALUE_ROWS, hb), F32, h) for qh in range(2)], axis=1)
        acc_ref[h] = alpha * acc_ref[h] + out

    def col_max(c):
        return jnp.concatenate([jnp.maximum(c[0], c[1]), jnp.maximum(c[2], c[3])], axis=1)

    def order(i, t):
        return jnp.where(t == 0, i, t - 1)

    @pl.when(g == 0)
    def _():
        for h in heads:
            drain_stage(h)
            prestage(h, 0)
            issue3(h, 0, 0)
        for h in heads:
            c = [pop_quad(h, SCORE_SLOTS[0], 0, 0, True)]
            issue4(h, 0)
            c.append(pop_quad(h, SCORE_SLOTS[1], 1, 0, True))
            c.append(pop_quad(h, SCORE_SLOTS[2], 0, 1, True))
            c.append(pop_quad(h, SCORE_SLOTS[0], 1, 1, True))
            cm_ref[h] = col_max(c)
            prestage(h, min(1, n_tiles - 1))

    def step(blk, state, nxt, staged_tile=None):
        ms, alphas, cs = [], [], [[] for _ in heads]
        for h in heads:
            m, cm, _ = state[h]
            m_new = jnp.maximum(m, cm)
            ms.append(m_new)
            alphas.append(jnp.exp2(m - m_new))
        for h in heads:
            numer(h, 0, 0, ms[h], 1)
        for h in heads:
            if nxt is not None:
                issue3(h, nxt[0], nxt[1])
            else:
                drain_stage(h)
        for h in heads:
            if nxt is not None:
                cs[h].append(pop_quad(h, SCORE_SLOTS[0], 0, 0, nxt[2]))
                issue4(h, nxt[1])
        for h in heads:
            numer(h, 1, 0, ms[h], 0)
        if nxt is not None:
            for h in heads:
                cs[h].append(pop_quad(h, SCORE_SLOTS[1], 1, 0, nxt[2]))
        for h in heads:
            accumulate(h, state[h][2])
            pv(h, blk, 0)
        for h in heads:
            numer(h, 0, 1, ms[h], 1)
            numer(h, 1, 1, ms[h], 0)
            pv(h, blk, 1)
        out = []
        for h in heads:
            if nxt is not None:
                cs[h].append(pop_quad(h, SCORE_SLOTS[2], 0, 1, nxt[2]))
                cs[h].append(pop_quad(h, SCORE_SLOTS[0], 1, 1, nxt[2]))
                prestage(h, nxt[0] if staged_tile is None else staged_tile)
                out.append((ms[h], col_max(cs[h]), alphas[h]))
            else:
                out.append((ms[h], state[h][1], alphas[h]))
        return tuple(out)

    def run_tile(i, sub):
        for h in heads:
            acc_ref[h] = jnp.zeros((VALUE_ROWS, tq), F32)
        state = tuple((jnp.full((1, tq), -jnp.inf, F32), cm_ref[h], jnp.ones((1, tq), F32)) for h in heads)
        next_tile = jnp.minimum(i + 1, n_tiles - 1)
        state = lax.fori_loop(
            0, i,
            lambda t, st: step(order(i, t), st, (i, t, False), jnp.where(t == i - 1, next_tile, i)),
            state)

        def finish(nxt):
            last = step(order(i, i), state, nxt)
            outs = []
            for h in heads:
                if nxt is not None:
                    cm_ref[h] = last[h][1]
                accumulate(h, last[h][2])
                acc = acc_ref[h]
                outs.append(acc[0:V_DIM, :] / acc[V_DIM:V_DIM + 1, :])
            o_ref[0, sub * tq:(sub + 1) * tq, :] = jnp.concatenate(outs, axis=0).T

        if sub < TILES_PER_STEP - 1:
            finish((i + 1, i + 1, True))
        else:
            @pl.when(i < n_tiles - 1)
            def _():
                finish((i + 1, i + 1, True))

            @pl.when(i == n_tiles - 1)
            def _():
                finish(None)

    for sub in range(TILES_PER_STEP):
        run_tile(g * TILES_PER_STEP + sub, sub)


def _attention(qt, k, vt, tq):
    B, H, S, _ = k.shape
    hps = HEADS_PER_STEP
    assert tq == 2 * MXU_TILE and HEAD_SLAB <= MXU_TILE
    return pl.pallas_call(
        functools.partial(_attn_kernel, tq=tq, n_tiles=S // tq),
        grid=(B, H // hps, S // (tq * TILES_PER_STEP)),
        in_specs=[pl.BlockSpec((1, hps, HEAD_SLAB, S), lambda b, p, i: (b, p, 0, 0)),
                  pl.BlockSpec((1, hps, S, HEAD_SLAB), lambda b, p, i: (b, p, 0, 0)),
                  pl.BlockSpec((1, hps * V_DIM, S), lambda b, p, i: (b, p, 0))],
        out_specs=pl.BlockSpec((1, tq * TILES_PER_STEP, hps * V_DIM), lambda b, p, i: (b, i, p)),
        out_shape=jax.ShapeDtypeStruct((B, S, ATTN_W), F32),
        scratch_shapes=[pltpu.VMEM((hps, tq, tq), F32),
                        pltpu.VMEM((hps, VALUE_ROWS, tq), F32),
                        pltpu.VMEM((hps, 1, tq), F32)],
        compiler_params=_params(3),
        name="mla_attention",
    )(qt, k, vt)


def _layout_w_in(w_in):
    lat = Q_RANK + KV_RANK
    main = jnp.concatenate([w_in[:, :, :lat], w_in[:, :, lat + QK_ROPE:]], axis=-1)
    kpe_t = jnp.swapaxes(w_in[:, :, lat:lat + QK_ROPE], 1, 2)
    return main.astype(BF16), kpe_t.astype(BF16)


def _layout_w_uq(w_uq):
    L, R, _ = w_uq.shape
    w = w_uq.reshape(L, R, N_HEADS, QK_NOPE + QK_ROPE)
    z_pad = jnp.zeros((L, R, N_HEADS, HEAD_SLAB - QK_NOPE - QK_ROPE), w.dtype)
    q = jnp.concatenate([w, z_pad], axis=-1).reshape(L, R, N_HEADS * HEAD_SLAB)
    return jnp.swapaxes(q, 1, 2).astype(BF16)


def _layout_w_ukv(w_ukv):
    L, R, _ = w_ukv.shape
    w = w_ukv.reshape(L, R, N_HEADS, QK_NOPE + V_DIM)
    k_nope, v = w[..., :QK_NOPE], w[..., QK_NOPE:]
    z = jnp.zeros((L, R, N_HEADS, HEAD_SLAB - QK_NOPE), w.dtype)
    wk = jnp.concatenate([k_nope, z], axis=-1).reshape(L, R, N_HEADS * HEAD_SLAB)
    wv_t = jnp.swapaxes(v.reshape(L, R, ATTN_W), 1, 2)
    return wk.astype(BF16), wv_t.astype(BF16)


def kernel(x, c, positions, w_ada, b_ada, ffn1_norm, ffn1_w_gu, ffn1_w_down, mix_norm, w_in, q_a_norm, w_uq, kv_a_norm, w_ukv, conv_w, attn_out_norm, conv_out_norm, w_o, ffn2_norm, ffn2_w_gu, ffn2_w_down, final_norm):
    B, S, D = x.shape
    L = w_ada.shape[0]
    tm = min(S, 512)
    tq = min(S, 512)

    mod = _modulation(c, w_ada, b_ada)
    cos, sin = _rope_tables(positions)

    wgu1, wd1 = ffn1_w_gu.astype(BF16), ffn1_w_down.astype(BF16)
    wgu2, wd2 = ffn2_w_gu.astype(BF16), ffn2_w_down.astype(BF16)
    win, wkpe_t = _layout_w_in(w_in)
    wuq_t = _layout_w_uq(w_uq)
    wk, wv_t = _layout_w_ukv(w_ukv)
    wo = w_o.astype(BF16)
    final_g = final_norm.reshape(1, D)
    rows = lambda g: g.reshape(L, 1, g.shape[-1])
    g1, g2, gm = rows(ffn1_norm), rows(ffn2_norm), rows(mix_norm)
    gq, gkv = rows(q_a_norm), rows(kv_a_norm)
    ga, gc = rows(attn_out_norm), rows(conv_out_norm)

    for l in range(L):
        x, qt, k, vt, conv = _ffn_proj(x, mod, g1, wgu1, wd1, gm, win, wkpe_t, gq, wuq_t, gkv, wk, wv_t,
                                       conv_w, gc, cos, sin, l, tm)
        attn = _attention(qt, k, vt, tq)
        x = _merge_ffn(x, mod, attn, conv, ga, wo, g2, wgu2, wd2, l, tm,
                       final_g=final_g if l == L - 1 else None)
    return x
```

```python
import functools

import jax
import jax.numpy as jnp
from jax import lax
from jax.experimental import pallas as pl
from jax.experimental.pallas import tpu as pltpu

F32 = jnp.float32
BF16 = jnp.bfloat16

D_MODEL = 1024
N_HEADS = 8
QK_NOPE = 64
QK_ROPE = 32
V_DIM = 64
Q_RANK = 384
KV_RANK = 256
CONV_W = 512
ATTN_W = N_HEADS * V_DIM
D_FF = 2816
N_MOD = 9
EPS = 1e-6
ROPE_THETA = 10000.0
HEAD_SLAB = 128
HALF_ROPE = QK_ROPE // 2
P_CQ = 0
P_CKV = P_CQ + Q_RANK
P_GB = P_CKV + KV_RANK
P_WIDTH = P_GB + 3 * CONV_W
HALO = 8
Q_SCALE = float((QK_NOPE + QK_ROPE) ** -0.5 * 1.4426950408889634)
VMEM_LIMIT = 56 * 1024 * 1024
FFN_ROW_GROUPS = 2
HEADS_PER_STEP = 2
TILES_PER_STEP = 2
MXU_TILE = 256
VALUE_ROWS = V_DIM + 16
SCORE_SLOTS = (0, MXU_TILE // 4, 2 * MXU_TILE // 4)
OUT_SLOTS = (3 * MXU_TILE // 4, 3 * MXU_TILE // 4 + VALUE_ROWS // 4)
NT_DIMS = (((1,), (1,)), ((), ()))


def _rms(x, g):
    return x * lax.rsqrt(jnp.mean(x * x, axis=-1, keepdims=True) + EPS) * g


def _silu(x):
    return x / (1.0 + jnp.exp(-x))


def _params(n_axes):
    return pltpu.CompilerParams(dimension_semantics=("arbitrary",) * n_axes,
                                vmem_limit_bytes=VMEM_LIMIT)


def _resident(shape, index_map):
    return pl.BlockSpec(shape, index_map, pipeline_mode=pl.Buffered(1))


def _row_spec(width, layer):
    return pl.BlockSpec((None, 1, width), lambda *_: (layer, 0, 0))


def _mod_spec(layer):
    return pl.BlockSpec((None, 1, N_MOD, D_MODEL), lambda b, *_: (layer, b, 0, 0))


def _mod_kernel(c_ref, w_ref, b_ref, o_ref):
    c = c_ref[...]
    ca = _silu(c).astype(BF16)
    o_ref[0] = jnp.dot(ca, w_ref[0].astype(BF16), preferred_element_type=F32) + b_ref[0]


def _modulation(c, w_ada, b_ada):
    L, D, N = w_ada.shape
    B = c.shape[0]
    tn = N // 8
    out = pl.pallas_call(
        _mod_kernel,
        grid=(L, N // tn),
        in_specs=[pl.BlockSpec((B, D), lambda l, n: (0, 0)),
                  pl.BlockSpec((1, D, tn), lambda l, n: (l, 0, n)),
                  pl.BlockSpec((1, 1, tn), lambda l, n: (l, 0, n))],
        out_specs=pl.BlockSpec((1, B, tn), lambda l, n: (l, 0, n)),
        out_shape=jax.ShapeDtypeStruct((L, B, N), F32),
        compiler_params=_params(2),
        name="adaln_mod",
    )(c, w_ada, b_ada.reshape(L, 1, N))
    return out.reshape(L, B, N_MOD, D)


def _rope_kernel(pos_ref, invf_ref, sgn_ref, cos_ref, sin_ref):
    ang = invf_ref[...] * pos_ref[0].astype(F32)
    cos_ref[0] = jnp.cos(ang)
    sin_ref[0] = jnp.sin(ang) * sgn_ref[...]


def _rope_tables(positions):
    B, S = positions.shape
    ts = min(S, 2048)
    inv_freq = 1.0 / (ROPE_THETA ** (jnp.arange(0, QK_ROPE, 2, dtype=F32) / QK_ROPE))
    invf = jnp.concatenate([inv_freq, inv_freq])[:, None]
    ones = jnp.ones((HALF_ROPE,), F32)
    sgn = jnp.concatenate([-ones, ones])[:, None]
    tab = jax.ShapeDtypeStruct((B, QK_ROPE, S), F32)
    col = pl.BlockSpec((QK_ROPE, 1), lambda b, i: (0, 0))
    out = pl.BlockSpec((1, QK_ROPE, ts), lambda b, i: (b, 0, i))
    return pl.pallas_call(
        _rope_kernel,
        grid=(B, S // ts),
        in_specs=[pl.BlockSpec((1, 1, ts), lambda b, i: (b, 0, i)), col, col],
        out_specs=[out, out],
        out_shape=[tab, tab],
        compiler_params=_params(2),
        name="rope_tables",
    )(positions.reshape(B, 1, S), invf, sgn)


def _swiglu_rows(x, mod_ref, mod_base, g_ref, wgu_ref, wd_ref):
    shift = mod_ref[0, mod_base:mod_base + 1, :]
    scale = mod_ref[0, mod_base + 1:mod_base + 2, :]
    gate = mod_ref[0, mod_base + 2:mod_base + 3, :]
    h = (_rms(x, g_ref[...]) * (1.0 + scale) + shift).astype(BF16)
    gu = jnp.dot(h, wgu_ref[...], preferred_element_type=F32)
    a = (_silu(gu[:, :D_FF]) * gu[:, D_FF:]).astype(BF16)
    y = jnp.dot(a, wd_ref[...], preferred_element_type=F32)
    return x + (0.5 * (1.0 + gate)) * y


def _merge_ffn_kernel(x_ref, mod_ref, attn_ref, conv_ref, ag_ref, wo_ref, g_ref, wgu_ref, wd_ref, *rest, final):
    o_ref = rest[-1]
    an = _rms(attn_ref[0], ag_ref[...]).astype(BF16)
    merged = jnp.concatenate([an, conv_ref[0]], axis=-1)
    ym = jnp.dot(merged, wo_ref[...], preferred_element_type=F32)
    x = x_ref[0] + (1.0 + mod_ref[0, 5:6, :]) * ym
    out = _swiglu_rows(x, mod_ref, 6, g_ref, wgu_ref, wd_ref)
    if final:
        out = _rms(out, rest[0][...])
    o_ref[0] = out


def _merge_ffn(x, mod_l, attn, conv, ag, wo, g, wgu, wd, layer, tm, final_g=None):
    B, S, D = x.shape
    tok = lambda w: pl.BlockSpec((1, tm, w), lambda b, i: (b, i, 0))
    wspec = lambda r, c: _resident((None, r, c), lambda b, i: (layer, 0, 0))
    in_specs = [tok(D), _mod_spec(layer), tok(ATTN_W), tok(CONV_W), _row_spec(ATTN_W, layer),
                wspec(ATTN_W + CONV_W, D), _row_spec(D, layer), wspec(D, 2 * D_FF), wspec(D_FF, D)]
    args = [x, mod_l, attn, conv, ag, wo, g, wgu, wd]
    if final_g is not None:
        in_specs.append(pl.BlockSpec((1, D), lambda b, i: (0, 0)))
        args.append(final_g)
    return pl.pallas_call(
        functools.partial(_merge_ffn_kernel, final=final_g is not None),
        grid=(B, S // tm),
        in_specs=in_specs,
        out_specs=tok(D),
        out_shape=jax.ShapeDtypeStruct((B, S, D), F32),
        compiler_params=_params(2),
        name="merge_ffn_final" if final_g is not None else "merge_ffn",
    )(*args)


def _rope_t(rows, cos, sin):
    partner = jnp.concatenate([rows[HALF_ROPE:], rows[:HALF_ROPE]], axis=0)
    return rows * cos + partner * sin


def _ffn_proj_kernel(x_ref, mod_ref, g1_ref, wgu_ref, wd_ref,
                     g_ref, win_ref, wkpe_ref, qg_ref, wuq_ref, kvg_ref, wk_ref, wv_ref,
                     cw_ref, cg_ref, cos_ref, sin_ref,
                     x1_ref, qt_ref, k_ref, vt_ref, conv_ref, hist_ref, *, tm):
    @pl.when(pl.program_id(1) == 0)
    def _():
        hist_ref[0:HALO, :] = jnp.zeros((HALO, CONV_W), F32)

    for r in range(FFN_ROW_GROUPS):
        rows = slice(r * tm // FFN_ROW_GROUPS, (r + 1) * tm // FFN_ROW_GROUPS)
        x1_ref[0, rows, :] = _swiglu_rows(x_ref[0, rows, :], mod_ref, 0, g1_ref, wgu_ref, wd_ref)

    x = x1_ref[0]
    shift = mod_ref[0, 3:4, :]
    scale = mod_ref[0, 4:5, :]
    h = (_rms(x, g_ref[...]) * (1.0 + scale) + shift).astype(BF16)
    proj = jnp.dot(h, win_ref[:, 0:P_GB], preferred_element_type=F32)
    cos = cos_ref[0]
    sin = sin_ref[0]
    rope_lo, rope_hi = QK_NOPE, QK_NOPE + QK_ROPE

    cqn = _rms(proj[:, P_CQ:P_CQ + Q_RANK], qg_ref[...]).astype(BF16)
    qt = lax.dot_general(wuq_ref[...], cqn, NT_DIMS, preferred_element_type=F32)
    gates = jnp.dot(h, win_ref[:, P_GB:P_WIDTH], preferred_element_type=F32)
    ckvn = _rms(proj[:, P_CKV:P_CKV + KV_RANK], kvg_ref[...]).astype(BF16)
    kk = jnp.dot(ckvn, wk_ref[...], preferred_element_type=F32)
    vt = lax.dot_general(wv_ref[...], ckvn, NT_DIMS, preferred_element_type=F32)
    kpe_t = lax.dot_general(wkpe_ref[...], h, NT_DIMS, preferred_element_type=F32)
    kpe_t = jnp.concatenate([jnp.zeros((rope_lo, tm), F32), _rope_t(kpe_t, cos, sin),
                             jnp.zeros((HEAD_SLAB - rope_hi, tm), F32)], axis=0)
    kpe = kpe_t.T
    for hd in range(N_HEADS):
        lo = hd * HEAD_SLAB
        qh = qt[lo:lo + HEAD_SLAB]
        qh = jnp.concatenate([qh[:rope_lo], _rope_t(qh[rope_lo:rope_hi], cos, sin), qh[rope_hi:]], axis=0)
        qt_ref[0, hd] = (qh * Q_SCALE).astype(BF16)
        k_ref[0, hd] = (kk[:, lo:lo + HEAD_SLAB] + kpe).astype(BF16)
    vt_ref[0] = vt.astype(BF16)

    cv =gates[:, CONV_W:2 * CONV_W] * gates[:, 2 * CONV_W:3 * CONV_W]
    hist_ref[HALO:HALO + tm, :] = cv
    cv1 = hist_ref[HALO - 1:HALO - 1 + tm, :]
    cv2 = hist_ref[HALO - 2:HALO - 2 + tm, :]
    conv = cw_ref[0:1, :] * cv2 + cw_ref[1:2, :] * cv1 + cw_ref[2:3, :] * cv
    conv = gates[:, 0:CONV_W] * conv
    conv_ref[0] = _rms(conv, cg_ref[...]).astype(BF16)
    hist_ref[0:HALO, :] = cv[tm - HALO:, :]


def _ffn_proj(x, mod_l, g1, wgu, wd, g, win, wkpe, qg, wuq, kvg, wk, wv, cw, cg, cos, sin, layer, tm):
    B, S, D = x.shape
    tok = lambda w: pl.BlockSpec((1, tm, w), lambda b, i: (b, i, 0))
    tok_t = lambda r: pl.BlockSpec((1, r, tm), lambda b, i: (b, 0, i))
    row = lambda w: _row_spec(w, layer)
    wspec = lambda r, c: _resident((None, r, c), lambda b, i: (layer, 0, 0))
    return pl.pallas_call(
        functools.partial(_ffn_proj_kernel, tm=tm),
        grid=(B, S // tm),
        in_specs=[tok(D),
                  _mod_spec(layer),
                  row(D), wspec(D, 2 * D_FF), wspec(D_FF, D),
                  row(D), wspec(D, P_WIDTH), wspec(QK_ROPE, D),
                  row(Q_RANK), wspec(N_HEADS * HEAD_SLAB, Q_RANK),
                  row(KV_RANK), wspec(KV_RANK, N_HEADS * HEAD_SLAB), wspec(ATTN_W, KV_RANK),
                  pl.BlockSpec((None, 3, CONV_W), lambda b, i: (layer, 0, 0)),
                  row(CONV_W), tok_t(QK_ROPE), tok_t(QK_ROPE)],
        out_specs=[tok(D),
                   pl.BlockSpec((1, N_HEADS, HEAD_SLAB, tm), lambda b, i: (b, 0, 0, i)),
                   pl.BlockSpec((1, N_HEADS, tm, HEAD_SLAB), lambda b, i: (b, 0, i, 0)),
                   tok_t(ATTN_W), tok(CONV_W)],
        out_shape=[jax.ShapeDtypeStruct((B, S, D), F32),
                   jax.ShapeDtypeStruct((B, N_HEADS, HEAD_SLAB, S), BF16),
                   jax.ShapeDtypeStruct((B, N_HEADS, S, HEAD_SLAB), BF16),
                   jax.ShapeDtypeStruct((B, ATTN_W, S), BF16),
                   jax.ShapeDtypeStruct((B, S, CONV_W), BF16)],
        scratch_shapes=[pltpu.VMEM((HALO + tm, CONV_W), F32)],
        compiler_params=_params(2),
        name="ffn_proj",
    )(x, mod_l, g1, wgu, wd, g, win, wkpe, qg, wuq, kvg, wk, wv, cw, cg, cos, sin)


def _attn_kernel(qt_ref, k_ref, vt_ref, o_ref, s_ref, acc_ref, cm_ref, *, tq, n_tiles):
    g = pl.program_id(2)
    hb = MXU_TILE
    krow = lax.broadcasted_iota(jnp.int32, (hb, hb), 0)
    qcol = lax.broadcasted_iota(jnp.int32, (hb, hb), 1)
    tri = krow <= qcol
    zpad = jnp.zeros((hb, hb - HEAD_SLAB), BF16)
    zrows = jnp.zeros((hb - HEAD_SLAB, hb), BF16)
    ones = jnp.ones((VALUE_ROWS - V_DIM, hb), BF16)
    heads = range(HEADS_PER_STEP)

    def q_half(h, tile, qh):
        start = pl.multiple_of(tile * tq + qh * hb, hb)
        return jnp.concatenate([qt_ref[0, h, :, pl.ds(start, hb)], zrows], axis=0)

    def k_half(h, blk, kh):
        start = pl.multiple_of(blk * tq + kh * hb, hb)
        return jnp.concatenate([k_ref[0, h, pl.ds(start, hb), :], zpad], axis=1)

    def v_half(h, blk, kh):
        start = pl.multiple_of(blk * tq + kh * hb, hb)
        return jnp.concatenate([vt_ref[0, h * V_DIM:(h + 1) * V_DIM, pl.ds(start, hb)], ones], axis=0)

    def quad_ref(h, kh, qh):
        return s_ref.at[h, kh * hb:(kh + 1) * hb, qh * hb:(qh + 1) * hb]

    def prestage(h, tile):
        pltpu.matmul_push_rhs(q_half(h, tile, 0), 0, h)

    def drain_stage(h):
        pltpu.matmul_acc_lhs(SCORE_SLOTS[1], jnp.zeros((16, hb), BF16), h, 0)
        pltpu.matmul_pop(SCORE_SLOTS[1], (16, hb), F32, h)

    def issue3(h, tile, blk):
        pltpu.matmul_acc_lhs(SCORE_SLOTS[0], k_half(h, blk, 0), h, 0)
        pltpu.matmul_acc_lhs(SCORE_SLOTS[1], k_half(h, blk, 1), h, None)
        pltpu.matmul_push_rhs(q_half(h, tile, 1), 0, h)
        pltpu.matmul_acc_lhs(SCORE_SLOTS[2], k_half(h, blk, 0), h, 0)

    def issue4(h, blk):
        pltpu.matmul_acc_lhs(SCORE_SLOTS[0], k_half(h, blk, 1), h, None)

    def pop_quad(h, slot, kh, qh, diag):
        st = pltpu.matmul_pop(slot, (hb, hb), F32, h)
        if diag:
            if kh > qh:
                st = jnp.full((hb, hb), -jnp.inf, F32)
            elif kh == qh:
                st = jnp.where(tri, st, -jnp.inf)
        quad_ref(h, kh, qh)[...] = st
        return jnp.max(st, axis=0, keepdims=True)

    def numer(h, kh, qh, m_new, reg):
        mq = m_new[:, qh * hb:(qh + 1) * hb]
        p = jnp.exp2(quad_ref(h, kh, qh)[...] - mq).astype(BF16)
        pltpu.matmul_push_rhs(p, reg, h)

    def pv(h, blk, qh):
        pltpu.matmul_acc_lhs(OUT_SLOTS[qh], v_half(h, blk, 0), h, 1)
        pltpu.matmul_acc_lhs(OUT_SLOTS[qh], v_half(h, blk, 1), h, 0)

    def accumulate(h, alpha):
        out = jnp.concatenate([pltpu.matmul_pop(OUT_SLOTS[qh], (VALUE_ROWS, hb), F32, h) for qh in range(2)], axis=1)
        acc_ref[h] = alpha * acc_ref[h] + out

    def col_max(c):
        return jnp.concatenate([jnp.maximum(c[0], c[1]), jnp.maximum(c[2], c[3])], axis=1)

    def order(i, t):
        return jnp.where(t == 0, i, t - 1)

    @pl.when(g == 0)
    def _():
        for h in heads:
            drain_stage(h)
            prestage(h, 0)
            issue3(h, 0, 0)
        for h in heads:
            c = [pop_quad(h, SCORE_SLOTS[0], 0, 0, True)]
            issue4(h, 0)
            c.append(pop_quad(h, SCORE_SLOTS[1], 1, 0, True))
            c.append(pop_quad(h, SCORE_SLOTS[2], 0, 1, True))
            c.append(pop_quad(h, SCORE_SLOTS[0], 1, 1, True))
            cm_ref[h] = col_max(c)
            prestage(h, min(1, n_tiles - 1))

    def step(blk, state, nxt, staged_tile=None):
        ms, alphas, cs = [], [], [[] for _ in heads]
        for h in heads:
            m, cm, _ = state[h]
            m_new = jnp.maximum(m, cm)
            ms.append(m_new)
            alphas.append(jnp.exp2(m - m_new))
        for h in heads:
            numer(h, 0, 0, ms[h], 1)
        for h in heads:
            if nxt is not None:
                issue3(h, nxt[0], nxt[1])
            else:
                drain_stage(h)
        for h in heads:
            if nxt is not None:
                cs[h].append(pop_quad(h, SCORE_SLOTS[0], 0, 0, nxt[2]))
                issue4(h, nxt[1])
        for h in heads:
            numer(h, 1, 0, ms[h], 0)
        if nxt is not None:
            for h in heads:
                cs[h].append(pop_quad(h, SCORE_SLOTS[1], 1, 0, nxt[2]))
        for h in heads:
            accumulate(h, state[h][2])
            pv(h, blk, 0)
        for h in heads:
            numer(h, 0, 1, ms[h], 1)
            numer(h, 1, 1, ms[h], 0)
            pv(h, blk, 1)
        out = []
        for h in heads:
            if nxt is not None:
                cs[h].append(pop_quad(h, SCORE_SLOTS[2], 0, 1, nxt[2]))
                cs[h].append(pop_quad(h, SCORE_SLOTS[0], 1, 1, nxt[2]))
                prestage(h, nxt[0] if staged_tile is None else staged_tile)
                out.append((ms[h], col_max(cs[h]), alphas[h]))
            else:
                out.append((ms[h], state[h][1], alphas[h]))
        return tuple(out)

    def run_tile(i, sub):
        for h in heads:
            acc_ref[h] = jnp.zeros((VALUE_ROWS, tq), F32)
        state = tuple((jnp.full((1, tq), -jnp.inf, F32), cm_ref[h], jnp.ones((1, tq), F32)) for h in heads)
        next_tile = jnp.minimum(i + 1, n_tiles - 1)
        state = lax.fori_loop(
            0, i,
            lambda t, st: step(order(i, t), st, (i, t, False), jnp.where(t == i - 1, next_tile, i)),
            state)

        def finish(nxt):
            last = step(order(i, i), state, nxt)
            outs = []
            for h in heads:
                if nxt is not None:
                    cm_ref[h] = last[h][1]
                accumulate(h, last[h][2])
                acc = acc_ref[h]
                outs.append(acc[0:V_DIM, :] / acc[V_DIM:V_DIM + 1, :])
            o_ref[0, sub * tq:(sub + 1) * tq, :] = jnp.concatenate(outs, axis=0).T

        if sub < TILES_PER_STEP - 1:
            finish((i + 1, i + 1, True))
        else:
            @pl.when(i < n_tiles - 1)
            def _():
                finish((i + 1, i + 1, True))

            @pl.when(i == n_tiles - 1)
            def _():
                finish(None)

    for sub in range(TILES_PER_STEP):
        run_tile(g * TILES_PER_STEP + sub, sub)


def _attention(qt, k, vt, tq):
    B, H, S, _ = k.shape
    hps = HEADS_PER_STEP
    assert tq == 2 * MXU_TILE and HEAD_SLAB <= MXU_TILE and S % (tq * TILES_PER_STEP) == 0
    return pl.pallas_call(
        functools.partial(_attn_kernel, tq=tq, n_tiles=S // tq),
        grid=(B, H // hps, S // (tq * TILES_PER_STEP)),
        in_specs=[pl.BlockSpec((1, hps, HEAD_SLAB, S), lambda b, p, i: (b, p, 0, 0)),
                  pl.BlockSpec((1, hps, S, HEAD_SLAB), lambda b, p, i: (b, p, 0, 0)),
                  pl.BlockSpec((1, hps * V_DIM, S), lambda b, p, i: (b, p, 0))],
        out_specs=pl.BlockSpec((1, tq * TILES_PER_STEP, hps * V_DIM), lambda b, p, i: (b, i, p)),
        out_shape=jax.ShapeDtypeStruct((B, S, ATTN_W), F32),
        scratch_shapes=[pltpu.VMEM((hps, tq, tq), F32),
                        pltpu.VMEM((hps, VALUE_ROWS, tq), F32),
                        pltpu.VMEM((hps, 1, tq), F32)],
        compiler_params=_params(3),
        name="mla_attention",
    )(qt, k, vt)


def _layout_w_in(w_in):
    lat = Q_RANK + KV_RANK
    main = jnp.concatenate([w_in[:, :, :lat], w_in[:, :, lat + QK_ROPE:]], axis=-1)
    kpe_t = jnp.swapaxes(w_in[:, :, lat:lat + QK_ROPE], 1, 2)
    return main.astype(BF16), kpe_t.astype(BF16)


def _layout_w_uq(w_uq):
    L, R, _ = w_uq.shape
    w = w_uq.reshape(L, R, N_HEADS, QK_NOPE + QK_ROPE)
    z_pad = jnp.zeros((L, R, N_HEADS, HEAD_SLAB - QK_NOPE - QK_ROPE), w.dtype)
    q = jnp.concatenate([w, z_pad], axis=-1).reshape(L, R, N_HEADS * HEAD_SLAB)
    return jnp.swapaxes(q, 1, 2).astype(BF16)


def _layout_w_ukv(w_ukv):
    L, R, _ = w_ukv.shape
    w = w_ukv.reshape(L, R, N_HEADS, QK_NOPE + V_DIM)
    k_nope, v = w[..., :QK_NOPE], w[..., QK_NOPE:]
    z = jnp.zeros((L, R, N_HEADS, HEAD_SLAB - QK_NOPE), w.dtype)
    wk = jnp.concatenate([k_nope, z], axis=-1).reshape(L, R, N_HEADS * HEAD_SLAB)
    wv_t = jnp.swapaxes(v.reshape(L, R, ATTN_W), 1, 2)
    return wk.astype(BF16), wv_t.astype(BF16)


def kernel(x, c, positions, w_ada, b_ada, ffn1_norm, ffn1_w_gu, ffn1_w_down, mix_norm, w_in, q_a_norm, w_uq, kv_a_norm, w_ukv, conv_w, attn_out_norm, conv_out_norm, w_o, ffn2_norm, ffn2_w_gu, ffn2_w_down, final_norm):
    B, S, D = x.shape
    L = w_ada.shape[0]
    tm = min(S, 512)
    tq = min(S, 512)

    mod = _modulation(c, w_ada, b_ada)
    cos, sin = _rope_tables(positions)

    wgu1, wd1 = ffn1_w_gu.astype(BF16), ffn1_w_down.astype(BF16)
    wgu2, wd2 = ffn2_w_gu.astype(BF16), ffn2_w_down.astype(BF16)
    win, wkpe_t = _layout_w_in(w_in)
    wuq_t = _layout_w_uq(w_uq)
    wk, wv_t = _layout_w_ukv(w_ukv)
    wo = w_o.astype(BF16)
    final_g = final_norm.reshape(1, D)
    rows = lambda g: g.reshape(L, 1, g.shape[-1])
    g1, g2, gm = rows(ffn1_norm), rows(ffn2_norm), rows(mix_norm)
    gq, gkv = rows(q_a_norm), rows(kv_a_norm)
    ga, gc = rows(attn_out_norm), rows(conv_out_norm)

    for l in range(L):
        x, qt, k, vt, conv = _ffn_proj(x, mod, g1, wgu1, wd1, gm, win, wkpe_t, gq, wuq_t, gkv, wk, wv_t,
                                       conv_w, gc, cos, sin, l, tm)
        attn = _attention(qt, k, vt, tq)
        x = _merge_ffn(x, mod, attn, conv, ga, wo, g2, wgu2, wd2, l, tm,
                       final_g=final_g if l == L - 1 else None)
    return x
```

```python
import functools

import jax
import jax.numpy as jnp
from jax import lax
from jax.experimental import pallas as pl
from jax.experimental.pallas import tpu as pltpu

F32 = jnp.float32
BF16 = jnp.bfloat16

D_MODEL = 1024
N_HEADS = 8
QK_NOPE = 64
QK_ROPE = 32
V_DIM = 64
Q_RANK = 384
KV_RANK = 256
CONV_W = 512
ATTN_W = N_HEADS * V_DIM
D_FF = 2816
N_MOD = 9
EPS = 1e-6
ROPE_THETA = 10000.0
HEAD_SLAB = 128
HALF_ROPE = QK_ROPE // 2
P_CQ = 0
P_CKV = P_CQ + Q_RANK
P_GB = P_CKV + KV_RANK
P_WIDTH = P_GB + 3 * CONV_W
HALO = 8
Q_SCALE = float((QK_NOPE + QK_ROPE) ** -0.5 * 1.4426950408889634)
VMEM_LIMIT = 56 * 1024 * 1024
FFN_ROW_GROUPS = 2
HEADS_PER_STEP = 2
TILES_PER_STEP = 2
MXU_TILE = 256
VALUE_ROWS = V_DIM + 16
SCORE_SLOTS = (0, MXU_TILE // 4, 2 * MXU_TILE // 4)
OUT_SLOTS = (3 * MXU_TILE // 4, 3 * MXU_TILE // 4 + VALUE_ROWS // 4)
NT_DIMS = (((1,), (1,)), ((), ()))


def _rms(x, g):
    return x * lax.rsqrt(jnp.mean(x * x, axis=-1, keepdims=True) + EPS) * g


def _silu(x):
    return x / (1.0 + jnp.exp(-x))


def _params(n_axes):
    return pltpu.CompilerParams(dimension_semantics=("arbitrary",) * n_axes,
                                vmem_limit_bytes=VMEM_LIMIT)


def _resident(shape, index_map):
    return pl.BlockSpec(shape, index_map, pipeline_mode=pl.Buffered(1))


def _row_spec(width, layer):
    return pl.BlockSpec((None, 1, width), lambda *_: (layer, 0, 0))


def _mod_spec(layer):
    return pl.BlockSpec((None, 1, N_MOD, D_MODEL), lambda b, *_: (layer, b, 0, 0))


def _mod_kernel(c_ref, w_ref, b_ref, o_ref):
    c = c_ref[...]
    ca = _silu(c).astype(BF16)
    o_ref[0] = jnp.dot(ca, w_ref[0].astype(BF16), preferred_element_type=F32) + b_ref[0]


def _modulation(c, w_ada, b_ada):
    L, D, N = w_ada.shape
    B = c.shape[0]
    tn = N // 8
    out = pl.pallas_call(
        _mod_kernel,
        grid=(L, N // tn),
        in_specs=[pl.BlockSpec((B, D), lambda l, n: (0, 0)),
                  pl.BlockSpec((1, D, tn), lambda l, n: (l, 0, n)),
                  pl.BlockSpec((1, 1, tn), lambda l, n: (l, 0, n))],
        out_specs=pl.BlockSpec((1, B, tn), lambda l, n: (l, 0, n)),
        out_shape=jax.ShapeDtypeStruct((L, B, N), F32),
        compiler_params=_params(2),
        name="adaln_mod",
    )(c, w_ada, b_ada.reshape(L, 1, N))
    return out.reshape(L, B, N_MOD, D)


def _rope_kernel(pos_ref, invf_ref, sgn_ref, cos_ref, sin_ref):
    ang = invf_ref[...] * pos_ref[0].astype(F32)
    cos_ref[0] = jnp.cos(ang)
    sin_ref[0] = jnp.sin(ang) * sgn_ref[...]


def _rope_tables(positions):
    B, S = positions.shape
    ts = min(S, 2048)
    inv_freq = 1.0 / (ROPE_THETA ** (jnp.arange(0, QK_ROPE, 2, dtype=F32) / QK_ROPE))
    invf = jnp.concatenate([inv_freq, inv_freq])[:, None]
    ones = jnp.ones((HALF_ROPE,), F32)
    sgn = jnp.concatenate([-ones, ones])[:, None]
    tab = jax.ShapeDtypeStruct((B, QK_ROPE, S), F32)
    col = pl.BlockSpec((QK_ROPE, 1), lambda b, i: (0, 0))
    out = pl.BlockSpec((1, QK_ROPE, ts), lambda b, i: (b, 0, i))
    return pl.pallas_call(
        _rope_kernel,
        grid=(B, S // ts),
        in_specs=[pl.BlockSpec((1, 1, ts), lambda b, i: (b, 0, i)), col, col],
        out_specs=[out, out],
        out_shape=[tab, tab],
        compiler_params=_params(2),
        name="rope_tables",
    )(positions.reshape(B, 1, S), invf, sgn)


def _swiglu_rows(x, mod_ref, mod_base, g_ref, wgu_ref, wd_ref):
    shift = mod_ref[0, mod_base:mod_base + 1, :]
    scale = mod_ref[0, mod_base + 1:mod_base + 2, :]
    gate = mod_ref[0, mod_base + 2:mod_base + 3, :]
    h = (_rms(x, g_ref[...]) * (1.0 + scale) + shift).astype(BF16)
    gu = jnp.dot(h, wgu_ref[...], preferred_element_type=F32)
    a = (_silu(gu[:, :D_FF]) * gu[:, D_FF:]).astype(BF16)
    y = jnp.dot(a, wd_ref[...], preferred_element_type=F32)
    return x + (0.5 * (1.0 + gate)) * y


def _merge_ffn_kernel(x_ref, mod_ref, attn_ref, conv_ref, ag_ref, wo_ref, g_ref, wgu_ref, wd_ref, *rest, final):
    o_ref = rest[-1]
    an = _rms(attn_ref[0], ag_ref[...]).astype(BF16)
    ym = jnp.dot(an, wo_ref[0:ATTN_W, :], preferred_element_type=F32)
    ym = ym + jnp.dot(conv_ref[0], wo_ref[ATTN_W:, :], preferred_element_type=F32)
    x = x_ref[0] + (1.0 + mod_ref[0, 5:6, :]) * ym
    out = _swiglu_rows(x, mod_ref, 6, g_ref, wgu_ref, wd_ref)
    if final:
        out = _rms(out, rest[0][...])
    o_ref[0] = out


def _merge_ffn(x, mod_l, attn, conv, ag, wo, g, wgu, wd, layer, tm, final_g=None):
    B, S, D = x.shape
    tok = lambda w: pl.BlockSpec((1, tm, w), lambda b, i: (b, i, 0))
    wspec = lambda r, c: _resident((None, r, c), lambda b, i: (layer, 0, 0))
    in_specs = [tok(D), _mod_spec(layer), tok(ATTN_W), tok(CONV_W), _row_spec(ATTN_W, layer),
                wspec(ATTN_W + CONV_W, D), _row_spec(D, layer), wspec(D, 2 * D_FF), wspec(D_FF, D)]
    args = [x, mod_l, attn, conv, ag, wo, g, wgu, wd]
    if final_g is not None:
        in_specs.append(pl.BlockSpec((1, D), lambda b, i: (0, 0)))
        args.append(final_g)
    return pl.pallas_call(
        functools.partial(_merge_ffn_kernel, final=final_g is not None),
        grid=(B, S // tm),
        in_specs=in_specs,
        out_specs=tok(D),
        out_shape=jax.ShapeDtypeStruct((B, S, D), F32),
        compiler_params=_params(2),
        name="merge_ffn_final" if final_g is not None else "merge_ffn",
    )(*args)


def _rope_t(rows, cos, sin):
    partner = jnp.concatenate([rows[HALF_ROPE:], rows[:HALF_ROPE]], axis=0)
    return rows * cos + partner * sin


def _ffn_proj_kernel(x_ref, mod_ref, g1_ref, wgu_ref, wd_ref,
                     g_ref, win_ref, wkpe_ref, qg_ref, wuq_ref, kvg_ref, wk_ref, wv_ref,
                     cw_ref, cg_ref, cos_ref, sin_ref,
                     x1_ref, qt_ref, k_ref, vt_ref, conv_ref, hist_ref, *, tm):
    @pl.when(pl.program_id(1) == 0)
    def _():
        hist_ref[0:HALO, :] = jnp.zeros((HALO, CONV_W), F32)

    for r in range(FFN_ROW_GROUPS):
        rows = slice(r * tm // FFN_ROW_GROUPS, (r + 1) * tm // FFN_ROW_GROUPS)
        x1_ref[0, rows, :] = _swiglu_rows(x_ref[0, rows, :], mod_ref, 0, g1_ref, wgu_ref, wd_ref)

    x = x1_ref[0]
    shift = mod_ref[0, 3:4, :]
    scale = mod_ref[0, 4:5, :]
    h = (_rms(x, g_ref[...]) * (1.0 + scale) + shift).astype(BF16)
    proj = jnp.dot(h, win_ref[:, 0:P_GB], preferred_element_type=F32)
    cos = cos_ref[0]
    sin = sin_ref[0]
    rope_lo, rope_hi = QK_NOPE, QK_NOPE + QK_ROPE

    cqn = _rms(proj[:, P_CQ:P_CQ + Q_RANK], qg_ref[...]).astype(BF16)
    qt = lax.dot_general(wuq_ref[...], cqn, NT_DIMS, preferred_element_type=F32)
    gates = jnp.dot(h, win_ref[:, P_GB:P_WIDTH], preferred_element_type=F32)
    ckvn = _rms(proj[:, P_CKV:P_CKV + KV_RANK], kvg_ref[...]).astype(BF16)
    kk = jnp.dot(ckvn, wk_ref[...], preferred_element_type=F32)
    vt = lax.dot_general(wv_ref[...], ckvn, NT_DIMS, preferred_element_type=F32)
    kpe_t = lax.dot_general(wkpe_ref[...], h, NT_DIMS, preferred_element_type=F32)
    kpe_t = jnp.concatenate([jnp.zeros((rope_lo, tm), F32), _rope_t(kpe_t, cos, sin),
                             jnp.zeros((HEAD_SLAB - rope_hi, tm), F32)], axis=0)
    kpe = kpe_t.T
    for hd in range(N_HEADS):
        lo = hd * HEAD_SLAB
        qh = qt[lo:lo + HEAD_SLAB]
        qh = jnp.concatenate([qh[:rope_lo], _rope_t(qh[rope_lo:rope_hi], cos, sin), qh[rope_hi:]], axis=0)
        qt_ref[0, hd] = (qh * Q_SCALE).astype(BF16)
        k_ref[0, hd] = (kk[:, lo:lo + HEAD_SLAB] + kpe).astype(BF16)
    vt_ref[0] = vt.astype(BF16)

    cv =gates[:, CONV_W:2 * CONV_W] * gates[:, 2 * CONV_W:3 * CONV_W]
    hist_ref[HALO:HALO + tm, :] = cv
    cv1 = hist_ref[HALO - 1:HALO - 1 + tm, :]
    cv2 = hist_ref[HALO - 2:HALO - 2 + tm, :]
    conv = cw_ref[0:1, :] * cv2 + cw_ref[1:2, :] * cv1 + cw_ref[2:3, :] * cv
    conv = gates[:, 0:CONV_W] * conv
    conv_ref[0] = _rms(conv, cg_ref[...]).astype(BF16)
    hist_ref[0:HALO, :] = cv[tm - HALO:, :]


def _ffn_proj(x, mod_l, g1, wgu, wd, g, win, wkpe, qg, wuq, kvg, wk, wv, cw, cg, cos, sin, layer, tm):
    B, S, D = x.shape
    tok = lambda w: pl.BlockSpec((1, tm, w), lambda b, i: (b, i, 0))
    tok_t = lambda r: pl.BlockSpec((1, r, tm), lambda b, i: (b, 0, i))
    row = lambda w: _row_spec(w, layer)
    wspec = lambda r, c: _resident((None, r, c), lambda b, i: (layer, 0, 0))
    return pl.pallas_call(
        functools.partial(_ffn_proj_kernel, tm=tm),
        grid=(B, S // tm),
        in_specs=[tok(D),
                  _mod_spec(layer),
                  row(D), wspec(D, 2 * D_FF), wspec(D_FF, D),
                  row(D), wspec(D, P_WIDTH), wspec(QK_ROPE, D),
                  row(Q_RANK), wspec(N_HEADS * HEAD_SLAB, Q_RANK),
                  row(KV_RANK), wspec(KV_RANK, N_HEADS * HEAD_SLAB), wspec(ATTN_W, KV_RANK),
                  pl.BlockSpec((None, 3, CONV_W), lambda b, i: (layer, 0, 0)),
                  row(CONV_W), tok_t(QK_ROPE), tok_t(QK_ROPE)],
        out_specs=[tok(D),
                   pl.BlockSpec((1, N_HEADS, HEAD_SLAB, tm), lambda b, i: (b, 0, 0, i)),
                   pl.BlockSpec((1, N_HEADS, tm, HEAD_SLAB), lambda b, i: (b, 0, i, 0)),
                   tok_t(ATTN_W), tok(CONV_W)],
        out_shape=[jax.ShapeDtypeStruct((B, S, D), F32),
                   jax.ShapeDtypeStruct((B, N_HEADS, HEAD_SLAB, S), BF16),
                   jax.ShapeDtypeStruct((B, N_HEADS, S, HEAD_SLAB), BF16),
                   jax.ShapeDtypeStruct((B, ATTN_W, S), BF16),
                   jax.ShapeDtypeStruct((B, S, CONV_W), BF16)],
        scratch_shapes=[pltpu.VMEM((HALO + tm, CONV_W), F32)],
        compiler_params=_params(2),
        name="ffn_proj",
    )(x, mod_l, g1, wgu, wd, g, win, wkpe, qg, wuq, kvg, wk, wv, cw, cg, cos, sin)


def _attn_kernel(qt_ref, k_ref, vt_ref, o_ref, s_ref, acc_ref, cm_ref, *, tq, n_tiles):
    g = pl.program_id(2)
    hb = MXU_TILE
    krow = lax.broadcasted_iota(jnp.int32, (hb, hb), 0)
    qcol = lax.broadcasted_iota(jnp.int32, (hb, hb), 1)
    tri = krow <= qcol
    zpad = jnp.zeros((hb, hb - HEAD_SLAB), BF16)
    zrows = jnp.zeros((hb - HEAD_SLAB, hb), BF16)
    ones = jnp.ones((VALUE_ROWS - V_DIM, hb), BF16)
    heads = range(HEADS_PER_STEP)

    def q_half(h, tile, qh):
        start = pl.multiple_of(tile * tq + qh * hb, hb)
        return jnp.concatenate([qt_ref[0, h, :, pl.ds(start, hb)], zrows], axis=0)

    def k_half(h, blk, kh):
        start = pl.multiple_of(blk * tq + kh * hb, hb)
        return jnp.concatenate([k_ref[0, h, pl.ds(start, hb), :], zpad], axis=1)

    def v_half(h, blk, kh):
        start = pl.multiple_of(blk * tq + kh * hb, hb)
        return jnp.concatenate([vt_ref[0, h * V_DIM:(h + 1) * V_DIM, pl.ds(start, hb)], ones], axis=0)

    def quad_ref(h, kh, qh):
        return s_ref.at[h, 2 * kh + qh]

    def prestage(h, tile):
        pltpu.matmul_push_rhs(q_half(h, tile, 0), 0, h)

    def drain_stage(h):
        pltpu.matmul_acc_lhs(SCORE_SLOTS[1], jnp.zeros((16, hb), BF16), h, 0)
        pltpu.matmul_pop(SCORE_SLOTS[1], (16, hb), F32, h)

    def issue3(h, tile, blk):
        pltpu.matmul_acc_lhs(SCORE_SLOTS[0], k_half(h, blk, 0), h, 0)
        pltpu.matmul_acc_lhs(SCORE_SLOTS[1], k_half(h, blk, 1), h, None)
        pltpu.matmul_push_rhs(q_half(h, tile, 1), 0, h)
        pltpu.matmul_acc_lhs(SCORE_SLOTS[2], k_half(h, blk, 0), h, 0)

    def issue4(h, blk):
        pltpu.matmul_acc_lhs(SCORE_SLOTS[0], k_half(h, blk, 1), h, None)

    def pop_quad(h, slot, kh, qh, diag):
        st = pltpu.matmul_pop(slot, (hb, hb), F32, h)
        if diag:
            if kh > qh:
                st = jnp.full((hb, hb), -jnp.inf, F32)
            elif kh == qh:
                st = jnp.where(tri, st, -jnp.inf)
        quad_ref(h, kh, qh)[...] = st
        return jnp.max(st, axis=0, keepdims=True)

    def numer(h, kh, qh, m_new, reg):
        mq = m_new[:, qh * hb:(qh + 1) * hb]
        p = jnp.exp2(quad_ref(h, kh, qh)[...] - mq).astype(BF16)
        pltpu.matmul_push_rhs(p, reg, h)

    def pv(h, blk, qh):
        pltpu.matmul_acc_lhs(OUT_SLOTS[qh], v_half(h, blk, 0), h, 1)
        pltpu.matmul_acc_lhs(OUT_SLOTS[qh], v_half(h, blk, 1), h, 0)

    def accumulate(h, alpha):
        out = jnp.concatenate([pltpu.matmul_pop(OUT_SLOTS[qh], (VALUE_ROWS, hb), F32, h) for qh in range(2)], axis=1)
        acc_ref[h] = alpha * acc_ref[h] + out

    def col_max(c):
        return jnp.concatenate([jnp.maximum(c[0], c[1]), jnp.maximum(c[2], c[3])], axis=1)

    def order(i, t):
        return jnp.where(t == 0, i, t - 1)

    @pl.when(g == 0)
    def _():
        for h in heads:
            drain_stage(h)
            prestage(h, 0)
            issue3(h, 0, 0)
        for h in heads:
            c = [pop_quad(h, SCORE_SLOTS[0], 0, 0, True)]
            issue4(h, 0)
            c.append(pop_quad(h, SCORE_SLOTS[1], 1, 0, True))
            c.append(pop_quad(h, SCORE_SLOTS[2], 0, 1, True))
            c.append(pop_quad(h, SCORE_SLOTS[0], 1, 1, True))
            cm_ref[h] = col_max(c)
            prestage(h, min(1, n_tiles - 1))

    def step(blk, state, nxt, staged_tile=None):
        ms, alphas, cs = [], [], [[] for _ in heads]
        for h in heads:
            m, cm, _ = state[h]
            m_new = jnp.maximum(m, cm)
            ms.append(m_new)
            alphas.append(jnp.exp2(m - m_new))
        for h in heads:
            numer(h, 0, 0, ms[h], 1)
        for h in heads:
            if nxt is not None:
                issue3(h, nxt[0], nxt[1])
            else:
                drain_stage(h)
        for h in heads:
            if nxt is not None:
                cs[h].append(pop_quad(h, SCORE_SLOTS[0], 0, 0, nxt[2]))
                issue4(h, nxt[1])
        for h in heads:
            numer(h, 1, 0, ms[h], 0)
        if nxt is not None:
            for h in heads:
                cs[h].append(pop_quad(h, SCORE_SLOTS[1], 1, 0, nxt[2]))
        for h in heads:
            accumulate(h, state[h][2])
            pv(h, blk, 0)
        for h in heads:
            numer(h, 0, 1, ms[h], 1)
            numer(h, 1, 1, ms[h], 0)
            pv(h, blk, 1)
        out = []
        for h in heads:
            if nxt is not None:
                cs[h].append(pop_quad(h, SCORE_SLOTS[2], 0, 1, nxt[2]))
                cs[h].append(pop_quad(h, SCORE_SLOTS[0], 1, 1, nxt[2]))
                prestage(h, nxt[0] if staged_tile is None else staged_tile)
                out.append((ms[h], col_max(cs[h]), alphas[h]))
            else:
                out.append((ms[h], state[h][1], alphas[h]))
        return tuple(out)

    def run_tile(i, sub):
        for h in heads:
            acc_ref[h] = jnp.zeros((VALUE_ROWS, tq), F32)
        state = tuple((jnp.full((1, tq), -jnp.inf, F32), cm_ref[h], jnp.ones((1, tq), F32)) for h in heads)
        next_tile = jnp.minimum(i + 1, n_tiles - 1)
        state = lax.fori_loop(
            0, i,
            lambda t, st: step(order(i, t), st, (i, t, False), jnp.where(t == i - 1, next_tile, i)),
            state)

        def finish(nxt):
            last = step(order(i, i), state, nxt)
            outs = []
            for h in heads:
                if nxt is not None:
                    cm_ref[h] = last[h][1]
                accumulate(h, last[h][2])
                acc = acc_ref[h]
                outs.append(acc[0:V_DIM, :] / acc[V_DIM:V_DIM + 1, :])
            o_ref[0, sub * tq:(sub + 1) * tq, :] = jnp.concatenate(outs, axis=0).T

        if sub < TILES_PER_STEP - 1:
            finish((i + 1, i + 1, True))
        else:
            @pl.when(i < n_tiles - 1)
            def _():
                finish((i + 1, i + 1, True))

            @pl.when(i == n_tiles - 1)
            def _():
                finish(None)

    for sub in range(TILES_PER_STEP):
        run_tile(g * TILES_PER_STEP + sub, sub)


def _attention(qt, k, vt, tq):
    B, H, S, _ = k.shape
    hps = HEADS_PER_STEP
    assert tq == 2 * MXU_TILE and HEAD_SLAB <= MXU_TILE and S % (tq * TILES_PER_STEP) == 0
    return pl.pallas_call(
        functools.partial(_attn_kernel, tq=tq, n_tiles=S // tq),
        grid=(B, H // hps, S // (tq * TILES_PER_STEP)),
        in_specs=[pl.BlockSpec((1, hps, HEAD_SLAB, S), lambda b, p, i: (b, p, 0, 0)),
                  pl.BlockSpec((1, hps, S, HEAD_SLAB), lambda b, p, i: (b, p, 0, 0)),
                  pl.BlockSpec((1, hps * V_DIM, S), lambda b, p, i: (b, p, 0))],
        out_specs=pl.BlockSpec((1, tq * TILES_PER_STEP, hps * V_DIM), lambda b, p, i: (b, i, p)),
        out_shape=jax.ShapeDtypeStruct((B, S, ATTN_W), F32),
        scratch_shapes=[pltpu.VMEM((hps, 4, MXU_TILE, MXU_TILE), F32),
                        pltpu.VMEM((hps, VALUE_ROWS, tq), F32),
                        pltpu.VMEM((hps, 1, tq), F32)],
        compiler_params=_params(3),
        name="mla_attention",
    )(qt, k, vt)


def _layout_w_in(w_in):
    lat = Q_RANK + KV_RANK
    main = jnp.concatenate([w_in[:, :, :lat], w_in[:, :, lat + QK_ROPE:]], axis=-1)
    kpe_t = jnp.swapaxes(w_in[:, :, lat:lat + QK_ROPE], 1, 2)
    return main.astype(BF16), kpe_t.astype(BF16)


def _layout_w_uq(w_uq):
    L, R, _ = w_uq.shape
    w = w_uq.reshape(L, R, N_HEADS, QK_NOPE + QK_ROPE)
    z_pad = jnp.zeros((L, R, N_HEADS, HEAD_SLAB - QK_NOPE - QK_ROPE), w.dtype)
    q = jnp.concatenate([w, z_pad], axis=-1).reshape(L, R, N_HEADS * HEAD_SLAB)
    return jnp.swapaxes(q, 1, 2).astype(BF16)


def _layout_w_ukv(w_ukv):
    L, R, _ = w_ukv.shape
    w = w_ukv.reshape(L, R, N_HEADS, QK_NOPE + V_DIM)
    k_nope, v = w[..., :QK_NOPE], w[..., QK_NOPE:]
    z = jnp.zeros((L, R, N_HEADS, HEAD_SLAB - QK_NOPE), w.dtype)
    wk = jnp.concatenate([k_nope, z], axis=-1).reshape(L, R, N_HEADS * HEAD_SLAB)
    wv_t = jnp.swapaxes(v.reshape(L, R, ATTN_W), 1, 2)
    return wk.astype(BF16), wv_t.astype(BF16)


def kernel(x, c, positions, w_ada, b_ada, ffn1_norm, ffn1_w_gu, ffn1_w_down, mix_norm, w_in, q_a_norm, w_uq, kv_a_norm, w_ukv, conv_w, attn_out_norm, conv_out_norm, w_o, ffn2_norm, ffn2_w_gu, ffn2_w_down, final_norm):
    B, S, D = x.shape
    L = w_ada.shape[0]
    tm = min(S, 512)
    tq = min(S, 512)

    mod = _modulation(c, w_ada, b_ada)
    cos, sin = _rope_tables(positions)

    wgu1, wd1 = ffn1_w_gu.astype(BF16), ffn1_w_down.astype(BF16)
    wgu2, wd2 = ffn2_w_gu.astype(BF16), ffn2_w_down.astype(BF16)
    win, wkpe_t = _layout_w_in(w_in)
    wuq_t = _layout_w_uq(w_uq)
    wk, wv_t = _layout_w_ukv(w_ukv)
    wo = w_o.astype(BF16)
    final_g = final_norm.reshape(1, D)
    rows = lambda g: g.reshape(L, 1, g.shape[-1])
    g1, g2, gm = rows(ffn1_norm), rows(ffn2_norm), rows(mix_norm)
    gq, gkv = rows(q_a_norm), rows(kv_a_norm)
    ga, gc = rows(attn_out_norm), rows(conv_out_norm)

    for l in range(L):
        x, qt, k, vt, conv = _ffn_proj(x, mod, g1, wgu1, wd1, gm, win, wkpe_t, gq, wuq_t, gkv, wk, wv_t,
                                       conv_w, gc, cos, sin, l, tm)
        attn = _attention(qt, k, vt, tq)
        x = _merge_ffn(x, mod, attn, conv, ga, wo, g2, wgu2, wd2, l, tm,
                       final_g=final_g if l == L - 1 else None)
    return x
```

```python
import functools

import jax
import jax.numpy as jnp
from jax import lax
from jax.experimental import pallas as pl
from jax.experimental.pallas import tpu as pltpu

F32 = jnp.float32
BF16 = jnp.bfloat16

D_MODEL = 1024
N_HEADS = 8
QK_NOPE = 64
QK_ROPE = 32
V_DIM = 64
Q_RANK = 384
KV_RANK = 256
CONV_W = 512
ATTN_W = N_HEADS * V_DIM
D_FF = 2816
N_MOD = 9
EPS = 1e-6
ROPE_THETA = 10000.0
HEAD_SLAB = 128
HALF_ROPE = QK_ROPE // 2
P_CQ = 0
P_CKV = P_CQ + Q_RANK
P_GB = P_CKV + KV_RANK
P_WIDTH = P_GB + 3 * CONV_W
HALO = 8
Q_SCALE = float((QK_NOPE + QK_ROPE) ** -0.5 * 1.4426950408889634)
VMEM_LIMIT = 56 * 1024 * 1024
FFN_ROW_GROUPS = 2
HEADS_PER_STEP = 2
TILES_PER_STEP = 2
MXU_TILE = 256
VALUE_ROWS = V_DIM + 16
SCORE_SLOTS = (0, MXU_TILE // 4, 2 * MXU_TILE // 4)
OUT_SLOTS = (3 * MXU_TILE // 4, 3 * MXU_TILE // 4 + VALUE_ROWS // 4)
NT_DIMS = (((1,), (1,)), ((), ()))


def _rms(x, g):
    return x * lax.rsqrt(jnp.mean(x * x, axis=-1, keepdims=True) + EPS) * g


def _silu(x):
    return x / (1.0 + jnp.exp(-x))


def _params(n_axes):
    return pltpu.CompilerParams(dimension_semantics=("arbitrary",) * n_axes,
                                vmem_limit_bytes=VMEM_LIMIT)


def _resident(shape, index_map):
    return pl.BlockSpec(shape, index_map, pipeline_mode=pl.Buffered(1))


def _row_spec(width, layer):
    return pl.BlockSpec((None, 1, width), lambda *_: (layer, 0, 0))


def _mod_spec(layer):
    return pl.BlockSpec((None, 1, N_MOD, D_MODEL), lambda b, *_: (layer, b, 0, 0))


def _mod_kernel(c_ref, w_ref, b_ref, o_ref):
    c = c_ref[...]
    ca = _silu(c).astype(BF16)
    o_ref[0] = jnp.dot(ca, w_ref[0].astype(BF16), preferred_element_type=F32) + b_ref[0]


def _modulation(c, w_ada, b_ada):
    L, D, N = w_ada.shape
    B = c.shape[0]
    tn = N // 8
    out = pl.pallas_call(
        _mod_kernel,
        grid=(L, N // tn),
        in_specs=[pl.BlockSpec((B, D), lambda l, n: (0, 0)),
                  pl.BlockSpec((1, D, tn), lambda l, n: (l, 0, n)),
                  pl.BlockSpec((1, 1, tn), lambda l, n: (l, 0, n))],
        out_specs=pl.BlockSpec((1, B, tn), lambda l, n: (l, 0, n)),
        out_shape=jax.ShapeDtypeStruct((L, B, N), F32),
        compiler_params=_params(2),
        name="adaln_mod",
    )(c, w_ada, b_ada.reshape(L, 1, N))
    return out.reshape(L, B, N_MOD, D)


def _rope_kernel(pos_ref, invf_ref, sgn_ref, cos_ref, sin_ref):
    ang = invf_ref[...] * pos_ref[0].astype(F32)
    cos_ref[0] = jnp.cos(ang)
    sin_ref[0] = jnp.sin(ang) * sgn_ref[...]


def _rope_tables(positions):
    B, S = positions.shape
    ts = min(S, 2048)
    inv_freq = 1.0 / (ROPE_THETA ** (jnp.arange(0, QK_ROPE, 2, dtype=F32) / QK_ROPE))
    invf = jnp.concatenate([inv_freq, inv_freq])[:, None]
    ones = jnp.ones((HALF_ROPE,), F32)
    sgn = jnp.concatenate([-ones, ones])[:, None]
    tab = jax.ShapeDtypeStruct((B, QK_ROPE, S), F32)
    col = pl.BlockSpec((QK_ROPE, 1), lambda b, i: (0, 0))
    out = pl.BlockSpec((1, QK_ROPE, ts), lambda b, i: (b, 0, i))
    return pl.pallas_call(
        _rope_kernel,
        grid=(B, S // ts),
        in_specs=[pl.BlockSpec((1, 1, ts), lambda b, i: (b, 0, i)), col, col],
        out_specs=[out, out],
        out_shape=[tab, tab],
        compiler_params=_params(2),
        name="rope_tables",
    )(positions.reshape(B, 1, S), invf, sgn)


def _swiglu_rows(x, mod_ref, mod_base, g_ref, wgu_ref, wd_ref):
    shift = mod_ref[0, mod_base:mod_base + 1, :]
    scale = mod_ref[0, mod_base + 1:mod_base + 2, :]
    gate = mod_ref[0, mod_base + 2:mod_base + 3, :]
    h = (_rms(x, g_ref[...]) * (1.0 + scale) + shift).astype(BF16)
    gu = jnp.dot(h, wgu_ref[...], preferred_element_type=F32)
    a = (_silu(gu[:, :D_FF]) * gu[:, D_FF:]).astype(BF16)
    y = jnp.dot(a, wd_ref[...], preferred_element_type=F32)
    return x + (0.5 * (1.0 + gate)) * y


def _merge_ffn_kernel(x_ref, mod_ref, attn_ref, conv_ref, ag_ref, wo_ref, g_ref, wgu_ref, wd_ref, *rest, final):
    o_ref = rest[-1]
    an = _rms(attn_ref[0], ag_ref[...]).astype(BF16)
    ym = jnp.dot(an, wo_ref[0:ATTN_W, :], preferred_element_type=F32)
    ym = ym + jnp.dot(conv_ref[0], wo_ref[ATTN_W:, :], preferred_element_type=F32)
    x = x_ref[0] + (1.0 + mod_ref[0, 5:6, :]) * ym
    out = _swiglu_rows(x, mod_ref, 6, g_ref, wgu_ref, wd_ref)
    if final:
        out = _rms(out, rest[0][...])
    o_ref[0] = out


def _merge_ffn(x, mod_l, attn, conv, ag, wo, g, wgu, wd, layer, tm, final_g=None):
    B, S, D = x.shape
    tok = lambda w: pl.BlockSpec((1, tm, w), lambda b, i: (b, i, 0))
    wspec = lambda r, c: _resident((None, r, c), lambda b, i: (layer, 0, 0))
    in_specs = [tok(D), _mod_spec(layer), tok(ATTN_W), tok(CONV_W), _row_spec(ATTN_W, layer),
                wspec(ATTN_W + CONV_W, D), _row_spec(D, layer), wspec(D, 2 * D_FF), wspec(D_FF, D)]
    args = [x, mod_l, attn, conv, ag, wo, g, wgu, wd]
    if final_g is not None:
        in_specs.append(pl.BlockSpec((1, D), lambda b, i: (0, 0)))
        args.append(final_g)
    return pl.pallas_call(
        functools.partial(_merge_ffn_kernel, final=final_g is not None),
        grid=(B, S // tm),
        in_specs=in_specs,
        out_specs=tok(D),
        out_shape=jax.ShapeDtypeStruct((B, S, D), F32),
        compiler_params=_params(2),
        name="merge_ffn_final" if final_g is not None else "merge_ffn",
    )(*args)


def _rope_t(rows, cos, sin):
    partner = jnp.concatenate([rows[HALF_ROPE:], rows[:HALF_ROPE]], axis=0)
    return rows * cos + partner * sin


def _ffn_proj_kernel(x_ref, mod_ref, g1_ref, wgu_ref, wd_ref,
                     g_ref, win_ref, wkpe_ref, qg_ref, wuq_ref, kvg_ref, wk_ref, wv_ref,
                     cw_ref, cg_ref, cos_ref, sin_ref,
                     x1_ref, qt_ref, k_ref, vt_ref, conv_ref, hist_ref, *, tm):
    @pl.when(pl.program_id(1) == 0)
    def _():
        hist_ref[0:HALO, :] = jnp.zeros((HALO, CONV_W), F32)

    for r in range(FFN_ROW_GROUPS):
        rows = slice(r * tm // FFN_ROW_GROUPS, (r + 1) * tm // FFN_ROW_GROUPS)
        x1_ref[0, rows, :] = _swiglu_rows(x_ref[0, rows, :], mod_ref, 0, g1_ref, wgu_ref, wd_ref)

    x = x1_ref[0]
    shift = mod_ref[0, 3:4, :]
    scale = mod_ref[0, 4:5, :]
    h = (_rms(x, g_ref[...]) * (1.0 + scale) + shift).astype(BF16)
    proj = jnp.dot(h, win_ref[:, 0:P_GB], preferred_element_type=F32)
    cos = cos_ref[0]
    sin = sin_ref[0]
    rope_lo, rope_hi = QK_NOPE, QK_NOPE + QK_ROPE

    cqn = _rms(proj[:, P_CQ:P_CQ + Q_RANK], qg_ref[...]).astype(BF16)
    qt = lax.dot_general(wuq_ref[...], cqn, NT_DIMS, preferred_element_type=F32)
    gates = jnp.dot(h, win_ref[:, P_GB:P_WIDTH], preferred_element_type=F32)
    ckvn = _rms(proj[:, P_CKV:P_CKV + KV_RANK], kvg_ref[...]).astype(BF16)
    kk = jnp.dot(ckvn, wk_ref[...], preferred_element_type=F32)
    vt = lax.dot_general(wv_ref[...], ckvn, NT_DIMS, preferred_element_type=F32)
    kpe_t = lax.dot_general(wkpe_ref[...], h, NT_DIMS, preferred_element_type=F32)
    kpe_t = jnp.concatenate([jnp.zeros((rope_lo, tm), F32), _rope_t(kpe_t, cos, sin),
                             jnp.zeros((HEAD_SLAB - rope_hi, tm), F32)], axis=0)
    kpe = kpe_t.T
    for hd in range(N_HEADS):
        lo = hd * HEAD_SLAB
        qh = qt[lo:lo + HEAD_SLAB]
        qh = jnp.concatenate([qh[:rope_lo], _rope_t(qh[rope_lo:rope_hi], cos, sin), qh[rope_hi:]], axis=0)
        qt_ref[0, hd] = (qh * Q_SCALE).astype(BF16)
        k_ref[0, hd] = (kk[:, lo:lo + HEAD_SLAB] + kpe).astype(BF16)
    vt_ref[0] = vt.astype(BF16)

    cv =gates[:, CONV_W:2 * CONV_W] * gates[:, 2 * CONV_W:3 * CONV_W]
    hist_ref[HALO:HALO + tm, :] = cv
    cv1 = hist_ref[HALO - 1:HALO - 1 + tm, :]
    cv2 = hist_ref[HALO - 2:HALO - 2 + tm, :]
    conv = cw_ref[0:1, :] * cv2 + cw_ref[1:2, :] * cv1 + cw_ref[2:3, :] * cv
    conv = gates[:, 0:CONV_W] * conv
    conv_ref[0] = _rms(conv, cg_ref[...]).astype(BF16)
    hist_ref[0:HALO, :] = cv[tm - HALO:, :]


def _ffn_proj(x, mod_l, g1, wgu, wd, g, win, wkpe, qg, wuq, kvg, wk, wv, cw, cg, cos, sin, layer, tm):
    B, S, D = x.shape
    tok = lambda w: pl.BlockSpec((1, tm, w), lambda b, i: (b, i, 0))
    tok_t = lambda r: pl.BlockSpec((1, r, tm), lambda b, i: (b, 0, i))
    row = lambda w: _row_spec(w, layer)
    wspec = lambda r, c: _resident((None, r, c), lambda b, i: (layer, 0, 0))
    return pl.pallas_call(
        functools.partial(_ffn_proj_kernel, tm=tm),
        grid=(B, S // tm),
        in_specs=[tok(D),
                  _mod_spec(layer),
                  row(D), wspec(D, 2 * D_FF), wspec(D_FF, D),
                  row(D), wspec(D, P_WIDTH), wspec(QK_ROPE, D),
                  row(Q_RANK), wspec(N_HEADS * HEAD_SLAB, Q_RANK),
                  row(KV_RANK), wspec(KV_RANK, N_HEADS * HEAD_SLAB), wspec(ATTN_W, KV_RANK),
                  pl.BlockSpec((None, 3, CONV_W), lambda b, i: (layer, 0, 0)),
                  row(CONV_W), tok_t(QK_ROPE), tok_t(QK_ROPE)],
        out_specs=[tok(D),
                   pl.BlockSpec((1, N_HEADS, HEAD_SLAB, tm), lambda b, i: (b, 0, 0, i)),
                   pl.BlockSpec((1, N_HEADS, tm, HEAD_SLAB), lambda b, i: (b, 0, i, 0)),
                   tok_t(ATTN_W), tok(CONV_W)],
        out_shape=[jax.ShapeDtypeStruct((B, S, D), F32),
                   jax.ShapeDtypeStruct((B, N_HEADS, HEAD_SLAB, S), BF16),
                   jax.ShapeDtypeStruct((B, N_HEADS, S, HEAD_SLAB), BF16),
                   jax.ShapeDtypeStruct((B, ATTN_W, S), BF16),
                   jax.ShapeDtypeStruct((B, S, CONV_W), BF16)],
        scratch_shapes=[pltpu.VMEM((HALO + tm, CONV_W), F32)],
        compiler_params=_params(2),
        name="ffn_proj",
    )(x, mod_l, g1, wgu, wd, g, win, wkpe, qg, wuq, kvg, wk, wv, cw, cg, cos, sin)


def _attn_kernel(qt_ref, k_ref, vt_ref, o_ref, s_ref, acc_ref, cm_ref, *, tq, n_tiles):
    g = pl.program_id(2)
    hb = MXU_TILE
    krow = lax.broadcasted_iota(jnp.int32, (hb, hb), 0)
    qcol = lax.broadcasted_iota(jnp.int32, (hb, hb), 1)
    tri = krow <= qcol
    zpad = jnp.zeros((hb, hb - HEAD_SLAB), BF16)
    zrows = jnp.zeros((hb - HEAD_SLAB, hb), BF16)
    ones = jnp.ones((VALUE_ROWS - V_DIM, hb), BF16)
    heads = range(HEADS_PER_STEP)

    def q_half(h, tile, qh):
        start = pl.multiple_of(tile * tq + qh * hb, hb)
        return jnp.concatenate([qt_ref[0, h, :, pl.ds(start, hb)], zrows], axis=0)

    def k_half(h, blk, kh):
        start = pl.multiple_of(blk * tq + kh * hb, hb)
        return jnp.concatenate([k_ref[0, h, pl.ds(start, hb), :], zpad], axis=1)

    def v_half(h, blk, kh):
        start = pl.multiple_of(blk * tq + kh * hb, hb)
        return jnp.concatenate([vt_ref[0, h * V_DIM:(h + 1) * V_DIM, pl.ds(start, hb)], ones], axis=0)

    def quad_ref(h, kh, qh):
        return s_ref.at[h, 2 * kh + qh]

    def prestage(h, tile):
        pltpu.matmul_push_rhs(q_half(h, tile, 0), 1, h)

    def drain_stage(h):
        pltpu.matmul_acc_lhs(SCORE_SLOTS[1], jnp.zeros((16, hb), BF16), h, 1)
        pltpu.matmul_pop(SCORE_SLOTS[1], (16, hb), F32, h)

    def issue3(h, tile, blk):
        pltpu.matmul_acc_lhs(SCORE_SLOTS[0], k_half(h, blk, 0), h, 1)
        pltpu.matmul_acc_lhs(SCORE_SLOTS[1], k_half(h, blk, 1), h, None)
        pltpu.matmul_push_rhs(q_half(h, tile, 1), 0, h)
        pltpu.matmul_acc_lhs(SCORE_SLOTS[2], k_half(h, blk, 0), h, 0)

    def issue4(h, blk):
        pltpu.matmul_acc_lhs(SCORE_SLOTS[0], k_half(h, blk, 1), h, None)

    def pop_quad(h, slot, kh, qh, diag):
        st = pltpu.matmul_pop(slot, (hb, hb), F32, h)
        if diag:
            if kh > qh:
                st = jnp.full((hb, hb), -jnp.inf, F32)
            elif kh == qh:
                st = jnp.where(tri, st, -jnp.inf)
        quad_ref(h, kh, qh)[...] = st
        return jnp.max(st, axis=0, keepdims=True)

    def numer(h, kh, qh, m_new, reg):
        mq = m_new[:, qh * hb:(qh + 1) * hb]
        p = jnp.exp2(quad_ref(h, kh, qh)[...] - mq).astype(BF16)
        pltpu.matmul_push_rhs(p, reg, h)

    def pv(h, blk, qh, between=None):
        pltpu.matmul_acc_lhs(OUT_SLOTS[qh], v_half(h, blk, 0), h, 1)
        if between is not None:
            between()
        pltpu.matmul_acc_lhs(OUT_SLOTS[qh], v_half(h, blk, 1), h, 0)

    def accumulate(h, alpha):
        out = jnp.concatenate([pltpu.matmul_pop(OUT_SLOTS[qh], (VALUE_ROWS, hb), F32, h) for qh in range(2)], axis=1)
        acc_ref[h] = alpha * acc_ref[h] + out

    def col_max(c):
        return jnp.concatenate([jnp.maximum(c[0], c[1]), jnp.maximum(c[2], c[3])], axis=1)

    def order(i, t):
        return jnp.where(t == 0, i, t - 1)

    @pl.when(g == 0)
    def _():
        for h in heads:
            drain_stage(h)
            prestage(h, 0)
            issue3(h, 0, 0)
        for h in heads:
            c = [pop_quad(h, SCORE_SLOTS[0], 0, 0, True)]
            issue4(h, 0)
            c.append(pop_quad(h, SCORE_SLOTS[1], 1, 0, True))
            c.append(pop_quad(h, SCORE_SLOTS[2], 0, 1, True))
            c.append(pop_quad(h, SCORE_SLOTS[0], 1, 1, True))
            cm_ref[h] = col_max(c)
            prestage(h, min(1, n_tiles - 1))

    def step(blk, state, nxt, staged_tile=None):
        ms, alphas, cs = [], [], [[] for _ in heads]
        for h in heads:
            m, cm, _ = state[h]
            m_new = jnp.maximum(m, cm)
            ms.append(m_new)
            alphas.append(jnp.exp2(m - m_new))
        for h in heads:
            if nxt is not None:
                issue3(h, nxt[0], nxt[1])
            else:
                drain_stage(h)
        for h in heads:
            numer(h, 0, 0, ms[h], 1)
        for h in heads:
            if nxt is not None:
                cs[h].append(pop_quad(h, SCORE_SLOTS[0], 0, 0, nxt[2]))
                issue4(h, nxt[1])
        for h in heads:
            numer(h, 1, 0, ms[h], 0)
        if nxt is not None:
            for h in heads:
                cs[h].append(pop_quad(h, SCORE_SLOTS[1], 1, 0, nxt[2]))
        for h in heads:
            accumulate(h, state[h][2])
            pv(h, blk, 0)
        for h in heads:
            numer(h, 0, 1, ms[h], 1)
            numer(h, 1, 1, ms[h], 0)
            pv(h, blk, 1, (lambda h=h: prestage(h, nxt[0] if staged_tile is None else staged_tile))
               if nxt is not None else None)
        out = []
        for h in heads:
            if nxt is not None:
                cs[h].append(pop_quad(h, SCORE_SLOTS[2], 0, 1, nxt[2]))
                cs[h].append(pop_quad(h, SCORE_SLOTS[0], 1, 1, nxt[2]))
                out.append((ms[h], col_max(cs[h]), alphas[h]))
            else:
                out.append((ms[h], state[h][1], alphas[h]))
        return tuple(out)

    def run_tile(i, sub):
        for h in heads:
            acc_ref[h] = jnp.zeros((VALUE_ROWS, tq), F32)
        state = tuple((jnp.full((1, tq), -jnp.inf, F32), cm_ref[h], jnp.ones((1, tq), F32)) for h in heads)
        next_tile = jnp.minimum(i + 1, n_tiles - 1)
        state = lax.fori_loop(
            0, i,
            lambda t, st: step(order(i, t), st, (i, t, False), jnp.where(t == i - 1, next_tile, i)),
            state)

        def finish(nxt):
            last = step(order(i, i), state, nxt)
            outs = []
            for h in heads:
                if nxt is not None:
                    cm_ref[h] = last[h][1]
                accumulate(h, last[h][2])
                acc = acc_ref[h]
                outs.append(acc[0:V_DIM, :] / acc[V_DIM:V_DIM + 1, :])
            o_ref[0, sub * tq:(sub + 1) * tq, :] = jnp.concatenate(outs, axis=0).T

        if sub < TILES_PER_STEP - 1:
            finish((i + 1, i + 1, True))
        else:
            @pl.when(i < n_tiles - 1)
            def _():
                finish((i + 1, i + 1, True))

            @pl.when(i == n_tiles - 1)
            def _():
                finish(None)

    for sub in range(TILES_PER_STEP):
        run_tile(g * TILES_PER_STEP + sub, sub)


def _attention(qt, k, vt, tq):
    B, H, S, _ = k.shape
    hps = HEADS_PER_STEP
    assert tq == 2 * MXU_TILE and HEAD_SLAB <= MXU_TILE and S % (tq * TILES_PER_STEP) == 0
    return pl.pallas_call(
        functools.partial(_attn_kernel, tq=tq, n_tiles=S // tq),
        grid=(B, H // hps, S // (tq * TILES_PER_STEP)),
        in_specs=[pl.BlockSpec((1, hps, HEAD_SLAB, S), lambda b, p, i: (b, p, 0, 0)),
                  pl.BlockSpec((1, hps, S, HEAD_SLAB), lambda b, p, i: (b, p, 0, 0)),
                  pl.BlockSpec((1, hps * V_DIM, S), lambda b, p, i: (b, p, 0))],
        out_specs=pl.BlockSpec((1, tq * TILES_PER_STEP, hps * V_DIM), lambda b, p, i: (b, i, p)),
        out_shape=jax.ShapeDtypeStruct((B, S, ATTN_W), F32),
        scratch_shapes=[pltpu.VMEM((hps, 4, MXU_TILE, MXU_TILE), F32),
                        pltpu.VMEM((hps, VALUE_ROWS, tq), F32),
                        pltpu.VMEM((hps, 1, tq), F32)],
        compiler_params=_params(3),
        name="mla_attention",
    )(qt, k, vt)


def _layout_w_in(w_in):
    lat = Q_RANK + KV_RANK
    main = jnp.concatenate([w_in[:, :, :lat], w_in[:, :, lat + QK_ROPE:]], axis=-1)
    kpe_t = jnp.swapaxes(w_in[:, :, lat:lat + QK_ROPE], 1, 2)
    return main.astype(BF16), kpe_t.astype(BF16)


def _layout_w_uq(w_uq):
    L, R, _ = w_uq.shape
    w = w_uq.reshape(L, R, N_HEADS, QK_NOPE + QK_ROPE)
    z_pad = jnp.zeros((L, R, N_HEADS, HEAD_SLAB - QK_NOPE - QK_ROPE), w.dtype)
    q = jnp.concatenate([w, z_pad], axis=-1).reshape(L, R, N_HEADS * HEAD_SLAB)
    return jnp.swapaxes(q, 1, 2).astype(BF16)


def _layout_w_ukv(w_ukv):
    L, R, _ = w_ukv.shape
    w = w_ukv.reshape(L, R, N_HEADS, QK_NOPE + V_DIM)
    k_nope, v = w[..., :QK_NOPE], w[..., QK_NOPE:]
    z = jnp.zeros((L, R, N_HEADS, HEAD_SLAB - QK_NOPE), w.dtype)
    wk = jnp.concatenate([k_nope, z], axis=-1).reshape(L, R, N_HEADS * HEAD_SLAB)
    wv_t = jnp.swapaxes(v.reshape(L, R, ATTN_W), 1, 2)
    return wk.astype(BF16), wv_t.astype(BF16)


def kernel(x, c, positions, w_ada, b_ada, ffn1_norm, ffn1_w_gu, ffn1_w_down, mix_norm, w_in, q_a_norm, w_uq, kv_a_norm, w_ukv, conv_w, attn_out_norm, conv_out_norm, w_o, ffn2_norm, ffn2_w_gu, ffn2_w_down, final_norm):
    B, S, D = x.shape
    L = w_ada.shape[0]
    tm = min(S, 512)
    tq = min(S, 512)

    mod = _modulation(c, w_ada, b_ada)
    cos, sin = _rope_tables(positions)

    wgu1, wd1 = ffn1_w_gu.astype(BF16), ffn1_w_down.astype(BF16)
    wgu2, wd2 = ffn2_w_gu.astype(BF16), ffn2_w_down.astype(BF16)
    win, wkpe_t = _layout_w_in(w_in)
    wuq_t = _layout_w_uq(w_uq)
    wk, wv_t = _layout_w_ukv(w_ukv)
    wo = w_o.astype(BF16)
    final_g = final_norm.reshape(1, D)
    rows = lambda g: g.reshape(L, 1, g.shape[-1])
    g1, g2, gm = rows(ffn1_norm), rows(ffn2_norm), rows(mix_norm)
    gq, gkv = rows(q_a_norm), rows(kv_a_norm)
    ga, gc = rows(attn_out_norm), rows(conv_out_norm)

    for l in range(L):
        x, qt, k, vt, conv = _ffn_proj(x, mod, g1, wgu1, wd1, gm, win, wkpe_t, gq, wuq_t, gkv, wk, wv_t,
                                       conv_w, gc, cos, sin, l, tm)
        attn = _attention(qt, k, vt, tq)
        x = _merge_ffn(x, mod, attn, conv, ga, wo, g2, wgu2, wd2, l, tm,
                       final_g=final_g if l == L - 1 else None)
    return x
```

```python
import functools

import jax
import jax.numpy as jnp
from jax import lax
from jax.experimental import pallas as pl
from jax.experimental.pallas import tpu as pltpu

F32 = jnp.float32
BF16 = jnp.bfloat16

D_MODEL = 1024
N_HEADS = 8
QK_NOPE = 64
QK_ROPE = 32
V_DIM = 64
Q_RANK = 384
KV_RANK = 256
CONV_W = 512
ATTN_W = N_HEADS * V_DIM
D_FF = 2816
N_MOD = 9
EPS = 1e-6
ROPE_THETA = 10000.0
HEAD_SLAB = 128
HALF_ROPE = QK_ROPE // 2
P_CQ = 0
P_CKV = P_CQ + Q_RANK
P_GB = P_CKV + KV_RANK
P_WIDTH = P_GB + 3 * CONV_W
HALO = 8
Q_SCALE = float((QK_NOPE + QK_ROPE) ** -0.5 * 1.4426950408889634)
VMEM_LIMIT = 56 * 1024 * 1024
FFN_ROW_GROUPS = 2
HEADS_PER_STEP = 2
TILES_PER_STEP = 2
MXU_TILE = 256
VALUE_ROWS = V_DIM + 16
SCORE_SLOTS = (0, MXU_TILE // 4, 2 * MXU_TILE // 4)
OUT_SLOTS = (3 * MXU_TILE // 4, 3 * MXU_TILE // 4 + VALUE_ROWS // 4)
NT_DIMS = (((1,), (1,)), ((), ()))


def _rms(x, g):
    return x * lax.rsqrt(jnp.mean(x * x, axis=-1, keepdims=True) + EPS) * g


def _silu(x):
    return x / (1.0 + jnp.exp(-x))


def _params(n_axes):
    return pltpu.CompilerParams(dimension_semantics=("arbitrary",) * n_axes,
                                vmem_limit_bytes=VMEM_LIMIT)


def _resident(shape, index_map):
    return pl.BlockSpec(shape, index_map, pipeline_mode=pl.Buffered(1))


def _row_spec(width, layer):
    return pl.BlockSpec((None, 1, width), lambda *_: (layer, 0, 0))


def _mod_spec(layer):
    return pl.BlockSpec((None, 1, N_MOD, D_MODEL), lambda b, *_: (layer, b, 0, 0))


def _mod_kernel(c_ref, w_ref, b_ref, o_ref):
    c = c_ref[...]
    ca = _silu(c).astype(BF16)
    o_ref[0] = jnp.dot(ca, w_ref[0].astype(BF16), preferred_element_type=F32) + b_ref[0]


def _modulation(c, w_ada, b_ada):
    L, D, N = w_ada.shape
    B = c.shape[0]
    tn = N // 8
    out = pl.pallas_call(
        _mod_kernel,
        grid=(L, N // tn),
        in_specs=[pl.BlockSpec((B, D), lambda l, n: (0, 0)),
                  pl.BlockSpec((1, D, tn), lambda l, n: (l, 0, n)),
                  pl.BlockSpec((1, 1, tn), lambda l, n: (l, 0, n))],
        out_specs=pl.BlockSpec((1, B, tn), lambda l, n: (l, 0, n)),
        out_shape=jax.ShapeDtypeStruct((L, B, N), F32),
        compiler_params=_params(2),
        name="adaln_mod",
    )(c, w_ada, b_ada.reshape(L, 1, N))
    return out.reshape(L, B, N_MOD, D)


def _rope_kernel(pos_ref, invf_ref, sgn_ref, cos_ref, sin_ref):
    ang = invf_ref[...] * pos_ref[0].astype(F32)
    cos_ref[0] = jnp.cos(ang)
    sin_ref[0] = jnp.sin(ang) * sgn_ref[...]


def _rope_tables(positions):
    B, S = positions.shape
    ts = min(S, 2048)
    inv_freq = 1.0 / (ROPE_THETA ** (jnp.arange(0, QK_ROPE, 2, dtype=F32) / QK_ROPE))
    invf = jnp.concatenate([inv_freq, inv_freq])[:, None]
    ones = jnp.ones((HALF_ROPE,), F32)
    sgn = jnp.concatenate([-ones, ones])[:, None]
    tab = jax.ShapeDtypeStruct((B, QK_ROPE, S), F32)
    col = pl.BlockSpec((QK_ROPE, 1), lambda b, i: (0, 0))
    out = pl.BlockSpec((1, QK_ROPE, ts), lambda b, i: (b, 0, i))
    return pl.pallas_call(
        _rope_kernel,
        grid=(B, S // ts),
        in_specs=[pl.BlockSpec((1, 1, ts), lambda b, i: (b, 0, i)), col, col],
        out_specs=[out, out],
        out_shape=[tab, tab],
        compiler_params=_params(2),
        name="rope_tables",
    )(positions.reshape(B, 1, S), invf, sgn)


def _swiglu_rows(x, mod_ref, mod_base, g_ref, wgu_ref, wd_ref):
    shift = mod_ref[0, mod_base:mod_base + 1, :]
    scale = mod_ref[0, mod_base + 1:mod_base + 2, :]
    gate = mod_ref[0, mod_base + 2:mod_base + 3, :]
    h = (_rms(x, g_ref[...]) * (1.0 + scale) + shift).astype(BF16)
    gu = jnp.dot(h, wgu_ref[...], preferred_element_type=F32)
    a = (_silu(gu[:, :D_FF]) * gu[:, D_FF:]).astype(BF16)
    y = jnp.dot(a, wd_ref[...], preferred_element_type=F32)
    return x + (0.5 * (1.0 + gate)) * y


def _merge_ffn_kernel(x_ref, mod_ref, attn_ref, conv_ref, ag_ref, wo_ref, g_ref, wgu_ref, wd_ref, *rest, final):
    o_ref = rest[-1]
    an = _rms(attn_ref[0], ag_ref[...]).astype(BF16)
    ym = jnp.dot(an, wo_ref[0:ATTN_W, :], preferred_element_type=F32)
    ym = ym + jnp.dot(conv_ref[0], wo_ref[ATTN_W:, :], preferred_element_type=F32)
    x = x_ref[0] + (1.0 + mod_ref[0, 5:6, :]) * ym
    out = _swiglu_rows(x, mod_ref, 6, g_ref, wgu_ref, wd_ref)
    if final:
        out = _rms(out, rest[0][...])
    o_ref[0] = out


def _merge_ffn(x, mod_l, attn, conv, ag, wo, g, wgu, wd, layer, tm, final_g=None):
    B, S, D = x.shape
    tok = lambda w: pl.BlockSpec((1, tm, w), lambda b, i: (b, i, 0))
    wspec = lambda r, c: _resident((None, r, c), lambda b, i: (layer, 0, 0))
    in_specs = [tok(D), _mod_spec(layer), tok(ATTN_W), tok(CONV_W), _row_spec(ATTN_W, layer),
                wspec(ATTN_W + CONV_W, D), _row_spec(D, layer), wspec(D, 2 * D_FF), wspec(D_FF, D)]
    args = [x, mod_l, attn, conv, ag, wo, g, wgu, wd]
    if final_g is not None:
        in_specs.append(pl.BlockSpec((1, D), lambda b, i: (0, 0)))
        args.append(final_g)
    return pl.pallas_call(
        functools.partial(_merge_ffn_kernel, final=final_g is not None),
        grid=(B, S // tm),
        in_specs=in_specs,
        out_specs=tok(D),
        out_shape=jax.ShapeDtypeStruct((B, S, D), F32),
        compiler_params=_params(2),
        name="merge_ffn_final" if final_g is not None else "merge_ffn",
    )(*args)


def _rope_t(rows, cos, sin):
    partner = jnp.concatenate([rows[HALF_ROPE:], rows[:HALF_ROPE]], axis=0)
    return rows * cos + partner * sin


def _ffn_proj_kernel(x_ref, mod_ref, g1_ref, wgu_ref, wd_ref,
                     g_ref, win_ref, wkpe_ref, qg_ref, wuq_ref, kvg_ref, wk_ref, wv_ref,
                     cw_ref, cg_ref, cos_ref, sin_ref,
                     x1_ref, qt_ref, k_ref, vt_ref, conv_ref, hist_ref, *, tm):
    @pl.when(pl.program_id(1) == 0)
    def _():
        hist_ref[0:HALO, :] = jnp.zeros((HALO, CONV_W), F32)

    for r in range(FFN_ROW_GROUPS):
        rows = slice(r * tm // FFN_ROW_GROUPS, (r + 1) * tm // FFN_ROW_GROUPS)
        x1_ref[0, rows, :] = _swiglu_rows(x_ref[0, rows, :], mod_ref, 0, g1_ref, wgu_ref, wd_ref)

    x = x1_ref[0]
    shift = mod_ref[0, 3:4, :]
    scale = mod_ref[0, 4:5, :]
    h = (_rms(x, g_ref[...]) * (1.0 + scale) + shift).astype(BF16)
    proj = jnp.dot(h, win_ref[:, 0:P_GB], preferred_element_type=F32)
    cos = cos_ref[0]
    sin = sin_ref[0]
    rope_lo, rope_hi = QK_NOPE, QK_NOPE + QK_ROPE

    cqn = _rms(proj[:, P_CQ:P_CQ + Q_RANK], qg_ref[...]).astype(BF16)
    qt = lax.dot_general(wuq_ref[...], cqn, NT_DIMS, preferred_element_type=F32)
    gates = jnp.dot(h, win_ref[:, P_GB:P_WIDTH], preferred_element_type=F32)
    ckvn = _rms(proj[:, P_CKV:P_CKV + KV_RANK], kvg_ref[...]).astype(BF16)
    kk = jnp.dot(ckvn, wk_ref[...], preferred_element_type=F32)
    vt = lax.dot_general(wv_ref[...], ckvn, NT_DIMS, preferred_element_type=F32)
    kpe_t = lax.dot_general(wkpe_ref[...], h, NT_DIMS, preferred_element_type=F32)
    kpe_t = jnp.concatenate([jnp.zeros((rope_lo, tm), F32), _rope_t(kpe_t, cos, sin),
                             jnp.zeros((HEAD_SLAB - rope_hi, tm), F32)], axis=0)
    kpe = kpe_t.T
    for hd in range(N_HEADS):
        lo = hd * HEAD_SLAB
        qh = qt[lo:lo + HEAD_SLAB]
        qh = jnp.concatenate([qh[:rope_lo], _rope_t(qh[rope_lo:rope_hi], cos, sin), qh[rope_hi:]], axis=0)
        qh = (qh * Q_SCALE).astype(BF16)
        vh = vt[hd * V_DIM:(hd + 1) * V_DIM].astype(BF16)
        for j in range(tm // MXU_TILE):
            qt_ref[0, hd, j] = qh[:, j * MXU_TILE:(j + 1) * MXU_TILE]
            vt_ref[0, hd, j] = vh[:, j * MXU_TILE:(j + 1) * MXU_TILE]
        k_ref[0, hd] = (kk[:, lo:lo + HEAD_SLAB] + kpe).astype(BF16)

    cv =gates[:, CONV_W:2 * CONV_W] * gates[:, 2 * CONV_W:3 * CONV_W]
    hist_ref[HALO:HALO + tm, :] = cv
    cv1 = hist_ref[HALO - 1:HALO - 1 + tm, :]
    cv2 = hist_ref[HALO - 2:HALO - 2 + tm, :]
    conv = cw_ref[0:1, :] * cv2 + cw_ref[1:2, :] * cv1 + cw_ref[2:3, :] * cv
    conv = gates[:, 0:CONV_W] * conv
    conv_ref[0] = _rms(conv, cg_ref[...]).astype(BF16)
    hist_ref[0:HALO, :] = cv[tm - HALO:, :]


def _ffn_proj(x, mod_l, g1, wgu, wd, g, win, wkpe, qg, wuq, kvg, wk, wv, cw, cg, cos, sin, layer, tm):
    B, S, D = x.shape
    tok = lambda w: pl.BlockSpec((1, tm, w), lambda b, i: (b, i, 0))
    tok_t = lambda r: pl.BlockSpec((1, r, tm), lambda b, i: (b, 0, i))
    row = lambda w: _row_spec(w, layer)
    wspec = lambda r, c: _resident((None, r, c), lambda b, i: (layer, 0, 0))
    return pl.pallas_call(
        functools.partial(_ffn_proj_kernel, tm=tm),
        grid=(B, S // tm),
        in_specs=[tok(D),
                  _mod_spec(layer),
                  row(D), wspec(D, 2 * D_FF), wspec(D_FF, D),
                  row(D), wspec(D, P_WIDTH), wspec(QK_ROPE, D),
                  row(Q_RANK), wspec(N_HEADS * HEAD_SLAB, Q_RANK),
                  row(KV_RANK), wspec(KV_RANK, N_HEADS * HEAD_SLAB), wspec(ATTN_W, KV_RANK),
                  pl.BlockSpec((None, 3, CONV_W), lambda b, i: (layer, 0, 0)),
                  row(CONV_W), tok_t(QK_ROPE), tok_t(QK_ROPE)],
        out_specs=[tok(D),
                   pl.BlockSpec((1, N_HEADS, tm // MXU_TILE, HEAD_SLAB, MXU_TILE), lambda b, i: (b, 0, i, 0, 0)),
                   pl.BlockSpec((1, N_HEADS, tm, HEAD_SLAB), lambda b, i: (b, 0, i, 0)),
                   pl.BlockSpec((1, N_HEADS, tm // MXU_TILE, V_DIM, MXU_TILE), lambda b, i: (b, 0, i, 0, 0)),
                   tok(CONV_W)],
        out_shape=[jax.ShapeDtypeStruct((B, S, D), F32),
                   jax.ShapeDtypeStruct((B, N_HEADS, S // MXU_TILE, HEAD_SLAB, MXU_TILE), BF16),
                   jax.ShapeDtypeStruct((B, N_HEADS, S, HEAD_SLAB), BF16),
                   jax.ShapeDtypeStruct((B, N_HEADS, S // MXU_TILE, V_DIM, MXU_TILE), BF16),
                   jax.ShapeDtypeStruct((B, S, CONV_W), BF16)],
        scratch_shapes=[pltpu.VMEM((HALO + tm, CONV_W), F32)],
        compiler_params=_params(2),
        name="ffn_proj",
    )(x, mod_l, g1, wgu, wd, g, win, wkpe, qg, wuq, kvg, wk, wv, cw, cg, cos, sin)


def _attn_kernel(qt_ref, k_ref, vt_ref, o_ref, s_ref, acc_ref, cm_ref, *, tq, n_tiles):
    g = pl.program_id(2)
    hb = MXU_TILE
    krow = lax.broadcasted_iota(jnp.int32, (hb, hb), 0)
    qcol = lax.broadcasted_iota(jnp.int32, (hb, hb), 1)
    tri = krow <= qcol
    zpad = jnp.zeros((hb, hb - HEAD_SLAB), BF16)
    zrows = jnp.zeros((hb - HEAD_SLAB, hb), BF16)
    ones = jnp.ones((VALUE_ROWS - V_DIM, hb), BF16)
    heads = range(HEADS_PER_STEP)

    def q_half(h, tile, qh):
        return jnp.concatenate([qt_ref[0, h, tile * (tq // hb) + qh], zrows], axis=0)

    def k_half(h, blk, kh):
        start = pl.multiple_of(blk * tq + kh * hb, hb)
        return jnp.concatenate([k_ref[0, h, pl.ds(start, hb), :], zpad], axis=1)

    def v_half(h, blk, kh):
        return jnp.concatenate([vt_ref[0, h, blk * (tq // hb) + kh], ones], axis=0)

    def quad_ref(h, kh, qh):
        return s_ref.at[h, 2 * kh + qh]

    def prestage(h, tile):
        pltpu.matmul_push_rhs(q_half(h, tile, 0), 1, h)

    def drain_stage(h):
        pltpu.matmul_acc_lhs(SCORE_SLOTS[1], jnp.zeros((16, hb), BF16), h, 1)
        pltpu.matmul_pop(SCORE_SLOTS[1], (16, hb), F32, h)

    def issue3(h, tile, blk):
        pltpu.matmul_acc_lhs(SCORE_SLOTS[0], k_half(h, blk, 0), h, 1)
        pltpu.matmul_acc_lhs(SCORE_SLOTS[1], k_half(h, blk, 1), h, None)
        pltpu.matmul_push_rhs(q_half(h, tile, 1), 0, h)
        pltpu.matmul_acc_lhs(SCORE_SLOTS[2], k_half(h, blk, 0), h, 0)

    def issue4(h, blk):
        pltpu.matmul_acc_lhs(SCORE_SLOTS[0], k_half(h, blk, 1), h, None)

    def pop_quad(h, slot, kh, qh, diag):
        st = pltpu.matmul_pop(slot, (hb, hb), F32, h)
        if diag:
            if kh > qh:
                st = jnp.full((hb, hb), -jnp.inf, F32)
            elif kh == qh:
                st = jnp.where(tri, st, -jnp.inf)
        quad_ref(h, kh, qh)[...] = st
        return jnp.max(st, axis=0, keepdims=True)

    def numer(h, kh, qh, m_new, reg):
        mq = m_new[:, qh * hb:(qh + 1) * hb]
        p = jnp.exp2(quad_ref(h, kh, qh)[...] - mq).astype(BF16)
        pltpu.matmul_push_rhs(p, reg, h)

    def pv(h, blk, qh, between=None):
        pltpu.matmul_acc_lhs(OUT_SLOTS[qh], v_half(h, blk, 0), h, 1)
        if between is not None:
            between()
        pltpu.matmul_acc_lhs(OUT_SLOTS[qh], v_half(h, blk, 1), h, 0)

    def accumulate(h, alpha):
        out = jnp.concatenate([pltpu.matmul_pop(OUT_SLOTS[qh], (VALUE_ROWS, hb), F32, h) for qh in range(2)], axis=1)
        acc_ref[h] = alpha * acc_ref[h] + out

    def col_max(c):
        return jnp.concatenate([jnp.maximum(c[0], c[1]), jnp.maximum(c[2], c[3])], axis=1)

    def order(i, t):
        return jnp.where(t == 0, i, t - 1)

    @pl.when(g == 0)
    def _():
        for h in heads:
            drain_stage(h)
            prestage(h, 0)
            issue3(h, 0, 0)
        for h in heads:
            c = [pop_quad(h, SCORE_SLOTS[0], 0, 0, True)]
            issue4(h, 0)
            c.append(pop_quad(h, SCORE_SLOTS[1], 1, 0, True))
            c.append(pop_quad(h, SCORE_SLOTS[2], 0, 1, True))
            c.append(pop_quad(h, SCORE_SLOTS[0], 1, 1, True))
            cm_ref[h] = col_max(c)
            prestage(h, min(1, n_tiles - 1))

    def step(blk, state, nxt, staged_tile=None):
        ms, alphas, cs = [], [], [[] for _ in heads]
        for h in heads:
            m, cm, _ = state[h]
            m_new = jnp.maximum(m, cm)
            ms.append(m_new)
            alphas.append(jnp.exp2(m - m_new))
        for h in heads:
            if nxt is not None:
                issue3(h, nxt[0], nxt[1])
            else:
                drain_stage(h)
        for h in heads:
            numer(h, 0, 0, ms[h], 1)
        for h in heads:
            if nxt is not None:
                cs[h].append(pop_quad(h, SCORE_SLOTS[0], 0, 0, nxt[2]))
                issue4(h, nxt[1])
        for h in heads:
            numer(h, 1, 0, ms[h], 0)
        if nxt is not None:
            for h in heads:
                cs[h].append(pop_quad(h, SCORE_SLOTS[1], 1, 0, nxt[2]))
        for h in heads:
            accumulate(h, state[h][2])
            pv(h, blk, 0)
        for h in heads:
            numer(h, 0, 1, ms[h], 1)
            numer(h, 1, 1, ms[h], 0)
            pv(h, blk, 1, (lambda h=h: prestage(h, nxt[0] if staged_tile is None else staged_tile))
               if nxt is not None else None)
        out = []
        for h in heads:
            if nxt is not None:
                cs[h].append(pop_quad(h, SCORE_SLOTS[2], 0, 1, nxt[2]))
                cs[h].append(pop_quad(h, SCORE_SLOTS[0], 1, 1, nxt[2]))
                out.append((ms[h], col_max(cs[h]), alphas[h]))
            else:
                out.append((ms[h], state[h][1], alphas[h]))
        return tuple(out)

    def run_tile(i, sub):
        for h in heads:
            acc_ref[h] = jnp.zeros((VALUE_ROWS, tq), F32)
        state = tuple((jnp.full((1, tq), -jnp.inf, F32), cm_ref[h], jnp.ones((1, tq), F32)) for h in heads)
        next_tile = jnp.minimum(i + 1, n_tiles - 1)
        state = lax.fori_loop(
            0, i,
            lambda t, st: step(order(i, t), st, (i, t, False), jnp.where(t == i - 1, next_tile, i)),
            state)

        def finish(nxt):
            last = step(order(i, i), state, nxt)
            outs = []
            for h in heads:
                if nxt is not None:
                    cm_ref[h] = last[h][1]
                accumulate(h, last[h][2])
                acc = acc_ref[h]
                outs.append(acc[0:V_DIM, :] / acc[V_DIM:V_DIM + 1, :])
            o_ref[0, sub * tq:(sub + 1) * tq, :] = jnp.concatenate(outs, axis=0).T

        if sub < TILES_PER_STEP - 1:
            finish((i + 1, i + 1, True))
        else:
            @pl.when(i < n_tiles - 1)
            def _():
                finish((i + 1, i + 1, True))

            @pl.when(i == n_tiles - 1)
            def _():
                finish(None)

    for sub in range(TILES_PER_STEP):
        run_tile(g * TILES_PER_STEP + sub, sub)


def _attention(qt, k, vt, tq):
    B, H, S, _ = k.shape
    hps = HEADS_PER_STEP
    assert tq == 2 * MXU_TILE and HEAD_SLAB <= MXU_TILE and S % (tq * TILES_PER_STEP) == 0
    return pl.pallas_call(
        functools.partial(_attn_kernel, tq=tq, n_tiles=S // tq),
        grid=(B, H // hps, S // (tq * TILES_PER_STEP)),
        in_specs=[pl.BlockSpec((1, hps, S // MXU_TILE, HEAD_SLAB, MXU_TILE), lambda b, p, i: (b, p, 0, 0, 0)),
                  pl.BlockSpec((1, hps, S, HEAD_SLAB), lambda b, p, i: (b, p, 0, 0)),
                  pl.BlockSpec((1, hps, S // MXU_TILE, V_DIM, MXU_TILE), lambda b, p, i: (b, p, 0, 0, 0))],
        out_specs=pl.BlockSpec((1, tq * TILES_PER_STEP, hps * V_DIM), lambda b, p, i: (b, i, p)),
        out_shape=jax.ShapeDtypeStruct((B, S, ATTN_W), F32),
        scratch_shapes=[pltpu.VMEM((hps, 4, MXU_TILE, MXU_TILE), F32),
                        pltpu.VMEM((hps, VALUE_ROWS, tq), F32),
                        pltpu.VMEM((hps, 1, tq), F32)],
        compiler_params=_params(3),
        name="mla_attention",
    )(qt, k, vt)


def _layout_w_in(w_in):
    lat = Q_RANK + KV_RANK
    main = jnp.concatenate([w_in[:, :, :lat], w_in[:, :, lat + QK_ROPE:]], axis=-1)
    kpe_t = jnp.swapaxes(w_in[:, :, lat:lat + QK_ROPE], 1, 2)
    return main.astype(BF16), kpe_t.astype(BF16)


def _layout_w_uq(w_uq):
    L, R, _ = w_uq.shape
    w = w_uq.reshape(L, R, N_HEADS, QK_NOPE + QK_ROPE)
    z_pad = jnp.zeros((L, R, N_HEADS, HEAD_SLAB - QK_NOPE - QK_ROPE), w.dtype)
    q = jnp.concatenate([w, z_pad], axis=-1).reshape(L, R, N_HEADS * HEAD_SLAB)
    return jnp.swapaxes(q, 1, 2).astype(BF16)


def _layout_w_ukv(w_ukv):
    L, R, _ = w_ukv.shape
    w = w_ukv.reshape(L, R, N_HEADS, QK_NOPE + V_DIM)
    k_nope, v = w[..., :QK_NOPE], w[..., QK_NOPE:]
    z = jnp.zeros((L, R, N_HEADS, HEAD_SLAB - QK_NOPE), w.dtype)
    wk = jnp.concatenate([k_nope, z], axis=-1).reshape(L, R, N_HEADS * HEAD_SLAB)
    wv_t = jnp.swapaxes(v.reshape(L, R, ATTN_W), 1, 2)
    return wk.astype(BF16), wv_t.astype(BF16)


def kernel(x, c, positions, w_ada, b_ada, ffn1_norm, ffn1_w_gu, ffn1_w_down, mix_norm, w_in, q_a_norm, w_uq, kv_a_norm, w_ukv, conv_w, attn_out_norm, conv_out_norm, w_o, ffn2_norm, ffn2_w_gu, ffn2_w_down, final_norm):
    B, S, D = x.shape
    L = w_ada.shape[0]
    tm = min(S, 512)
    tq = min(S, 512)

    mod = _modulation(c, w_ada, b_ada)
    cos, sin = _rope_tables(positions)

    wgu1, wd1 = ffn1_w_gu.astype(BF16), ffn1_w_down.astype(BF16)
    wgu2, wd2 = ffn2_w_gu.astype(BF16), ffn2_w_down.astype(BF16)
    win, wkpe_t = _layout_w_in(w_in)
    wuq_t = _layout_w_uq(w_uq)
    wk, wv_t = _layout_w_ukv(w_ukv)
    wo = w_o.astype(BF16)
    final_g = final_norm.reshape(1, D)
    rows = lambda g: g.reshape(L, 1, g.shape[-1])
    g1, g2, gm = rows(ffn1_norm), rows(ffn2_norm), rows(mix_norm)
    gq, gkv = rows(q_a_norm), rows(kv_a_norm)
    ga, gc = rows(attn_out_norm), rows(conv_out_norm)

    for l in range(L):
        x, qt, k, vt, conv = _ffn_proj(x, mod, g1, wgu1, wd1, gm, win, wkpe_t, gq, wuq_t, gkv, wk, wv_t,
                                       conv_w, gc, cos, sin, l, tm)
        attn = _attention(qt, k, vt, tq)
        x = _merge_ffn(x, mod, attn, conv, ga, wo, g2, wgu2, wd2, l, tm,
                       final_g=final_g if l == L - 1 else None)
    return x
```

```python
import functools

import jax
import jax.numpy as jnp
from jax import lax
from jax.experimental import pallas as pl
from jax.experimental.pallas import tpu as pltpu

F32 = jnp.float32
BF16 = jnp.bfloat16

D_MODEL = 1024
N_HEADS = 8
QK_NOPE = 64
QK_ROPE = 32
V_DIM = 64
Q_RANK = 384
KV_RANK = 256
CONV_W = 512
ATTN_W = N_HEADS * V_DIM
D_FF = 2816
N_MOD = 9
EPS = 1e-6
ROPE_THETA = 10000.0
HEAD_SLAB = 128
HALF_ROPE = QK_ROPE // 2
P_CQ = 0
P_CKV = P_CQ + Q_RANK
P_GB = P_CKV + KV_RANK
P_WIDTH = P_GB + 3 * CONV_W
HALO = 8
Q_SCALE = float((QK_NOPE + QK_ROPE) ** -0.5 * 1.4426950408889634)
VMEM_LIMIT = 56 * 1024 * 1024
FFN_ROW_GROUPS = 2
HEADS_PER_STEP = 2
TILES_PER_STEP = 4
MXU_TILE = 256
VALUE_ROWS = V_DIM + 16
SCORE_SLOTS = (0, MXU_TILE // 4, 2 * MXU_TILE // 4)
OUT_SLOTS = (3 * MXU_TILE // 4, 3 * MXU_TILE // 4 + VALUE_ROWS // 4)
NT_DIMS = (((1,), (1,)), ((), ()))


def _rms(x, g):
    return x * lax.rsqrt(jnp.mean(x * x, axis=-1, keepdims=True) + EPS) * g


def _silu(x):
    return x / (1.0 + jnp.exp(-x))


def _params(n_axes):
    return pltpu.CompilerParams(dimension_semantics=("arbitrary",) * n_axes,
                                vmem_limit_bytes=VMEM_LIMIT)


def _resident(shape, index_map):
    return pl.BlockSpec(shape, index_map, pipeline_mode=pl.Buffered(1))


def _row_spec(width, layer):
    return pl.BlockSpec((None, 1, width), lambda *_: (layer, 0, 0))


def _mod_spec(layer):
    return pl.BlockSpec((None, 1, N_MOD, D_MODEL), lambda b, *_: (layer, b, 0, 0))


def _mod_kernel(c_ref, w_ref, b_ref, o_ref):
    c = c_ref[...]
    ca = _silu(c).astype(BF16)
    o_ref[0] = jnp.dot(ca, w_ref[0].astype(BF16), preferred_element_type=F32) + b_ref[0]


def _modulation(c, w_ada, b_ada):
    L, D, N = w_ada.shape
    B = c.shape[0]
    tn = N // 8
    out = pl.pallas_call(
        _mod_kernel,
        grid=(L, N // tn),
        in_specs=[pl.BlockSpec((B, D), lambda l, n: (0, 0)),
                  pl.BlockSpec((1, D, tn), lambda l, n: (l, 0, n)),
                  pl.BlockSpec((1, 1, tn), lambda l, n: (l, 0, n))],
        out_specs=pl.BlockSpec((1, B, tn), lambda l, n: (l, 0, n)),
        out_shape=jax.ShapeDtypeStruct((L, B, N), F32),
        compiler_params=_params(2),
        name="adaln_mod",
    )(c, w_ada, b_ada.reshape(L, 1, N))
    return out.reshape(L, B, N_MOD, D)


def _rope_kernel(pos_ref, invf_ref, sgn_ref, cos_ref, sin_ref):
    ang = invf_ref[...] * pos_ref[0].astype(F32)
    cos_ref[0] = jnp.cos(ang)
    sin_ref[0] = jnp.sin(ang) * sgn_ref[...]


def _rope_tables(positions):
    B, S = positions.shape
    ts = min(S, 2048)
    inv_freq = 1.0 / (ROPE_THETA ** (jnp.arange(0, QK_ROPE, 2, dtype=F32) / QK_ROPE))
    invf = jnp.concatenate([inv_freq, inv_freq])[:, None]
    ones = jnp.ones((HALF_ROPE,), F32)
    sgn = jnp.concatenate([-ones, ones])[:, None]
    tab = jax.ShapeDtypeStruct((B, QK_ROPE, S), F32)
    col = pl.BlockSpec((QK_ROPE, 1), lambda b, i: (0, 0))
    out = pl.BlockSpec((1, QK_ROPE, ts), lambda b, i: (b, 0, i))
    return pl.pallas_call(
        _rope_kernel,
        grid=(B, S // ts),
        in_specs=[pl.BlockSpec((1, 1, ts), lambda b, i: (b, 0, i)), col, col],
        out_specs=[out, out],
        out_shape=[tab, tab],
        compiler_params=_params(2),
        name="rope_tables",
    )(positions.reshape(B, 1, S), invf, sgn)


def _swiglu_rows(x, mod_ref, mod_base, g_ref, wgu_ref, wd_ref):
    shift = mod_ref[0, mod_base:mod_base + 1, :]
    scale = mod_ref[0, mod_base + 1:mod_base + 2, :]
    gate = mod_ref[0, mod_base + 2:mod_base + 3, :]
    h = (_rms(x, g_ref[...]) * (1.0 + scale) + shift).astype(BF16)
    gu = jnp.dot(h, wgu_ref[...], preferred_element_type=F32)
    a = (_silu(gu[:, :D_FF]) * gu[:, D_FF:]).astype(BF16)
    y = jnp.dot(a, wd_ref[...], preferred_element_type=F32)
    return x + (0.5 * (1.0 + gate)) * y


def _merge_ffn_kernel(x_ref, mod_ref, attn_ref, conv_ref, ag_ref, wo_ref, g_ref, wgu_ref, wd_ref, *rest, final):
    o_ref = rest[-1]
    an = _rms(attn_ref[0], ag_ref[...]).astype(BF16)
    ym = jnp.dot(an, wo_ref[0:ATTN_W, :], preferred_element_type=F32)
    ym = ym + jnp.dot(conv_ref[0], wo_ref[ATTN_W:, :], preferred_element_type=F32)
    x = x_ref[0] + (1.0 + mod_ref[0, 5:6, :]) * ym
    out = _swiglu_rows(x, mod_ref, 6, g_ref, wgu_ref, wd_ref)
    if final:
        out = _rms(out, rest[0][...])
    o_ref[0] = out


def _merge_ffn(x, mod_l, attn, conv, ag, wo, g, wgu, wd, layer, tm, final_g=None):
    B, S, D = x.shape
    tok = lambda w: pl.BlockSpec((1, tm, w), lambda b, i: (b, i, 0))
    wspec = lambda r, c: _resident((None, r, c), lambda b, i: (layer, 0, 0))
    in_specs = [tok(D), _mod_spec(layer), tok(ATTN_W), tok(CONV_W), _row_spec(ATTN_W, layer),
                wspec(ATTN_W + CONV_W, D), _row_spec(D, layer), wspec(D, 2 * D_FF), wspec(D_FF, D)]
    args = [x, mod_l, attn, conv, ag, wo, g, wgu, wd]
    if final_g is not None:
        in_specs.append(pl.BlockSpec((1, D), lambda b, i: (0, 0)))
        args.append(final_g)
    return pl.pallas_call(
        functools.partial(_merge_ffn_kernel, final=final_g is not None),
        grid=(B, S // tm),
        in_specs=in_specs,
        out_specs=tok(D),
        out_shape=jax.ShapeDtypeStruct((B, S, D), F32),
        compiler_params=_params(2),
        name="merge_ffn_final" if final_g is not None else "merge_ffn",
    )(*args)


def _rope_t(rows, cos, sin):
    partner = jnp.concatenate([rows[HALF_ROPE:], rows[:HALF_ROPE]], axis=0)
    return rows * cos + partner * sin


def _ffn_proj_kernel(x_ref, mod_ref, g1_ref, wgu_ref, wd_ref,
                     g_ref, win_ref, wkpe_ref, qg_ref, wuq_ref, kvg_ref, wk_ref, wv_ref,
                     cw_ref, cg_ref, cos_ref, sin_ref,
                     x1_ref, qt_ref, k_ref, vt_ref, conv_ref, hist_ref, *, tm):
    @pl.when(pl.program_id(1) == 0)
    def _():
        hist_ref[0:HALO, :] = jnp.zeros((HALO, CONV_W), F32)

    for r in range(FFN_ROW_GROUPS):
        rows = slice(r * tm // FFN_ROW_GROUPS, (r + 1) * tm // FFN_ROW_GROUPS)
        x1_ref[0, rows, :] = _swiglu_rows(x_ref[0, rows, :], mod_ref, 0, g1_ref, wgu_ref, wd_ref)

    x = x1_ref[0]
    shift = mod_ref[0, 3:4, :]
    scale = mod_ref[0, 4:5, :]
    h = (_rms(x, g_ref[...]) * (1.0 + scale) + shift).astype(BF16)
    proj = jnp.dot(h, win_ref[:, 0:P_GB], preferred_element_type=F32)
    cos = cos_ref[0]
    sin = sin_ref[0]
    rope_lo, rope_hi = QK_NOPE, QK_NOPE + QK_ROPE

    cqn = _rms(proj[:, P_CQ:P_CQ + Q_RANK], qg_ref[...]).astype(BF16)
    qt = lax.dot_general(wuq_ref[...], cqn, NT_DIMS, preferred_element_type=F32)
    gates = jnp.dot(h, win_ref[:, P_GB:P_WIDTH], preferred_element_type=F32)
    ckvn = _rms(proj[:, P_CKV:P_CKV + KV_RANK], kvg_ref[...]).astype(BF16)
    kk = jnp.dot(ckvn, wk_ref[...], preferred_element_type=F32)
    vt = lax.dot_general(wv_ref[...], ckvn, NT_DIMS, preferred_element_type=F32)
    kpe_t = lax.dot_general(wkpe_ref[...], h, NT_DIMS, preferred_element_type=F32)
    kpe_t = jnp.concatenate([jnp.zeros((rope_lo, tm), F32), _rope_t(kpe_t, cos, sin),
                             jnp.zeros((HEAD_SLAB - rope_hi, tm), F32)], axis=0)
    kpe = kpe_t.T
    for hd in range(N_HEADS):
        lo = hd * HEAD_SLAB
        qh = qt[lo:lo + HEAD_SLAB]
        qh = jnp.concatenate([qh[:rope_lo], _rope_t(qh[rope_lo:rope_hi], cos, sin), qh[rope_hi:]], axis=0)
        qh = (qh * Q_SCALE).astype(BF16)
        vh = vt[hd * V_DIM:(hd + 1) * V_DIM].astype(BF16)
        for j in range(tm // MXU_TILE):
            qt_ref[0, hd, j] = qh[:, j * MXU_TILE:(j + 1) * MXU_TILE]
            vt_ref[0, hd, j] = vh[:, j * MXU_TILE:(j + 1) * MXU_TILE]
        k_ref[0, hd] = (kk[:, lo:lo + HEAD_SLAB] + kpe).astype(BF16)

    cv =gates[:, CONV_W:2 * CONV_W] * gates[:, 2 * CONV_W:3 * CONV_W]
    hist_ref[HALO:HALO + tm, :] = cv
    cv1 = hist_ref[HALO - 1:HALO - 1 + tm, :]
    cv2 = hist_ref[HALO - 2:HALO - 2 + tm, :]
    conv = cw_ref[0:1, :] * cv2 + cw_ref[1:2, :] * cv1 + cw_ref[2:3, :] * cv
    conv = gates[:, 0:CONV_W] * conv
    conv_ref[0] = _rms(conv, cg_ref[...]).astype(BF16)
    hist_ref[0:HALO, :] = cv[tm - HALO:, :]


def _ffn_proj(x, mod_l, g1, wgu, wd, g, win, wkpe, qg, wuq, kvg, wk, wv, cw, cg, cos, sin, layer, tm):
    B, S, D = x.shape
    tok = lambda w: pl.BlockSpec((1, tm, w), lambda b, i: (b, i, 0))
    tok_t = lambda r: pl.BlockSpec((1, r, tm), lambda b, i: (b, 0, i))
    row = lambda w: _row_spec(w, layer)
    wspec = lambda r, c: _resident((None, r, c), lambda b, i: (layer, 0, 0))
    return pl.pallas_call(
        functools.partial(_ffn_proj_kernel, tm=tm),
        grid=(B, S // tm),
        in_specs=[tok(D),
                  _mod_spec(layer),
                  row(D), wspec(D, 2 * D_FF), wspec(D_FF, D),
                  row(D), wspec(D, P_WIDTH), wspec(QK_ROPE, D),
                  row(Q_RANK), wspec(N_HEADS * HEAD_SLAB, Q_RANK),
                  row(KV_RANK), wspec(KV_RANK, N_HEADS * HEAD_SLAB), wspec(ATTN_W, KV_RANK),
                  pl.BlockSpec((None, 3, CONV_W), lambda b, i: (layer, 0, 0)),
                  row(CONV_W), tok_t(QK_ROPE), tok_t(QK_ROPE)],
        out_specs=[tok(D),
                   pl.BlockSpec((1, N_HEADS, tm // MXU_TILE, HEAD_SLAB, MXU_TILE), lambda b, i: (b, 0, i, 0, 0)),
                   pl.BlockSpec((1, N_HEADS, tm, HEAD_SLAB), lambda b, i: (b, 0, i, 0)),
                   pl.BlockSpec((1, N_HEADS, tm // MXU_TILE, V_DIM, MXU_TILE), lambda b, i: (b, 0, i, 0, 0)),
                   tok(CONV_W)],
        out_shape=[jax.ShapeDtypeStruct((B, S, D), F32),
                   jax.ShapeDtypeStruct((B, N_HEADS, S // MXU_TILE, HEAD_SLAB, MXU_TILE), BF16),
                   jax.ShapeDtypeStruct((B, N_HEADS, S, HEAD_SLAB), BF16),
                   jax.ShapeDtypeStruct((B, N_HEADS, S // MXU_TILE, V_DIM, MXU_TILE), BF16),
                   jax.ShapeDtypeStruct((B, S, CONV_W), BF16)],
        scratch_shapes=[pltpu.VMEM((HALO + tm, CONV_W), F32)],
        compiler_params=_params(2),
        name="ffn_proj",
    )(x, mod_l, g1, wgu, wd, g, win, wkpe, qg, wuq, kvg, wk, wv, cw, cg, cos, sin)


def _attn_kernel(qt_ref, k_ref, vt_ref, o_ref, s_ref, acc_ref, cm_ref, *, tq, n_tiles):
    g = pl.program_id(2)
    hb = MXU_TILE
    krow = lax.broadcasted_iota(jnp.int32, (hb, hb), 0)
    qcol = lax.broadcasted_iota(jnp.int32, (hb, hb), 1)
    tri = krow <= qcol
    zpad = jnp.zeros((hb, hb - HEAD_SLAB), BF16)
    zrows = jnp.zeros((hb - HEAD_SLAB, hb), BF16)
    ones = jnp.ones((VALUE_ROWS - V_DIM, hb), BF16)
    heads = range(HEADS_PER_STEP)

    def q_half(h, tile, qh):
        return jnp.concatenate([qt_ref[0, h, tile * (tq // hb) + qh], zrows], axis=0)

    def k_half(h, blk, kh):
        start = pl.multiple_of(blk * tq + kh * hb, hb)
        return jnp.concatenate([k_ref[0, h, pl.ds(start, hb), :], zpad], axis=1)

    def v_half(h, blk, kh):
        return jnp.concatenate([vt_ref[0, h, blk * (tq // hb) + kh], ones], axis=0)

    def quad_ref(h, kh, qh):
        return s_ref.at[h, 2 * kh + qh]

    def prestage(h, tile):
        pltpu.matmul_push_rhs(q_half(h, tile, 0), 1, h)

    def drain_stage(h):
        pltpu.matmul_acc_lhs(SCORE_SLOTS[1], jnp.zeros((16, hb), BF16), h, 1)
        pltpu.matmul_pop(SCORE_SLOTS[1], (16, hb), F32, h)

    def issue3(h, tile, blk):
        pltpu.matmul_acc_lhs(SCORE_SLOTS[0], k_half(h, blk, 0), h, 1)
        pltpu.matmul_acc_lhs(SCORE_SLOTS[1], k_half(h, blk, 1), h, None)
        pltpu.matmul_push_rhs(q_half(h, tile, 1), 0, h)
        pltpu.matmul_acc_lhs(SCORE_SLOTS[2], k_half(h, blk, 0), h, 0)

    def issue4(h, blk):
        pltpu.matmul_acc_lhs(SCORE_SLOTS[0], k_half(h, blk, 1), h, None)

    def pop_quad(h, slot, kh, qh, diag):
        st = pltpu.matmul_pop(slot, (hb, hb), F32, h)
        if diag:
            if kh > qh:
                st = jnp.full((hb, hb), -jnp.inf, F32)
            elif kh == qh:
                st = jnp.where(tri, st, -jnp.inf)
        quad_ref(h, kh, qh)[...] = st
        return jnp.max(st, axis=0, keepdims=True)

    def numer(h, kh, qh, m_new, reg):
        mq = m_new[:, qh * hb:(qh + 1) * hb]
        p = jnp.exp2(quad_ref(h, kh, qh)[...] - mq).astype(BF16)
        pltpu.matmul_push_rhs(p, reg, h)

    def pv(h, blk, qh, between=None):
        pltpu.matmul_acc_lhs(OUT_SLOTS[qh], v_half(h, blk, 0), h, 1)
        if between is not None:
            between()
        pltpu.matmul_acc_lhs(OUT_SLOTS[qh], v_half(h, blk, 1), h, 0)

    def accumulate(h, alpha):
        out = jnp.concatenate([pltpu.matmul_pop(OUT_SLOTS[qh], (VALUE_ROWS, hb), F32, h) for qh in range(2)], axis=1)
        acc_ref[h] = alpha * acc_ref[h] + out

    def col_max(c):
        return jnp.concatenate([jnp.maximum(c[0], c[1]), jnp.maximum(c[2], c[3])], axis=1)

    def order(i, t):
        return jnp.where(t == 0, i, t - 1)

    @pl.when(g == 0)
    def _():
        for h in heads:
            drain_stage(h)
            prestage(h, 0)
            issue3(h, 0, 0)
        for h in heads:
            c = [pop_quad(h, SCORE_SLOTS[0], 0, 0, True)]
            issue4(h, 0)
            c.append(pop_quad(h, SCORE_SLOTS[1], 1, 0, True))
            c.append(pop_quad(h, SCORE_SLOTS[2], 0, 1, True))
            c.append(pop_quad(h, SCORE_SLOTS[0], 1, 1, True))
            cm_ref[h] = col_max(c)
            prestage(h, min(1, n_tiles - 1))

    def step(blk, state, nxt, staged_tile=None):
        ms, alphas, cs = [], [], [[] for _ in heads]
        for h in heads:
            m, cm, _ = state[h]
            m_new = jnp.maximum(m, cm)
            ms.append(m_new)
            alphas.append(jnp.exp2(m - m_new))
        for h in heads:
            if nxt is not None:
                issue3(h, nxt[0], nxt[1])
            else:
                drain_stage(h)
        for h in heads:
            numer(h, 0, 0, ms[h], 1)
        for h in heads:
            if nxt is not None:
                cs[h].append(pop_quad(h, SCORE_SLOTS[0], 0, 0, nxt[2]))
                issue4(h, nxt[1])
        for h in heads:
            numer(h, 1, 0, ms[h], 0)
        if nxt is not None:
            for h in heads:
                cs[h].append(pop_quad(h, SCORE_SLOTS[1], 1, 0, nxt[2]))
        for h in heads:
            accumulate(h, state[h][2])
            pv(h, blk, 0)
        for h in heads:
            numer(h, 0, 1, ms[h], 1)
            numer(h, 1, 1, ms[h], 0)
            pv(h, blk, 1, (lambda h=h: prestage(h, nxt[0] if staged_tile is None else staged_tile))
               if nxt is not None else None)
        out = []
        for h in heads:
            if nxt is not None:
                cs[h].append(pop_quad(h, SCORE_SLOTS[2], 0, 1, nxt[2]))
                cs[h].append(pop_quad(h, SCORE_SLOTS[0], 1, 1, nxt[2]))
                out.append((ms[h], col_max(cs[h]), alphas[h]))
            else:
                out.append((ms[h], state[h][1], alphas[h]))
        return tuple(out)

    def run_tile(i, sub):
        for h in heads:
            acc_ref[h] = jnp.zeros((VALUE_ROWS, tq), F32)
        state = tuple((jnp.full((1, tq), -jnp.inf, F32), cm_ref[h], jnp.ones((1, tq), F32)) for h in heads)
        next_tile = jnp.minimum(i + 1, n_tiles - 1)
        state = lax.fori_loop(
            0, i,
            lambda t, st: step(order(i, t), st, (i, t, False), jnp.where(t == i - 1, next_tile, i)),
            state)

        def finish(nxt):
            last = step(order(i, i), state, nxt)
            outs = []
            for h in heads:
                if nxt is not None:
                    cm_ref[h] = last[h][1]
                accumulate(h, last[h][2])
                acc = acc_ref[h]
                outs.append(acc[0:V_DIM, :] / acc[V_DIM:V_DIM + 1, :])
            o_ref[0, sub * tq:(sub + 1) * tq, :] = jnp.concatenate(outs, axis=0).T

        if sub < TILES_PER_STEP - 1:
            finish((i + 1, i + 1, True))
        else:
            @pl.when(i < n_tiles - 1)
            def _():
                finish((i + 1, i + 1, True))

            @pl.when(i == n_tiles - 1)
            def _():
                finish(None)

    for sub in range(TILES_PER_STEP):
        run_tile(g * TILES_PER_STEP + sub, sub)


def _attention(qt, k, vt, tq):
    B, H, S, _ = k.shape
    hps = HEADS_PER_STEP
    assert tq == 2 * MXU_TILE and HEAD_SLAB <= MXU_TILE and S % (tq * TILES_PER_STEP) == 0
    return pl.pallas_call(
        functools.partial(_attn_kernel, tq=tq, n_tiles=S // tq),
        grid=(B, H // hps, S // (tq * TILES_PER_STEP)),
        in_specs=[pl.BlockSpec((1, hps, S // MXU_TILE, HEAD_SLAB, MXU_TILE), lambda b, p, i: (b, p, 0, 0, 0)),
                  pl.BlockSpec((1, hps, S, HEAD_SLAB), lambda b, p, i: (b, p, 0, 0)),
                  pl.BlockSpec((1, hps, S // MXU_TILE, V_DIM, MXU_TILE), lambda b, p, i: (b, p, 0, 0, 0))],
        out_specs=pl.BlockSpec((1, tq * TILES_PER_STEP, hps * V_DIM), lambda b, p, i: (b, i, p)),
        out_shape=jax.ShapeDtypeStruct((B, S, ATTN_W), F32),
        scratch_shapes=[pltpu.VMEM((hps, 4, MXU_TILE, MXU_TILE), F32),
                        pltpu.VMEM((hps, VALUE_ROWS, tq), F32),
                        pltpu.VMEM((hps, 1, tq), F32)],
        compiler_params=_params(3),
        name="mla_attention",
    )(qt, k, vt)


def _layout_w_in(w_in):
    lat = Q_RANK + KV_RANK
    main = jnp.concatenate([w_in[:, :, :lat], w_in[:, :, lat + QK_ROPE:]], axis=-1)
    kpe_t = jnp.swapaxes(w_in[:, :, lat:lat + QK_ROPE], 1, 2)
    return main.astype(BF16), kpe_t.astype(BF16)


def _layout_w_uq(w_uq):
    L, R, _ = w_uq.shape
    w = w_uq.reshape(L, R, N_HEADS, QK_NOPE + QK_ROPE)
    z_pad = jnp.zeros((L, R, N_HEADS, HEAD_SLAB - QK_NOPE - QK_ROPE), w.dtype)
    q = jnp.concatenate([w, z_pad], axis=-1).reshape(L, R, N_HEADS * HEAD_SLAB)
    return jnp.swapaxes(q, 1, 2).astype(BF16)


def _layout_w_ukv(w_ukv):
    L, R, _ = w_ukv.shape
    w = w_ukv.reshape(L, R, N_HEADS, QK_NOPE + V_DIM)
    k_nope, v = w[..., :QK_NOPE], w[..., QK_NOPE:]
    z = jnp.zeros((L, R, N_HEADS, HEAD_SLAB - QK_NOPE), w.dtype)
    wk = jnp.concatenate([k_nope, z], axis=-1).reshape(L, R, N_HEADS * HEAD_SLAB)
    wv_t = jnp.swapaxes(v.reshape(L, R, ATTN_W), 1, 2)
    return wk.astype(BF16), wv_t.astype(BF16)


def kernel(x, c, positions, w_ada, b_ada, ffn1_norm, ffn1_w_gu, ffn1_w_down, mix_norm, w_in, q_a_norm, w_uq, kv_a_norm, w_ukv, conv_w, attn_out_norm, conv_out_norm, w_o, ffn2_norm, ffn2_w_gu, ffn2_w_down, final_norm):
    B, S, D = x.shape
    L = w_ada.shape[0]
    tm = min(S, 512)
    tq = min(S, 512)

    mod = _modulation(c, w_ada, b_ada)
    cos, sin = _rope_tables(positions)

    wgu1, wd1 = ffn1_w_gu.astype(BF16), ffn1_w_down.astype(BF16)
    wgu2, wd2 = ffn2_w_gu.astype(BF16), ffn2_w_down.astype(BF16)
    win, wkpe_t = _layout_w_in(w_in)
    wuq_t = _layout_w_uq(w_uq)
    wk, wv_t = _layout_w_ukv(w_ukv)
    wo = w_o.astype(BF16)
    final_g = final_norm.reshape(1, D)
    rows = lambda g: g.reshape(L, 1, g.shape[-1])
    g1, g2, gm = rows(ffn1_norm), rows(ffn2_norm), rows(mix_norm)
    gq, gkv = rows(q_a_norm), rows(kv_a_norm)
    ga, gc = rows(attn_out_norm), rows(conv_out_norm)

    for l in range(L):
        x, qt, k, vt, conv = _ffn_proj(x, mod, g1, wgu1, wd1, gm, win, wkpe_t, gq, wuq_t, gkv, wk, wv_t,
                                       conv_w, gc, cos, sin, l, tm)
        attn = _attention(qt, k, vt, tq)
        x = _merge_ffn(x, mod, attn, conv, ga, wo, g2, wgu2, wd2, l, tm,
                       final_g=final_g if l == L - 1 else None)
    return x
```

```python
import functools

import jax
import jax.numpy as jnp
from jax import lax
from jax.experimental import pallas as pl
from jax.experimental.pallas import tpu as pltpu

F32 = jnp.float32
BF16 = jnp.bfloat16

D_MODEL = 1024
N_HEADS = 8
QK_NOPE = 64
QK_ROPE = 32
V_DIM = 64
Q_RANK = 384
KV_RANK = 256
CONV_W = 512
ATTN_W = N_HEADS * V_DIM
D_FF = 2816
N_MOD = 9
EPS = 1e-6
ROPE_THETA = 10000.0
HEAD_SLAB = 128
HALF_ROPE = QK_ROPE // 2
P_CQ = 0
P_CKV = P_CQ + Q_RANK
P_GB = P_CKV + KV_RANK
P_WIDTH = P_GB + 3 * CONV_W
HALO = 8
Q_SCALE = float((QK_NOPE + QK_ROPE) ** -0.5 * 1.4426950408889634)
VMEM_LIMIT = 56 * 1024 * 1024
FFN_ROW_GROUPS = 2
HEADS_PER_STEP = 2
TILES_PER_STEP = 8
MXU_TILE = 256
VALUE_ROWS = V_DIM + 16
SCORE_SLOTS = (0, MXU_TILE // 4, 2 * MXU_TILE // 4)
OUT_SLOTS = (3 * MXU_TILE // 4, 3 * MXU_TILE // 4 + VALUE_ROWS // 4)
NT_DIMS = (((1,), (1,)), ((), ()))


def _rms(x, g):
    return x * lax.rsqrt(jnp.mean(x * x, axis=-1, keepdims=True) + EPS) * g


def _silu(x):
    return x / (1.0 + jnp.exp(-x))


def _params(n_axes):
    return pltpu.CompilerParams(dimension_semantics=("arbitrary",) * n_axes,
                                vmem_limit_bytes=VMEM_LIMIT)


def _resident(shape, index_map):
    return pl.BlockSpec(shape, index_map, pipeline_mode=pl.Buffered(1))


def _row_spec(width, layer):
    return pl.BlockSpec((None, 1, width), lambda *_: (layer, 0, 0))


def _mod_spec(layer):
    return pl.BlockSpec((None, 1, N_MOD, D_MODEL), lambda b, *_: (layer, b, 0, 0))


def _mod_kernel(c_ref, w_ref, b_ref, o_ref):
    c = c_ref[...]
    ca = _silu(c).astype(BF16)
    o_ref[0] = jnp.dot(ca, w_ref[0].astype(BF16), preferred_element_type=F32) + b_ref[0]


def _modulation(c, w_ada, b_ada):
    L, D, N = w_ada.shape
    B = c.shape[0]
    tn = N // 8
    out = pl.pallas_call(
        _mod_kernel,
        grid=(L, N // tn),
        in_specs=[pl.BlockSpec((B, D), lambda l, n: (0, 0)),
                  pl.BlockSpec((1, D, tn), lambda l, n: (l, 0, n)),
                  pl.BlockSpec((1, 1, tn), lambda l, n: (l, 0, n))],
        out_specs=pl.BlockSpec((1, B, tn), lambda l, n: (l, 0, n)),
        out_shape=jax.ShapeDtypeStruct((L, B, N), F32),
        compiler_params=_params(2),
        name="adaln_mod",
    )(c, w_ada, b_ada.reshape(L, 1, N))
    return out.reshape(L, B, N_MOD, D)


def _rope_kernel(pos_ref, invf_ref, sgn_ref, cos_ref, sin_ref):
    ang = invf_ref[...] * pos_ref[0].astype(F32)
    cos_ref[0] = jnp.cos(ang)
    sin_ref[0] = jnp.sin(ang) * sgn_ref[...]


def _rope_tables(positions):
    B, S = positions.shape
    ts = min(S, 2048)
    inv_freq = 1.0 / (ROPE_THETA ** (jnp.arange(0, QK_ROPE, 2, dtype=F32) / QK_ROPE))
    invf = jnp.concatenate([inv_freq, inv_freq])[:, None]
    ones = jnp.ones((HALF_ROPE,), F32)
    sgn = jnp.concatenate([-ones, ones])[:, None]
    tab = jax.ShapeDtypeStruct((B, QK_ROPE, S), F32)
    col = pl.BlockSpec((QK_ROPE, 1), lambda b, i: (0, 0))
    out = pl.BlockSpec((1, QK_ROPE, ts), lambda b, i: (b, 0, i))
    return pl.pallas_call(
        _rope_kernel,
        grid=(B, S // ts),
        in_specs=[pl.BlockSpec((1, 1, ts), lambda b, i: (b, 0, i)), col, col],
        out_specs=[out, out],
        out_shape=[tab, tab],
        compiler_params=_params(2),
        name="rope_tables",
    )(positions.reshape(B, 1, S), invf, sgn)


def _swiglu_rows(x, mod_ref, mod_base, g_ref, wgu_ref, wd_ref):
    shift = mod_ref[0, mod_base:mod_base + 1, :]
    scale = mod_ref[0, mod_base + 1:mod_base + 2, :]
    gate = mod_ref[0, mod_base + 2:mod_base + 3, :]
    h = (_rms(x, g_ref[...]) * (1.0 + scale) + shift).astype(BF16)
    gu = jnp.dot(h, wgu_ref[...], preferred_element_type=F32)
    a = (_silu(gu[:, :D_FF]) * gu[:, D_FF:]).astype(BF16)
    y = jnp.dot(a, wd_ref[...], preferred_element_type=F32)
    return x + (0.5 * (1.0 + gate)) * y


def _merge_ffn_kernel(x_ref, mod_ref, attn_ref, conv_ref, ag_ref, wo_ref, g_ref, wgu_ref, wd_ref, *rest, final):
    o_ref = rest[-1]
    an = _rms(attn_ref[0], ag_ref[...]).astype(BF16)
    ym = jnp.dot(an, wo_ref[0:ATTN_W, :], preferred_element_type=F32)
    ym = ym + jnp.dot(conv_ref[0], wo_ref[ATTN_W:, :], preferred_element_type=F32)
    x = x_ref[0] + (1.0 + mod_ref[0, 5:6, :]) * ym
    out = _swiglu_rows(x, mod_ref, 6, g_ref, wgu_ref, wd_ref)
    if final:
        out = _rms(out, rest[0][...])
    o_ref[0] = out


def _merge_ffn(x, mod_l, attn, conv, ag, wo, g, wgu, wd, layer, tm, final_g=None):
    B, S, D = x.shape
    tok = lambda w: pl.BlockSpec((1, tm, w), lambda b, i: (b, i, 0))
    wspec = lambda r, c: _resident((None, r, c), lambda b, i: (layer, 0, 0))
    in_specs = [tok(D), _mod_spec(layer), tok(ATTN_W), tok(CONV_W), _row_spec(ATTN_W, layer),
                wspec(ATTN_W + CONV_W, D), _row_spec(D, layer), wspec(D, 2 * D_FF), wspec(D_FF, D)]
    args = [x, mod_l, attn, conv, ag, wo, g, wgu, wd]
    if final_g is not None:
        in_specs.append(pl.BlockSpec((1, D), lambda b, i: (0, 0)))
        args.append(final_g)
    return pl.pallas_call(
        functools.partial(_merge_ffn_kernel, final=final_g is not None),
        grid=(B, S // tm),
        in_specs=in_specs,
        out_specs=tok(D),
        out_shape=jax.ShapeDtypeStruct((B, S, D), F32),
        compiler_params=_params(2),
        name="merge_ffn_final" if final_g is not None else "merge_ffn",
    )(*args)


def _rope_t(rows, cos, sin):
    partner = jnp.concatenate([rows[HALF_ROPE:], rows[:HALF_ROPE]], axis=0)
    return rows * cos + partner * sin


def _ffn_proj_kernel(x_ref, mod_ref, g1_ref, wgu_ref, wd_ref,
                     g_ref, win_ref, wkpe_ref, qg_ref, wuq_ref, kvg_ref, wk_ref, wv_ref,
                     cw_ref, cg_ref, cos_ref, sin_ref,
                     x1_ref, qt_ref, k_ref, vt_ref, conv_ref, hist_ref, *, tm):
    @pl.when(pl.program_id(1) == 0)
    def _():
        hist_ref[0:HALO, :] = jnp.zeros((HALO, CONV_W), F32)

    for r in range(FFN_ROW_GROUPS):
        rows = slice(r * tm // FFN_ROW_GROUPS, (r + 1) * tm // FFN_ROW_GROUPS)
        x1_ref[0, rows, :] = _swiglu_rows(x_ref[0, rows, :], mod_ref, 0, g1_ref, wgu_ref, wd_ref)

    x = x1_ref[0]
    shift = mod_ref[0, 3:4, :]
    scale = mod_ref[0, 4:5, :]
    h = (_rms(x, g_ref[...]) * (1.0 + scale) + shift).astype(BF16)
    proj = jnp.dot(h, win_ref[:, 0:P_GB], preferred_element_type=F32)
    cos = cos_ref[0]
    sin = sin_ref[0]
    rope_lo, rope_hi = QK_NOPE, QK_NOPE + QK_ROPE

    cqn = _rms(proj[:, P_CQ:P_CQ + Q_RANK], qg_ref[...]).astype(BF16)
    qt = lax.dot_general(wuq_ref[...], cqn, NT_DIMS, preferred_element_type=F32)
    gates = jnp.dot(h, win_ref[:, P_GB:P_WIDTH], preferred_element_type=F32)
    ckvn = _rms(proj[:, P_CKV:P_CKV + KV_RANK], kvg_ref[...]).astype(BF16)
    kk = jnp.dot(ckvn, wk_ref[...], preferred_element_type=F32)
    vt = lax.dot_general(wv_ref[...], ckvn, NT_DIMS, preferred_element_type=F32)
    kpe_t = lax.dot_general(wkpe_ref[...], h, NT_DIMS, preferred_element_type=F32)
    kpe_t = jnp.concatenate([jnp.zeros((rope_lo, tm), F32), _rope_t(kpe_t, cos, sin),
                             jnp.zeros((HEAD_SLAB - rope_hi, tm), F32)], axis=0)
    kpe = kpe_t.T
    for hd in range(N_HEADS):
        lo = hd * HEAD_SLAB
        qh = qt[lo:lo + HEAD_SLAB]
        qh = jnp.concatenate([qh[:rope_lo], _rope_t(qh[rope_lo:rope_hi], cos, sin), qh[rope_hi:]], axis=0)
        qh = (qh * Q_SCALE).astype(BF16)
        vh = vt[hd * V_DIM:(hd + 1) * V_DIM].astype(BF16)
        for j in range(tm // MXU_TILE):
            qt_ref[0, hd, j] = qh[:, j * MXU_TILE:(j + 1) * MXU_TILE]
            vt_ref[0, hd, j] = vh[:, j * MXU_TILE:(j + 1) * MXU_TILE]
        k_ref[0, hd] = (kk[:, lo:lo + HEAD_SLAB] + kpe).astype(BF16)

    cv =gates[:, CONV_W:2 * CONV_W] * gates[:, 2 * CONV_W:3 * CONV_W]
    hist_ref[HALO:HALO + tm, :] = cv
    cv1 = hist_ref[HALO - 1:HALO - 1 + tm, :]
    cv2 = hist_ref[HALO - 2:HALO - 2 + tm, :]
    conv = cw_ref[0:1, :] * cv2 + cw_ref[1:2, :] * cv1 + cw_ref[2:3, :] * cv
    conv = gates[:, 0:CONV_W] * conv
    conv_ref[0] = _rms(conv, cg_ref[...]).astype(BF16)
    hist_ref[0:HALO, :] = cv[tm - HALO:, :]


def _ffn_proj(x, mod_l, g1, wgu, wd, g, win, wkpe, qg, wuq, kvg, wk, wv, cw, cg, cos, sin, layer, tm):
    B, S, D = x.shape
    tok = lambda w: pl.BlockSpec((1, tm, w), lambda b, i: (b, i, 0))
    tok_t = lambda r: pl.BlockSpec((1, r, tm), lambda b, i: (b, 0, i))
    row = lambda w: _row_spec(w, layer)
    wspec = lambda r, c: _resident((None, r, c), lambda b, i: (layer, 0, 0))
    return pl.pallas_call(
        functools.partial(_ffn_proj_kernel, tm=tm),
        grid=(B, S // tm),
        in_specs=[tok(D),
                  _mod_spec(layer),
                  row(D), wspec(D, 2 * D_FF), wspec(D_FF, D),
                  row(D), wspec(D, P_WIDTH), wspec(QK_ROPE, D),
                  row(Q_RANK), wspec(N_HEADS * HEAD_SLAB, Q_RANK),
                  row(KV_RANK), wspec(KV_RANK, N_HEADS * HEAD_SLAB), wspec(ATTN_W, KV_RANK),
                  pl.BlockSpec((None, 3, CONV_W), lambda b, i: (layer, 0, 0)),
                  row(CONV_W), tok_t(QK_ROPE), tok_t(QK_ROPE)],
        out_specs=[tok(D),
                   pl.BlockSpec((1, N_HEADS, tm // MXU_TILE, HEAD_SLAB, MXU_TILE), lambda b, i: (b, 0, i, 0, 0)),
                   pl.BlockSpec((1, N_HEADS, tm, HEAD_SLAB), lambda b, i: (b, 0, i, 0)),
                   pl.BlockSpec((1, N_HEADS, tm // MXU_TILE, V_DIM, MXU_TILE), lambda b, i: (b, 0, i, 0, 0)),
                   tok(CONV_W)],
        out_shape=[jax.ShapeDtypeStruct((B, S, D), F32),
                   jax.ShapeDtypeStruct((B, N_HEADS, S // MXU_TILE, HEAD_SLAB, MXU_TILE), BF16),
                   jax.ShapeDtypeStruct((B, N_HEADS, S, HEAD_SLAB), BF16),
                   jax.ShapeDtypeStruct((B, N_HEADS, S // MXU_TILE, V_DIM, MXU_TILE), BF16),
                   jax.ShapeDtypeStruct((B, S, CONV_W), BF16)],
        scratch_shapes=[pltpu.VMEM((HALO + tm, CONV_W), F32)],
        compiler_params=_params(2),
        name="ffn_proj",
    )(x, mod_l, g1, wgu, wd, g, win, wkpe, qg, wuq, kvg, wk, wv, cw, cg, cos, sin)


def _attn_kernel(qt_ref, k_ref, vt_ref, o_ref, s_ref, acc_ref, cm_ref, *, tq, n_tiles):
    g = pl.program_id(2)
    hb = MXU_TILE
    krow = lax.broadcasted_iota(jnp.int32, (hb, hb), 0)
    qcol = lax.broadcasted_iota(jnp.int32, (hb, hb), 1)
    tri = krow <= qcol
    zpad = jnp.zeros((hb, hb - HEAD_SLAB), BF16)
    zrows = jnp.zeros((hb - HEAD_SLAB, hb), BF16)
    ones = jnp.ones((VALUE_ROWS - V_DIM, hb), BF16)
    heads = range(HEADS_PER_STEP)

    def q_half(h, tile, qh):
        return jnp.concatenate([qt_ref[0, h, tile * (tq // hb) + qh], zrows], axis=0)

    def k_half(h, blk, kh):
        start = pl.multiple_of(blk * tq + kh * hb, hb)
        return jnp.concatenate([k_ref[0, h, pl.ds(start, hb), :], zpad], axis=1)

    def v_half(h, blk, kh):
        return jnp.concatenate([vt_ref[0, h, blk * (tq // hb) + kh], ones], axis=0)

    def quad_ref(h, kh, qh):
        return s_ref.at[h, 2 * kh + qh]

    def prestage(h, tile):
        pltpu.matmul_push_rhs(q_half(h, tile, 0), 1, h)

    def drain_stage(h):
        pltpu.matmul_acc_lhs(SCORE_SLOTS[1], jnp.zeros((16, hb), BF16), h, 1)
        pltpu.matmul_pop(SCORE_SLOTS[1], (16, hb), F32, h)

    def issue3(h, tile, blk):
        pltpu.matmul_acc_lhs(SCORE_SLOTS[0], k_half(h, blk, 0), h, 1)
        pltpu.matmul_acc_lhs(SCORE_SLOTS[1], k_half(h, blk, 1), h, None)
        pltpu.matmul_push_rhs(q_half(h, tile, 1), 0, h)
        pltpu.matmul_acc_lhs(SCORE_SLOTS[2], k_half(h, blk, 0), h, 0)

    def issue4(h, blk):
        pltpu.matmul_acc_lhs(SCORE_SLOTS[0], k_half(h, blk, 1), h, None)

    def pop_quad(h, slot, kh, qh, diag):
        st = pltpu.matmul_pop(slot, (hb, hb), F32, h)
        if diag:
            if kh > qh:
                st = jnp.full((hb, hb), -jnp.inf, F32)
            elif kh == qh:
                st = jnp.where(tri, st, -jnp.inf)
        quad_ref(h, kh, qh)[...] = st
        return jnp.max(st, axis=0, keepdims=True)

    def numer(h, kh, qh, m_new, reg):
        mq = m_new[:, qh * hb:(qh + 1) * hb]
        p = jnp.exp2(quad_ref(h, kh, qh)[...] - mq).astype(BF16)
        pltpu.matmul_push_rhs(p, reg, h)

    def pv(h, blk, qh, between=None):
        pltpu.matmul_acc_lhs(OUT_SLOTS[qh], v_half(h, blk, 0), h, 1)
        if between is not None:
            between()
        pltpu.matmul_acc_lhs(OUT_SLOTS[qh], v_half(h, blk, 1), h, 0)

    def accumulate(h, alpha):
        out = jnp.concatenate([pltpu.matmul_pop(OUT_SLOTS[qh], (VALUE_ROWS, hb), F32, h) for qh in range(2)], axis=1)
        acc_ref[h] = alpha * acc_ref[h] + out

    def col_max(c):
        return jnp.concatenate([jnp.maximum(c[0], c[1]), jnp.maximum(c[2], c[3])], axis=1)

    def order(i, t):
        return jnp.where(t == 0, i, t - 1)

    @pl.when(g == 0)
    def _():
        for h in heads:
            drain_stage(h)
            prestage(h, 0)
            issue3(h, 0, 0)
        for h in heads:
            c = [pop_quad(h, SCORE_SLOTS[0], 0, 0, True)]
            issue4(h, 0)
            c.append(pop_quad(h, SCORE_SLOTS[1], 1, 0, True))
            c.append(pop_quad(h, SCORE_SLOTS[2], 0, 1, True))
            c.append(pop_quad(h, SCORE_SLOTS[0], 1, 1, True))
            cm_ref[h] = col_max(c)
            prestage(h, min(1, n_tiles - 1))

    def step(blk, state, nxt, staged_tile=None):
        ms, alphas, cs = [], [], [[] for _ in heads]
        for h in heads:
            m, cm, _ = state[h]
            m_new = jnp.maximum(m, cm)
            ms.append(m_new)
            alphas.append(jnp.exp2(m - m_new))
        for h in heads:
            if nxt is not None:
                issue3(h, nxt[0], nxt[1])
            else:
                drain_stage(h)
        for h in heads:
            numer(h, 0, 0, ms[h], 1)
        for h in heads:
            if nxt is not None:
                cs[h].append(pop_quad(h, SCORE_SLOTS[0], 0, 0, nxt[2]))
                issue4(h, nxt[1])
        for h in heads:
            numer(h, 1, 0, ms[h], 0)
        if nxt is not None:
            for h in heads:
                cs[h].append(pop_quad(h, SCORE_SLOTS[1], 1, 0, nxt[2]))
        for h in heads:
            accumulate(h, state[h][2])
            pv(h, blk, 0)
        for h in heads:
            numer(h, 0, 1, ms[h], 1)
            numer(h, 1, 1, ms[h], 0)
            pv(h, blk, 1, (lambda h=h: prestage(h, nxt[0] if staged_tile is None else staged_tile))
               if nxt is not None else None)
        out = []
        for h in heads:
            if nxt is not None:
                cs[h].append(pop_quad(h, SCORE_SLOTS[2], 0, 1, nxt[2]))
                cs[h].append(pop_quad(h, SCORE_SLOTS[0], 1, 1, nxt[2]))
                out.append((ms[h], col_max(cs[h]), alphas[h]))
            else:
                out.append((ms[h], state[h][1], alphas[h]))
        return tuple(out)

    def run_tile(i, sub):
        for h in heads:
            acc_ref[h] = jnp.zeros((VALUE_ROWS, tq), F32)
        state = tuple((jnp.full((1, tq), -jnp.inf, F32), cm_ref[h], jnp.ones((1, tq), F32)) for h in heads)
        next_tile = jnp.minimum(i + 1, n_tiles - 1)
        state = lax.fori_loop(
            0, i,
            lambda t, st: step(order(i, t), st, (i, t, False), jnp.where(t == i - 1, next_tile, i)),
            state)

        def finish(nxt):
            last = step(order(i, i), state, nxt)
            outs = []
            for h in heads:
                if nxt is not None:
                    cm_ref[h] = last[h][1]
                accumulate(h, last[h][2])
                acc = acc_ref[h]
                outs.append(acc[0:V_DIM, :] / acc[V_DIM:V_DIM + 1, :])
            o_ref[0, sub * tq:(sub + 1) * tq, :] = jnp.concatenate(outs, axis=0).T

        if sub < TILES_PER_STEP - 1:
            finish((i + 1, i + 1, True))
        else:
            @pl.when(i < n_tiles - 1)
            def _():
                finish((i + 1, i + 1, True))

            @pl.when(i == n_tiles - 1)
            def _():
                finish(None)

    for sub in range(TILES_PER_STEP):
        run_tile(g * TILES_PER_STEP + sub, sub)


def _attention(qt, k, vt, tq):
    B, H, S, _ = k.shape
    hps = HEADS_PER_STEP
    assert tq == 2 * MXU_TILE and HEAD_SLAB <= MXU_TILE and S % (tq * TILES_PER_STEP) == 0
    return pl.pallas_call(
        functools.partial(_attn_kernel, tq=tq, n_tiles=S // tq),
        grid=(B, H // hps, S // (tq * TILES_PER_STEP)),
        in_specs=[pl.BlockSpec((1, hps, S // MXU_TILE, HEAD_SLAB, MXU_TILE), lambda b, p, i: (b, p, 0, 0, 0)),
                  pl.BlockSpec((1, hps, S, HEAD_SLAB), lambda b, p, i: (b, p, 0, 0)),
                  pl.BlockSpec((1, hps, S // MXU_TILE, V_DIM, MXU_TILE), lambda b, p, i: (b, p, 0, 0, 0))],
        out_specs=pl.BlockSpec((1, tq * TILES_PER_STEP, hps * V_DIM), lambda b, p, i: (b, i, p)),
        out_shape=jax.ShapeDtypeStruct((B, S, ATTN_W), F32),
        scratch_shapes=[pltpu.VMEM((hps, 4, MXU_TILE, MXU_TILE), F32),
                        pltpu.VMEM((hps, VALUE_ROWS, tq), F32),
                        pltpu.VMEM((hps, 1, tq), F32)],
        compiler_params=_params(3),
        name="mla_attention",
    )(qt, k, vt)


def _layout_w_in(w_in):
    lat = Q_RANK + KV_RANK
    main = jnp.concatenate([w_in[:, :, :lat], w_in[:, :, lat + QK_ROPE:]], axis=-1)
    kpe_t = jnp.swapaxes(w_in[:, :, lat:lat + QK_ROPE], 1, 2)
    return main.astype(BF16), kpe_t.astype(BF16)


def _layout_w_uq(w_uq):
    L, R, _ = w_uq.shape
    w = w_uq.reshape(L, R, N_HEADS, QK_NOPE + QK_ROPE)
    z_pad = jnp.zeros((L, R, N_HEADS, HEAD_SLAB - QK_NOPE - QK_ROPE), w.dtype)
    q = jnp.concatenate([w, z_pad], axis=-1).reshape(L, R, N_HEADS * HEAD_SLAB)
    return jnp.swapaxes(q, 1, 2).astype(BF16)


def _layout_w_ukv(w_ukv):
    L, R, _ = w_ukv.shape
    w = w_ukv.reshape(L, R, N_HEADS, QK_NOPE + V_DIM)
    k_nope, v = w[..., :QK_NOPE], w[..., QK_NOPE:]
    z = jnp.zeros((L, R, N_HEADS, HEAD_SLAB - QK_NOPE), w.dtype)
    wk = jnp.concatenate([k_nope, z], axis=-1).reshape(L, R, N_HEADS * HEAD_SLAB)
    wv_t = jnp.swapaxes(v.reshape(L, R, ATTN_W), 1, 2)
    return wk.astype(BF16), wv_t.astype(BF16)


def kernel(x, c, positions, w_ada, b_ada, ffn1_norm, ffn1_w_gu, ffn1_w_down, mix_norm, w_in, q_a_norm, w_uq, kv_a_norm, w_ukv, conv_w, attn_out_norm, conv_out_norm, w_o, ffn2_norm, ffn2_w_gu, ffn2_w_down, final_norm):
    B, S, D = x.shape
    L = w_ada.shape[0]
    tm = min(S, 512)
    tq = min(S, 512)

    mod = _modulation(c, w_ada, b_ada)
    cos, sin = _rope_tables(positions)

    wgu1, wd1 = ffn1_w_gu.astype(BF16), ffn1_w_down.astype(BF16)
    wgu2, wd2 = ffn2_w_gu.astype(BF16), ffn2_w_down.astype(BF16)
    win, wkpe_t = _layout_w_in(w_in)
    wuq_t = _layout_w_uq(w_uq)
    wk, wv_t = _layout_w_ukv(w_ukv)
    wo = w_o.astype(BF16)
    final_g = final_norm.reshape(1, D)
    rows = lambda g: g.reshape(L, 1, g.shape[-1])
    g1, g2, gm = rows(ffn1_norm), rows(ffn2_norm), rows(mix_norm)
    gq, gkv = rows(q_a_norm), rows(kv_a_norm)
    ga, gc = rows(attn_out_norm), rows(conv_out_norm)

    for l in range(L):
        x, qt, k, vt, conv = _ffn_proj(x, mod, g1, wgu1, wd1, gm, win, wkpe_t, gq, wuq_t, gkv, wk, wv_t,
                                       conv_w, gc, cos, sin, l, tm)
        attn = _attention(qt, k, vt, tq)
        x = _merge_ffn(x, mod, attn, conv, ga, wo, g2, wgu2, wd2, l, tm,
                       final_g=final_g if l == L - 1 else None)
    return x
```

```python
import functools

import jax
import jax.numpy as jnp
from jax import lax
from jax.experimental import pallas as pl
from jax.experimental.pallas import tpu as pltpu

F32 = jnp.float32
BF16 = jnp.bfloat16

D_MODEL = 1024
N_HEADS = 8
QK_NOPE = 64
QK_ROPE = 32
V_DIM = 64
Q_RANK = 384
KV_RANK = 256
CONV_W = 512
ATTN_W = N_HEADS * V_DIM
D_FF = 2816
N_MOD = 9
EPS = 1e-6
ROPE_THETA = 10000.0
HEAD_SLAB = 128
HALF_ROPE = QK_ROPE // 2
P_CQ = 0
P_CKV = P_CQ + Q_RANK
P_GB = P_CKV + KV_RANK
P_WIDTH = P_GB + 3 * CONV_W
HALO = 8
Q_SCALE = float((QK_NOPE + QK_ROPE) ** -0.5 * 1.4426950408889634)
VMEM_LIMIT = 56 * 1024 * 1024
FFN_ROW_GROUPS = 2
HEADS_PER_STEP = 2
TILES_PER_STEP = 4
MXU_TILE = 256
VALUE_ROWS = V_DIM + 16
SCORE_SLOTS = (0, MXU_TILE // 4, 2 * MXU_TILE // 4)
OUT_SLOTS = (3 * MXU_TILE // 4, 3 * MXU_TILE // 4 + VALUE_ROWS // 4)
NT_DIMS = (((1,), (1,)), ((), ()))


def _rms(x, g):
    return x * lax.rsqrt(jnp.mean(x * x, axis=-1, keepdims=True) + EPS) * g


def _silu(x):
    return x / (1.0 + jnp.exp(-x))


def _params(n_axes):
    return pltpu.CompilerParams(dimension_semantics=("arbitrary",) * n_axes,
                                vmem_limit_bytes=VMEM_LIMIT)


def _resident(shape, index_map):
    return pl.BlockSpec(shape, index_map, pipeline_mode=pl.Buffered(1))


def _row_spec(width, layer):
    return pl.BlockSpec((None, 1, width), lambda *_: (layer, 0, 0))


def _mod_spec(layer):
    return pl.BlockSpec((None, 1, N_MOD, D_MODEL), lambda b, *_: (layer, b, 0, 0))


def _mod_kernel(c_ref, w_ref, b_ref, o_ref):
    c = c_ref[...]
    ca = _silu(c).astype(BF16)
    o_ref[0] = jnp.dot(ca, w_ref[0].astype(BF16), preferred_element_type=F32) + b_ref[0]


def _modulation(c, w_ada, b_ada):
    L, D, N = w_ada.shape
    B = c.shape[0]
    tn = N // 8
    out = pl.pallas_call(
        _mod_kernel,
        grid=(L, N // tn),
        in_specs=[pl.BlockSpec((B, D), lambda l, n: (0, 0)),
                  pl.BlockSpec((1, D, tn), lambda l, n: (l, 0, n)),
                  pl.BlockSpec((1, 1, tn), lambda l, n: (l, 0, n))],
        out_specs=pl.BlockSpec((1, B, tn), lambda l, n: (l, 0, n)),
        out_shape=jax.ShapeDtypeStruct((L, B, N), F32),
        compiler_params=_params(2),
        name="adaln_mod",
    )(c, w_ada, b_ada.reshape(L, 1, N))
    return out.reshape(L, B, N_MOD, D)


def _rope_kernel(pos_ref, invf_ref, sgn_ref, cos_ref, sin_ref):
    ang = invf_ref[...] * pos_ref[0].astype(F32)
    cos_ref[0] = jnp.cos(ang)
    sin_ref[0] = jnp.sin(ang) * sgn_ref[...]


def _rope_tables(positions):
    B, S = positions.shape
    ts = min(S, 2048)
    inv_freq = 1.0 / (ROPE_THETA ** (jnp.arange(0, QK_ROPE, 2, dtype=F32) / QK_ROPE))
    invf = jnp.concatenate([inv_freq, inv_freq])[:, None]
    ones = jnp.ones((HALF_ROPE,), F32)
    sgn = jnp.concatenate([-ones, ones])[:, None]
    tab = jax.ShapeDtypeStruct((B, QK_ROPE, S), F32)
    col = pl.BlockSpec((QK_ROPE, 1), lambda b, i: (0, 0))
    out = pl.BlockSpec((1, QK_ROPE, ts), lambda b, i: (b, 0, i))
    return pl.pallas_call(
        _rope_kernel,
        grid=(B, S // ts),
        in_specs=[pl.BlockSpec((1, 1, ts), lambda b, i: (b, 0, i)), col, col],
        out_specs=[out, out],
        out_shape=[tab, tab],
        compiler_params=_params(2),
        name="rope_tables",
    )(positions.reshape(B, 1, S), invf, sgn)


def _swiglu_rows(x, mod_ref, mod_base, g_ref, wgu_ref, wd_ref):
    shift = mod_ref[0, mod_base:mod_base + 1, :]
    scale = mod_ref[0, mod_base + 1:mod_base + 2, :]
    gate = mod_ref[0, mod_base + 2:mod_base + 3, :]
    h = (_rms(x, g_ref[...]) * (1.0 + scale) + shift).astype(BF16)
    y = None
    for c in range(D_FF // MXU_TILE):
        lo = c * MXU_TILE
        g = jnp.dot(h, wgu_ref[:, lo:lo + MXU_TILE], preferred_element_type=F32)
        u = jnp.dot(h, wgu_ref[:, D_FF + lo:D_FF + lo + MXU_TILE], preferred_element_type=F32)
        a = (_silu(g) * u).astype(BF16)
        yc = jnp.dot(a, wd_ref[lo:lo + MXU_TILE, :], preferred_element_type=F32)
        y = yc if y is None else y + yc
    return x + (0.5 * (1.0 + gate)) * y


def _merge_ffn_kernel(x_ref, mod_ref, attn_ref, conv_ref, ag_ref, wo_ref, g_ref, wgu_ref, wd_ref, *rest, final):
    o_ref = rest[-1]
    an = _rms(attn_ref[0], ag_ref[...]).astype(BF16)
    ym = jnp.dot(an, wo_ref[0:ATTN_W, :], preferred_element_type=F32)
    ym = ym + jnp.dot(conv_ref[0], wo_ref[ATTN_W:, :], preferred_element_type=F32)
    x = x_ref[0] + (1.0 + mod_ref[0, 5:6, :]) * ym
    out = _swiglu_rows(x, mod_ref, 6, g_ref, wgu_ref, wd_ref)
    if final:
        out = _rms(out, rest[0][...])
    o_ref[0] = out


def _merge_ffn(x, mod_l, attn, conv, ag, wo, g, wgu, wd, layer, tm, final_g=None):
    B, S, D = x.shape
    tok = lambda w: pl.BlockSpec((1, tm, w), lambda b, i: (b, i, 0))
    wspec = lambda r, c: _resident((None, r, c), lambda b, i: (layer, 0, 0))
    in_specs = [tok(D), _mod_spec(layer), tok(ATTN_W), tok(CONV_W), _row_spec(ATTN_W, layer),
                wspec(ATTN_W + CONV_W, D), _row_spec(D, layer), wspec(D, 2 * D_FF), wspec(D_FF, D)]
    args = [x, mod_l, attn, conv, ag, wo, g, wgu, wd]
    if final_g is not None:
        in_specs.append(pl.BlockSpec((1, D), lambda b, i: (0, 0)))
        args.append(final_g)
    return pl.pallas_call(
        functools.partial(_merge_ffn_kernel, final=final_g is not None),
        grid=(B, S // tm),
        in_specs=in_specs,
        out_specs=tok(D),
        out_shape=jax.ShapeDtypeStruct((B, S, D), F32),
        compiler_params=_params(2),
        name="merge_ffn_final" if final_g is not None else "merge_ffn",
    )(*args)


def _rope_t(rows, cos, sin):
    partner = jnp.concatenate([rows[HALF_ROPE:], rows[:HALF_ROPE]], axis=0)
    return rows * cos + partner * sin


def _ffn_proj_kernel(x_ref, mod_ref, g1_ref, wgu_ref, wd_ref,
                     g_ref, win_ref, wkpe_ref, qg_ref, wuq_ref, kvg_ref, wk_ref, wv_ref,
                     cw_ref, cg_ref, cos_ref, sin_ref,
                     x1_ref, qt_ref, k_ref, vt_ref, conv_ref, hist_ref, *, tm):
    @pl.when(pl.program_id(1) == 0)
    def _():
        hist_ref[0:HALO, :] = jnp.zeros((HALO, CONV_W), F32)

    for r in range(FFN_ROW_GROUPS):
        rows = slice(r * tm // FFN_ROW_GROUPS, (r + 1) * tm // FFN_ROW_GROUPS)
        x1_ref[0, rows, :] = _swiglu_rows(x_ref[0, rows, :], mod_ref, 0, g1_ref, wgu_ref, wd_ref)

    x = x1_ref[0]
    shift = mod_ref[0, 3:4, :]
    scale = mod_ref[0, 4:5, :]
    h = (_rms(x, g_ref[...]) * (1.0 + scale) + shift).astype(BF16)
    proj = jnp.dot(h, win_ref[:, 0:P_GB], preferred_element_type=F32)
    cos = cos_ref[0]
    sin = sin_ref[0]
    rope_lo, rope_hi = QK_NOPE, QK_NOPE + QK_ROPE

    cqn = _rms(proj[:, P_CQ:P_CQ + Q_RANK], qg_ref[...]).astype(BF16)
    qt = lax.dot_general(wuq_ref[...], cqn, NT_DIMS, preferred_element_type=F32)
    gates = jnp.dot(h, win_ref[:, P_GB:P_WIDTH], preferred_element_type=F32)
    ckvn = _rms(proj[:, P_CKV:P_CKV + KV_RANK], kvg_ref[...]).astype(BF16)
    kk = jnp.dot(ckvn, wk_ref[...], preferred_element_type=F32)
    vt = lax.dot_general(wv_ref[...], ckvn, NT_DIMS, preferred_element_type=F32)
    kpe_t = lax.dot_general(wkpe_ref[...], h, NT_DIMS, preferred_element_type=F32)
    kpe_t = jnp.concatenate([jnp.zeros((rope_lo, tm), F32), _rope_t(kpe_t, cos, sin),
                             jnp.zeros((HEAD_SLAB - rope_hi, tm), F32)], axis=0)
    kpe = kpe_t.T
    for hd in range(N_HEADS):
        lo = hd * HEAD_SLAB
        qh = qt[lo:lo + HEAD_SLAB]
        qh = jnp.concatenate([qh[:rope_lo], _rope_t(qh[rope_lo:rope_hi], cos, sin), qh[rope_hi:]], axis=0)
        qh = (qh * Q_SCALE).astype(BF16)
        vh = vt[hd * V_DIM:(hd + 1) * V_DIM].astype(BF16)
        for j in range(tm // MXU_TILE):
            qt_ref[0, hd, j] = qh[:, j * MXU_TILE:(j + 1) * MXU_TILE]
            vt_ref[0, hd, j] = vh[:, j * MXU_TILE:(j + 1) * MXU_TILE]
        k_ref[0, hd] = (kk[:, lo:lo + HEAD_SLAB] + kpe).astype(BF16)

    cv =gates[:, CONV_W:2 * CONV_W] * gates[:, 2 * CONV_W:3 * CONV_W]
    hist_ref[HALO:HALO + tm, :] = cv
    cv1 = hist_ref[HALO - 1:HALO - 1 + tm, :]
    cv2 = hist_ref[HALO - 2:HALO - 2 + tm, :]
    conv = cw_ref[0:1, :] * cv2 + cw_ref[1:2, :] * cv1 + cw_ref[2:3, :] * cv
    conv = gates[:, 0:CONV_W] * conv
    conv_ref[0] = _rms(conv, cg_ref[...]).astype(BF16)
    hist_ref[0:HALO, :] = cv[tm - HALO:, :]


def _ffn_proj(x, mod_l, g1, wgu, wd, g, win, wkpe, qg, wuq, kvg, wk, wv, cw, cg, cos, sin, layer, tm):
    B, S, D = x.shape
    tok = lambda w: pl.BlockSpec((1, tm, w), lambda b, i: (b, i, 0))
    tok_t = lambda r: pl.BlockSpec((1, r, tm), lambda b, i: (b, 0, i))
    row = lambda w: _row_spec(w, layer)
    wspec = lambda r, c: _resident((None, r, c), lambda b, i: (layer, 0, 0))
    return pl.pallas_call(
        functools.partial(_ffn_proj_kernel, tm=tm),
        grid=(B, S // tm),
        in_specs=[tok(D),
                  _mod_spec(layer),
                  row(D), wspec(D, 2 * D_FF), wspec(D_FF, D),
                  row(D), wspec(D, P_WIDTH), wspec(QK_ROPE, D),
                  row(Q_RANK), wspec(N_HEADS * HEAD_SLAB, Q_RANK),
                  row(KV_RANK), wspec(KV_RANK, N_HEADS * HEAD_SLAB), wspec(ATTN_W, KV_RANK),
                  pl.BlockSpec((None, 3, CONV_W), lambda b, i: (layer, 0, 0)),
                  row(CONV_W), tok_t(QK_ROPE), tok_t(QK_ROPE)],
        out_specs=[tok(D),
                   pl.BlockSpec((1, N_HEADS, tm // MXU_TILE, HEAD_SLAB, MXU_TILE), lambda b, i: (b, 0, i, 0, 0)),
                   pl.BlockSpec((1, N_HEADS, tm, HEAD_SLAB), lambda b, i: (b, 0, i, 0)),
                   pl.BlockSpec((1, N_HEADS, tm // MXU_TILE, V_DIM, MXU_TILE), lambda b, i: (b, 0, i, 0, 0)),
                   tok(CONV_W)],
        out_shape=[jax.ShapeDtypeStruct((B, S, D), F32),
                   jax.ShapeDtypeStruct((B, N_HEADS, S // MXU_TILE, HEAD_SLAB, MXU_TILE), BF16),
                   jax.ShapeDtypeStruct((B, N_HEADS, S, HEAD_SLAB), BF16),
                   jax.ShapeDtypeStruct((B, N_HEADS, S // MXU_TILE, V_DIM, MXU_TILE), BF16),
                   jax.ShapeDtypeStruct((B, S, CONV_W), BF16)],
        scratch_shapes=[pltpu.VMEM((HALO + tm, CONV_W), F32)],
        compiler_params=_params(2),
        name="ffn_proj",
    )(x, mod_l, g1, wgu, wd, g, win, wkpe, qg, wuq, kvg, wk, wv, cw, cg, cos, sin)


def _attn_kernel(qt_ref, k_ref, vt_ref, o_ref, s_ref, acc_ref, cm_ref, *, tq, n_tiles):
    g = pl.program_id(2)
    hb = MXU_TILE
    krow = lax.broadcasted_iota(jnp.int32, (hb, hb), 0)
    qcol = lax.broadcasted_iota(jnp.int32, (hb, hb), 1)
    tri = krow <= qcol
    zpad = jnp.zeros((hb, hb - HEAD_SLAB), BF16)
    zrows = jnp.zeros((hb - HEAD_SLAB, hb), BF16)
    ones = jnp.ones((VALUE_ROWS - V_DIM, hb), BF16)
    heads = range(HEADS_PER_STEP)

    def q_half(h, tile, qh):
        return jnp.concatenate([qt_ref[0, h, tile * (tq // hb) + qh], zrows], axis=0)

    def k_half(h, blk, kh):
        start = pl.multiple_of(blk * tq + kh * hb, hb)
        return jnp.concatenate([k_ref[0, h, pl.ds(start, hb), :], zpad], axis=1)

    def v_half(h, blk, kh):
        return jnp.concatenate([vt_ref[0, h, blk * (tq // hb) + kh], ones], axis=0)

    def quad_ref(h, kh, qh):
        return s_ref.at[h, 2 * kh + qh]

    def prestage(h, tile):
        pltpu.matmul_push_rhs(q_half(h, tile, 0), 1, h)

    def drain_stage(h):
        pltpu.matmul_acc_lhs(SCORE_SLOTS[1], jnp.zeros((16, hb), BF16), h, 1)
        pltpu.matmul_pop(SCORE_SLOTS[1], (16, hb), F32, h)

    def issue3(h, tile, blk):
        pltpu.matmul_acc_lhs(SCORE_SLOTS[0], k_half(h, blk, 0), h, 1)
        pltpu.matmul_acc_lhs(SCORE_SLOTS[1], k_half(h, blk, 1), h, None)
        pltpu.matmul_push_rhs(q_half(h, tile, 1), 0, h)
        pltpu.matmul_acc_lhs(SCORE_SLOTS[2], k_half(h, blk, 0), h, 0)

    def issue4(h, blk):
        pltpu.matmul_acc_lhs(SCORE_SLOTS[0], k_half(h, blk, 1), h, None)

    def pop_quad(h, slot, kh, qh, diag):
        st = pltpu.matmul_pop(slot, (hb, hb), F32, h)
        if diag:
            if kh > qh:
                st = jnp.full((hb, hb), -jnp.inf, F32)
            elif kh == qh:
                st = jnp.where(tri, st, -jnp.inf)
        quad_ref(h, kh, qh)[...] = st
        return jnp.max(st, axis=0, keepdims=True)

    def numer(h, kh, qh, m_new, reg):
        mq = m_new[:, qh * hb:(qh + 1) * hb]
        p = jnp.exp2(quad_ref(h, kh, qh)[...] - mq).astype(BF16)
        pltpu.matmul_push_rhs(p, reg, h)

    def pv(h, blk, qh, between=None):
        pltpu.matmul_acc_lhs(OUT_SLOTS[qh], v_half(h, blk, 0), h, 1)
        if between is not None:
            between()
        pltpu.matmul_acc_lhs(OUT_SLOTS[qh], v_half(h, blk, 1), h, 0)

    def accumulate(h, alpha):
        out = jnp.concatenate([pltpu.matmul_pop(OUT_SLOTS[qh], (VALUE_ROWS, hb), F32, h) for qh in range(2)], axis=1)
        acc_ref[h] = alpha * acc_ref[h] + out

    def col_max(c):
        return jnp.concatenate([jnp.maximum(c[0], c[1]), jnp.maximum(c[2], c[3])], axis=1)

    def order(i, t):
        return jnp.where(t == 0, i, t - 1)

    @pl.when(g == 0)
    def _():
        for h in heads:
            drain_stage(h)
            prestage(h, 0)
            issue3(h, 0, 0)
        for h in heads:
            c = [pop_quad(h, SCORE_SLOTS[0], 0, 0, True)]
            issue4(h, 0)
            c.append(pop_quad(h, SCORE_SLOTS[1], 1, 0, True))
            c.append(pop_quad(h, SCORE_SLOTS[2], 0, 1, True))
            c.append(pop_quad(h, SCORE_SLOTS[0], 1, 1, True))
            cm_ref[h] = col_max(c)
            prestage(h, min(1, n_tiles - 1))

    def step(blk, state, nxt, staged_tile=None):
        ms, alphas, cs = [], [], [[] for _ in heads]
        for h in heads:
            m, cm, _ = state[h]
            m_new = jnp.maximum(m, cm)
            ms.append(m_new)
            alphas.append(jnp.exp2(m - m_new))
        for h in heads:
            if nxt is not None:
                issue3(h, nxt[0], nxt[1])
            else:
                drain_stage(h)
        for h in heads:
            numer(h, 0, 0, ms[h], 1)
        for h in heads:
            if nxt is not None:
                cs[h].append(pop_quad(h, SCORE_SLOTS[0], 0, 0, nxt[2]))
                issue4(h, nxt[1])
        for h in heads:
            numer(h, 1, 0, ms[h], 0)
        if nxt is not None:
            for h in heads:
                cs[h].append(pop_quad(h, SCORE_SLOTS[1], 1, 0, nxt[2]))
        for h in heads:
            accumulate(h, state[h][2])
            pv(h, blk, 0)
        for h in heads:
            numer(h, 0, 1, ms[h], 1)
            numer(h, 1, 1, ms[h], 0)
            pv(h, blk, 1, (lambda h=h: prestage(h, nxt[0] if staged_tile is None else staged_tile))
               if nxt is not None else None)
        out = []
        for h in heads:
            if nxt is not None:
                cs[h].append(pop_quad(h, SCORE_SLOTS[2], 0, 1, nxt[2]))
                cs[h].append(pop_quad(h, SCORE_SLOTS[0], 1, 1, nxt[2]))
                out.append((ms[h], col_max(cs[h]), alphas[h]))
            else:
                out.append((ms[h], state[h][1], alphas[h]))
        return tuple(out)

    def run_tile(i, sub):
        for h in heads:
            acc_ref[h] = jnp.zeros((VALUE_ROWS, tq), F32)
        state = tuple((jnp.full((1, tq), -jnp.inf, F32), cm_ref[h], jnp.ones((1, tq), F32)) for h in heads)
        next_tile = jnp.minimum(i + 1, n_tiles - 1)
        state = lax.fori_loop(
            0, i,
            lambda t, st: step(order(i, t), st, (i, t, False), jnp.where(t == i - 1, next_tile, i)),
            state)

        def finish(nxt):
            last = step(order(i, i), state, nxt)
            outs = []
            for h in heads:
                if nxt is not None:
                    cm_ref[h] = last[h][1]
                accumulate(h, last[h][2])
                acc = acc_ref[h]
                outs.append(acc[0:V_DIM, :] / acc[V_DIM:V_DIM + 1, :])
            o_ref[0, sub * tq:(sub + 1) * tq, :] = jnp.concatenate(outs, axis=0).T

        if sub < TILES_PER_STEP - 1:
            finish((i + 1, i + 1, True))
        else:
            @pl.when(i < n_tiles - 1)
            def _():
                finish((i + 1, i + 1, True))

            @pl.when(i == n_tiles - 1)
            def _():
                finish(None)

    for sub in range(TILES_PER_STEP):
        run_tile(g * TILES_PER_STEP + sub, sub)


def _attention(qt, k, vt, tq):
    B, H, S, _ = k.shape
    hps = HEADS_PER_STEP
    assert tq == 2 * MXU_TILE and HEAD_SLAB <= MXU_TILE and S % (tq * TILES_PER_STEP) == 0
    return pl.pallas_call(
        functools.partial(_attn_kernel, tq=tq, n_tiles=S // tq),
        grid=(B, H // hps, S // (tq * TILES_PER_STEP)),
        in_specs=[pl.BlockSpec((1, hps, S // MXU_TILE, HEAD_SLAB, MXU_TILE), lambda b, p, i: (b, p, 0, 0, 0)),
                  pl.BlockSpec((1, hps, S, HEAD_SLAB), lambda b, p, i: (b, p, 0, 0)),
                  pl.BlockSpec((1, hps, S // MXU_TILE, V_DIM, MXU_TILE), lambda b, p, i: (b, p, 0, 0, 0))],
        out_specs=pl.BlockSpec((1, tq * TILES_PER_STEP, hps * V_DIM), lambda b, p, i: (b, i, p)),
        out_shape=jax.ShapeDtypeStruct((B, S, ATTN_W), F32),
        scratch_shapes=[pltpu.VMEM((hps, 4, MXU_TILE, MXU_TILE), F32),
                        pltpu.VMEM((hps, VALUE_ROWS, tq), F32),
                        pltpu.VMEM((hps, 1, tq), F32)],
        compiler_params=_params(3),
        name="mla_attention",
    )(qt, k, vt)


def _layout_w_in(w_in):
    lat = Q_RANK + KV_RANK
    main = jnp.concatenate([w_in[:, :, :lat], w_in[:, :, lat + QK_ROPE:]], axis=-1)
    kpe_t = jnp.swapaxes(w_in[:, :, lat:lat + QK_ROPE], 1, 2)
    return main.astype(BF16), kpe_t.astype(BF16)


def _layout_w_uq(w_uq):
    L, R, _ = w_uq.shape
    w = w_uq.reshape(L, R, N_HEADS, QK_NOPE + QK_ROPE)
    z_pad = jnp.zeros((L, R, N_HEADS, HEAD_SLAB - QK_NOPE - QK_ROPE), w.dtype)
    q = jnp.concatenate([w, z_pad], axis=-1).reshape(L, R, N_HEADS * HEAD_SLAB)
    return jnp.swapaxes(q, 1, 2).astype(BF16)


def _layout_w_ukv(w_ukv):
    L, R, _ = w_ukv.shape
    w = w_ukv.reshape(L, R, N_HEADS, QK_NOPE + V_DIM)
    k_nope, v = w[..., :QK_NOPE], w[..., QK_NOPE:]
    z = jnp.zeros((L, R, N_HEADS, HEAD_SLAB - QK_NOPE), w.dtype)
    wk = jnp.concatenate([k_nope, z], axis=-1).reshape(L, R, N_HEADS * HEAD_SLAB)
    wv_t = jnp.swapaxes(v.reshape(L, R, ATTN_W), 1, 2)
    return wk.astype(BF16), wv_t.astype(BF16)


def kernel(x, c, positions, w_ada, b_ada, ffn1_norm, ffn1_w_gu, ffn1_w_down, mix_norm, w_in, q_a_norm, w_uq, kv_a_norm, w_ukv, conv_w, attn_out_norm, conv_out_norm, w_o, ffn2_norm, ffn2_w_gu, ffn2_w_down, final_norm):
    B, S, D = x.shape
    L = w_ada.shape[0]
    tm = min(S, 512)
    tq = min(S, 512)

    mod = _modulation(c, w_ada, b_ada)
    cos, sin = _rope_tables(positions)

    wgu1, wd1 = ffn1_w_gu.astype(BF16), ffn1_w_down.astype(BF16)
    wgu2, wd2 = ffn2_w_gu.astype(BF16), ffn2_w_down.astype(BF16)
    win, wkpe_t = _layout_w_in(w_in)
    wuq_t = _layout_w_uq(w_uq)
    wk, wv_t = _layout_w_ukv(w_ukv)
    wo = w_o.astype(BF16)
    final_g = final_norm.reshape(1, D)
    rows = lambda g: g.reshape(L, 1, g.shape[-1])
    g1, g2, gm = rows(ffn1_norm), rows(ffn2_norm), rows(mix_norm)
    gq, gkv = rows(q_a_norm), rows(kv_a_norm)
    ga, gc = rows(attn_out_norm), rows(conv_out_norm)

    for l in range(L):
        x, qt, k, vt, conv = _ffn_proj(x, mod, g1, wgu1, wd1, gm, win, wkpe_t, gq, wuq_t, gkv, wk, wv_t,
                                       conv_w, gc, cos, sin, l, tm)
        attn = _attention(qt, k, vt, tq)
        x = _merge_ffn(x, mod, attn, conv, ga, wo, g2, wgu2, wd2, l, tm,
                       final_g=final_g if l == L - 1 else None)
    return x
```

```python
import functools

import jax
import jax.numpy as jnp
from jax import lax
from jax.experimental import pallas as pl
from jax.experimental.pallas import tpu as pltpu

F32 = jnp.float32
BF16 = jnp.bfloat16

D_MODEL = 1024
N_HEADS = 8
QK_NOPE = 64
QK_ROPE = 32
V_DIM = 64
Q_RANK = 384
KV_RANK = 256
CONV_W = 512
ATTN_W = N_HEADS * V_DIM
D_FF = 2816
N_MOD = 9
EPS = 1e-6
ROPE_THETA = 10000.0
HEAD_SLAB = 128
HALF_ROPE = QK_ROPE // 2
P_CQ = 0
P_CKV = P_CQ + Q_RANK
P_GB = P_CKV + KV_RANK
P_WIDTH = P_GB + 3 * CONV_W
HALO = 8
Q_SCALE = float((QK_NOPE + QK_ROPE) ** -0.5 * 1.4426950408889634)
VMEM_LIMIT = 56 * 1024 * 1024
FFN_ROW_GROUPS = 2
HEADS_PER_STEP = 2
TILES_PER_STEP = 4
MXU_TILE = 256
VALUE_ROWS = V_DIM + 16
SCORE_SLOTS = (0, MXU_TILE // 4, 2 * MXU_TILE // 4)
OUT_SLOTS = (3 * MXU_TILE // 4, 3 * MXU_TILE // 4 + VALUE_ROWS // 4)
NT_DIMS = (((1,), (1,)), ((), ()))


def _rms(x, g):
    return x * lax.rsqrt(jnp.mean(x * x, axis=-1, keepdims=True) + EPS) * g


def _silu(x):
    return x / (1.0 + jnp.exp(-x))


def _params(n_axes):
    return pltpu.CompilerParams(dimension_semantics=("arbitrary",) * n_axes,
                                vmem_limit_bytes=VMEM_LIMIT)


def _resident(shape, index_map):
    return pl.BlockSpec(shape, index_map, pipeline_mode=pl.Buffered(1))


def _row_spec(width, layer):
    return pl.BlockSpec((None, 1, width), lambda *_: (layer, 0, 0))


def _mod_spec(layer):
    return pl.BlockSpec((None, 1, N_MOD, D_MODEL), lambda b, *_: (layer, b, 0, 0))


def _mod_kernel(c_ref, w_ref, b_ref, o_ref):
    c = c_ref[...]
    ca = _silu(c).astype(BF16)
    o_ref[0] = jnp.dot(ca, w_ref[0].astype(BF16), preferred_element_type=F32) + b_ref[0]


def _modulation(c, w_ada, b_ada):
    L, D, N = w_ada.shape
    B = c.shape[0]
    tn = N // 8
    out = pl.pallas_call(
        _mod_kernel,
        grid=(L, N // tn),
        in_specs=[pl.BlockSpec((B, D), lambda l, n: (0, 0)),
                  pl.BlockSpec((1, D, tn), lambda l, n: (l, 0, n)),
                  pl.BlockSpec((1, 1, tn), lambda l, n: (l, 0, n))],
        out_specs=pl.BlockSpec((1, B, tn), lambda l, n: (l, 0, n)),
        out_shape=jax.ShapeDtypeStruct((L, B, N), F32),
        compiler_params=_params(2),
        name="adaln_mod",
    )(c, w_ada, b_ada.reshape(L, 1, N))
    return out.reshape(L, B, N_MOD, D)


def _rope_kernel(pos_ref, invf_ref, sgn_ref, cos_ref, sin_ref):
    ang = invf_ref[...] * pos_ref[0].astype(F32)
    cos_ref[0] = jnp.cos(ang)
    sin_ref[0] = jnp.sin(ang) * sgn_ref[...]


def _rope_tables(positions):
    B, S = positions.shape
    ts = min(S, 2048)
    inv_freq = 1.0 / (ROPE_THETA ** (jnp.arange(0, QK_ROPE, 2, dtype=F32) / QK_ROPE))
    invf = jnp.concatenate([inv_freq, inv_freq])[:, None]
    ones = jnp.ones((HALF_ROPE,), F32)
    sgn = jnp.concatenate([-ones, ones])[:, None]
    tab = jax.ShapeDtypeStruct((B, QK_ROPE, S), F32)
    col = pl.BlockSpec((QK_ROPE, 1), lambda b, i: (0, 0))
    out = pl.BlockSpec((1, QK_ROPE, ts), lambda b, i: (b, 0, i))
    return pl.pallas_call(
        _rope_kernel,
        grid=(B, S // ts),
        in_specs=[pl.BlockSpec((1, 1, ts), lambda b, i: (b, 0, i)), col, col],
        out_specs=[out, out],
        out_shape=[tab, tab],
        compiler_params=_params(2),
        name="rope_tables",
    )(positions.reshape(B, 1, S), invf, sgn)


def _swiglu_rows(x, mod_ref, mod_base, g_ref, wgu_ref, wd_ref):
    shift = mod_ref[0, mod_base:mod_base + 1, :]
    scale = mod_ref[0, mod_base + 1:mod_base + 2, :]
    gate = mod_ref[0, mod_base + 2:mod_base + 3, :]
    h = (_rms(x, g_ref[...]) * (1.0 + scale) + shift).astype(BF16)
    if x.shape[0] >= 2 * MXU_TILE:
        y = None
        for c in range(D_FF // MXU_TILE):
            lo = c * MXU_TILE
            g = jnp.dot(h, wgu_ref[:, lo:lo + MXU_TILE], preferred_element_type=F32)
            u = jnp.dot(h, wgu_ref[:, D_FF + lo:D_FF + lo + MXU_TILE], preferred_element_type=F32)
            a = (_silu(g) * u).astype(BF16)
            yc = jnp.dot(a, wd_ref[lo:lo + MXU_TILE, :], preferred_element_type=F32)
            y = yc if y is None else y + yc
    else:
        gu = jnp.dot(h, wgu_ref[...], preferred_element_type=F32)
        a = (_silu(gu[:, :D_FF]) * gu[:, D_FF:]).astype(BF16)
        y = jnp.dot(a, wd_ref[...], preferred_element_type=F32)
    return x + (0.5 * (1.0 + gate)) * y


def _merge_ffn_kernel(x_ref, mod_ref, attn_ref, conv_ref, ag_ref, wo_ref, g_ref, wgu_ref, wd_ref, *rest, final):
    o_ref = rest[-1]
    an = _rms(attn_ref[0], ag_ref[...]).astype(BF16)
    ym = jnp.dot(an, wo_ref[0:ATTN_W, :], preferred_element_type=F32)
    ym = ym + jnp.dot(conv_ref[0], wo_ref[ATTN_W:, :], preferred_element_type=F32)
    x = x_ref[0] + (1.0 + mod_ref[0, 5:6, :]) * ym
    out = _swiglu_rows(x, mod_ref, 6, g_ref, wgu_ref, wd_ref)
    if final:
        out = _rms(out, rest[0][...])
    o_ref[0] = out


def _merge_ffn(x, mod_l, attn, conv, ag, wo, g, wgu, wd, layer, tm, final_g=None):
    B, S, D = x.shape
    tok = lambda w: pl.BlockSpec((1, tm, w), lambda b, i: (b, i, 0))
    wspec = lambda r, c: _resident((None, r, c), lambda b, i: (layer, 0, 0))
    in_specs = [tok(D), _mod_spec(layer), tok(ATTN_W), tok(CONV_W), _row_spec(ATTN_W, layer),
                wspec(ATTN_W + CONV_W, D), _row_spec(D, layer), wspec(D, 2 * D_FF), wspec(D_FF, D)]
    args = [x, mod_l, attn, conv, ag, wo, g, wgu, wd]
    if final_g is not None:
        in_specs.append(pl.BlockSpec((1, D), lambda b, i: (0, 0)))
        args.append(final_g)
    return pl.pallas_call(
        functools.partial(_merge_ffn_kernel, final=final_g is not None),
        grid=(B, S // tm),
        in_specs=in_specs,
        out_specs=tok(D),
        out_shape=jax.ShapeDtypeStruct((B, S, D), F32),
        compiler_params=_params(2),
        name="merge_ffn_final" if final_g is not None else "merge_ffn",
    )(*args)


def _rope_t(rows, cos, sin):
    partner = jnp.concatenate([rows[HALF_ROPE:], rows[:HALF_ROPE]], axis=0)
    return rows * cos + partner * sin


def _ffn_proj_kernel(x_ref, mod_ref, g1_ref, wgu_ref, wd_ref,
                     g_ref, win_ref, wkpe_ref, qg_ref, wuq_ref, kvg_ref, wk_ref, wv_ref,
                     cw_ref, cg_ref, cos_ref, sin_ref,
                     x1_ref, qt_ref, k_ref, vt_ref, conv_ref, hist_ref, *, tm):
    @pl.when(pl.program_id(1) == 0)
    def _():
        hist_ref[0:HALO, :] = jnp.zeros((HALO, CONV_W), F32)

    for r in range(FFN_ROW_GROUPS):
        rows = slice(r * tm // FFN_ROW_GROUPS, (r + 1) * tm // FFN_ROW_GROUPS)
        x1_ref[0, rows, :] = _swiglu_rows(x_ref[0, rows, :], mod_ref, 0, g1_ref, wgu_ref, wd_ref)

    x = x1_ref[0]
    shift = mod_ref[0, 3:4, :]
    scale = mod_ref[0, 4:5, :]
    h = (_rms(x, g_ref[...]) * (1.0 + scale) + shift).astype(BF16)
    proj = jnp.dot(h, win_ref[:, 0:P_GB], preferred_element_type=F32)
    cos = cos_ref[0]
    sin = sin_ref[0]
    rope_lo, rope_hi = QK_NOPE, QK_NOPE + QK_ROPE

    cqn = _rms(proj[:, P_CQ:P_CQ + Q_RANK], qg_ref[...]).astype(BF16)
    qt = lax.dot_general(wuq_ref[...], cqn, NT_DIMS, preferred_element_type=F32)
    gates = jnp.dot(h, win_ref[:, P_GB:P_WIDTH], preferred_element_type=F32)
    ckvn = _rms(proj[:, P_CKV:P_CKV + KV_RANK], kvg_ref[...]).astype(BF16)
    kk = jnp.dot(ckvn, wk_ref[...], preferred_element_type=F32)
    vt = lax.dot_general(wv_ref[...], ckvn, NT_DIMS, preferred_element_type=F32)
    kpe_t = lax.dot_general(wkpe_ref[...], h, NT_DIMS, preferred_element_type=F32)
    kpe_t = jnp.concatenate([jnp.zeros((rope_lo, tm), F32), _rope_t(kpe_t, cos, sin),
                             jnp.zeros((HEAD_SLAB - rope_hi, tm), F32)], axis=0)
    kpe = kpe_t.T
    for hd in range(N_HEADS):
        lo = hd * HEAD_SLAB
        qh = qt[lo:lo + HEAD_SLAB]
        qh = jnp.concatenate([qh[:rope_lo], _rope_t(qh[rope_lo:rope_hi], cos, sin), qh[rope_hi:]], axis=0)
        qh = (qh * Q_SCALE).astype(BF16)
        vh = vt[hd * V_DIM:(hd + 1) * V_DIM].astype(BF16)
        for j in range(tm // MXU_TILE):
            qt_ref[0, hd, j] = qh[:, j * MXU_TILE:(j + 1) * MXU_TILE]
            vt_ref[0, hd, j] = vh[:, j * MXU_TILE:(j + 1) * MXU_TILE]
        k_ref[0, hd] = (kk[:, lo:lo + HEAD_SLAB] + kpe).astype(BF16)

    cv =gates[:, CONV_W:2 * CONV_W] * gates[:, 2 * CONV_W:3 * CONV_W]
    hist_ref[HALO:HALO + tm, :] = cv
    cv1 = hist_ref[HALO - 1:HALO - 1 + tm, :]
    cv2 = hist_ref[HALO - 2:HALO - 2 + tm, :]
    conv = cw_ref[0:1, :] * cv2 + cw_ref[1:2, :] * cv1 + cw_ref[2:3, :] * cv
    conv = gates[:, 0:CONV_W] * conv
    conv_ref[0] = _rms(conv, cg_ref[...]).astype(BF16)
    hist_ref[0:HALO, :] = cv[tm - HALO:, :]


def _ffn_proj(x, mod_l, g1, wgu, wd, g, win, wkpe, qg, wuq, kvg, wk, wv, cw, cg, cos, sin, layer, tm):
    B, S, D = x.shape
    tok = lambda w: pl.BlockSpec((1, tm, w), lambda b, i: (b, i, 0))
    tok_t = lambda r: pl.BlockSpec((1, r, tm), lambda b, i: (b, 0, i))
    row = lambda w: _row_spec(w, layer)
    wspec = lambda r, c: _resident((None, r, c), lambda b, i: (layer, 0, 0))
    return pl.pallas_call(
        functools.partial(_ffn_proj_kernel, tm=tm),
        grid=(B, S // tm),
        in_specs=[tok(D),
                  _mod_spec(layer),
                  row(D), wspec(D, 2 * D_FF), wspec(D_FF, D),
                  row(D), wspec(D, P_WIDTH), wspec(QK_ROPE, D),
                  row(Q_RANK), wspec(N_HEADS * HEAD_SLAB, Q_RANK),
                  row(KV_RANK), wspec(KV_RANK, N_HEADS * HEAD_SLAB), wspec(ATTN_W, KV_RANK),
                  pl.BlockSpec((None, 3, CONV_W), lambda b, i: (layer, 0, 0)),
                  row(CONV_W), tok_t(QK_ROPE), tok_t(QK_ROPE)],
        out_specs=[tok(D),
                   pl.BlockSpec((1, N_HEADS, tm // MXU_TILE, HEAD_SLAB, MXU_TILE), lambda b, i: (b, 0, i, 0, 0)),
                   pl.BlockSpec((1, N_HEADS, tm, HEAD_SLAB), lambda b, i: (b, 0, i, 0)),
                   pl.BlockSpec((1, N_HEADS, tm // MXU_TILE, V_DIM, MXU_TILE), lambda b, i: (b, 0, i, 0, 0)),
                   tok(CONV_W)],
        out_shape=[jax.ShapeDtypeStruct((B, S, D), F32),
                   jax.ShapeDtypeStruct((B, N_HEADS, S // MXU_TILE, HEAD_SLAB, MXU_TILE), BF16),
                   jax.ShapeDtypeStruct((B, N_HEADS, S, HEAD_SLAB), BF16),
                   jax.ShapeDtypeStruct((B, N_HEADS, S // MXU_TILE, V_DIM, MXU_TILE), BF16),
                   jax.ShapeDtypeStruct((B, S, CONV_W), BF16)],
        scratch_shapes=[pltpu.VMEM((HALO + tm, CONV_W), F32)],
        compiler_params=_params(2),
        name="ffn_proj",
    )(x, mod_l, g1, wgu, wd, g, win, wkpe, qg, wuq, kvg, wk, wv, cw, cg, cos, sin)


def _attn_kernel(qt_ref, k_ref, vt_ref, o_ref, s_ref, acc_ref, cm_ref, *, tq, n_tiles):
    g = pl.program_id(2)
    hb = MXU_TILE
    krow = lax.broadcasted_iota(jnp.int32, (hb, hb), 0)
    qcol = lax.broadcasted_iota(jnp.int32, (hb, hb), 1)
    tri = krow <= qcol
    zpad = jnp.zeros((hb, hb - HEAD_SLAB), BF16)
    zrows = jnp.zeros((hb - HEAD_SLAB, hb), BF16)
    ones = jnp.ones((VALUE_ROWS - V_DIM, hb), BF16)
    heads = range(HEADS_PER_STEP)

    def q_half(h, tile, qh):
        return jnp.concatenate([qt_ref[0, h, tile * (tq // hb) + qh], zrows], axis=0)

    def k_half(h, blk, kh):
        start = pl.multiple_of(blk * tq + kh * hb, hb)
        return jnp.concatenate([k_ref[0, h, pl.ds(start, hb), :], zpad], axis=1)

    def v_half(h, blk, kh):
        return jnp.concatenate([vt_ref[0, h, blk * (tq // hb) + kh], ones], axis=0)

    def quad_ref(h, kh, qh):
        return s_ref.at[h, 2 * kh + qh]

    def prestage(h, tile):
        pltpu.matmul_push_rhs(q_half(h, tile, 0), 1, h)

    def drain_stage(h):
        pltpu.matmul_acc_lhs(SCORE_SLOTS[1], jnp.zeros((16, hb), BF16), h, 1)
        pltpu.matmul_pop(SCORE_SLOTS[1], (16, hb), F32, h)

    def issue3(h, tile, blk):
        pltpu.matmul_acc_lhs(SCORE_SLOTS[0], k_half(h, blk, 0), h, 1)
        pltpu.matmul_acc_lhs(SCORE_SLOTS[1], k_half(h, blk, 1), h, None)
        pltpu.matmul_push_rhs(q_half(h, tile, 1), 0, h)
        pltpu.matmul_acc_lhs(SCORE_SLOTS[2], k_half(h, blk, 0), h, 0)

    def issue4(h, blk):
        pltpu.matmul_acc_lhs(SCORE_SLOTS[0], k_half(h, blk, 1), h, None)

    def pop_quad(h, slot, kh, qh, diag):
        st = pltpu.matmul_pop(slot, (hb, hb), F32, h)
        if diag:
            if kh > qh:
                st = jnp.full((hb, hb), -jnp.inf, F32)
            elif kh == qh:
                st = jnp.where(tri, st, -jnp.inf)
        quad_ref(h, kh, qh)[...] = st
        return jnp.max(st, axis=0, keepdims=True)

    def numer(h, kh, qh, m_new, reg):
        mq = m_new[:, qh * hb:(qh + 1) * hb]
        p = jnp.exp2(quad_ref(h, kh, qh)[...] - mq).astype(BF16)
        pltpu.matmul_push_rhs(p, reg, h)

    def pv(h, blk, qh, between=None):
        pltpu.matmul_acc_lhs(OUT_SLOTS[qh], v_half(h, blk, 0), h, 1)
        if between is not None:
            between()
        pltpu.matmul_acc_lhs(OUT_SLOTS[qh], v_half(h, blk, 1), h, 0)

    def accumulate(h, alpha):
        out = jnp.concatenate([pltpu.matmul_pop(OUT_SLOTS[qh], (VALUE_ROWS, hb), F32, h) for qh in range(2)], axis=1)
        acc_ref[h] = alpha * acc_ref[h] + out

    def col_max(c):
        return jnp.concatenate([jnp.maximum(c[0], c[1]), jnp.maximum(c[2], c[3])], axis=1)

    def order(i, t):
        return jnp.where(t == 0, i, t - 1)

    @pl.when(g == 0)
    def _():
        for h in heads:
            drain_stage(h)
            prestage(h, 0)
            issue3(h, 0, 0)
        for h in heads:
            c = [pop_quad(h, SCORE_SLOTS[0], 0, 0, True)]
            issue4(h, 0)
            c.append(pop_quad(h, SCORE_SLOTS[1], 1, 0, True))
            c.append(pop_quad(h, SCORE_SLOTS[2], 0, 1, True))
            c.append(pop_quad(h, SCORE_SLOTS[0], 1, 1, True))
            cm_ref[h] = col_max(c)
            prestage(h, min(1, n_tiles - 1))

    def step(blk, state, nxt, staged_tile=None):
        ms, alphas, cs = [], [], [[] for _ in heads]
        for h in heads:
            m, cm, _ = state[h]
            m_new = jnp.maximum(m, cm)
            ms.append(m_new)
            alphas.append(jnp.exp2(m - m_new))
        for h in heads:
            if nxt is not None:
                issue3(h, nxt[0], nxt[1])
            else:
                drain_stage(h)
        for h in heads:
            numer(h, 0, 0, ms[h], 1)
        for h in heads:
            if nxt is not None:
                cs[h].append(pop_quad(h, SCORE_SLOTS[0], 0, 0, nxt[2]))
                issue4(h, nxt[1])
        for h in heads:
            numer(h, 1, 0, ms[h], 0)
        if nxt is not None:
            for h in heads:
                cs[h].append(pop_quad(h, SCORE_SLOTS[1], 1, 0, nxt[2]))
        for h in heads:
            accumulate(h, state[h][2])
            pv(h, blk, 0)
        for h in heads:
            numer(h, 0, 1, ms[h], 1)
            numer(h, 1, 1, ms[h], 0)
            pv(h, blk, 1, (lambda h=h: prestage(h, nxt[0] if staged_tile is None else staged_tile))
               if nxt is not None else None)
        out = []
        for h in heads:
            if nxt is not None:
                cs[h].append(pop_quad(h, SCORE_SLOTS[2], 0, 1, nxt[2]))
                cs[h].append(pop_quad(h, SCORE_SLOTS[0], 1, 1, nxt[2]))
                out.append((ms[h], col_max(cs[h]), alphas[h]))
            else:
                out.append((ms[h], state[h][1], alphas[h]))
        return tuple(out)

    def run_tile(i, sub):
        for h in heads:
            acc_ref[h] = jnp.zeros((VALUE_ROWS, tq), F32)
        state = tuple((jnp.full((1, tq), -jnp.inf, F32), cm_ref[h], jnp.ones((1, tq), F32)) for h in heads)
        next_tile = jnp.minimum(i + 1, n_tiles - 1)
        state = lax.fori_loop(
            0, i,
            lambda t, st: step(order(i, t), st, (i, t, False), jnp.where(t == i - 1, next_tile, i)),
            state)

        def finish(nxt):
            last = step(order(i, i), state, nxt)
            outs = []
            for h in heads:
                if nxt is not None:
                    cm_ref[h] = last[h][1]
                accumulate(h, last[h][2])
                acc = acc_ref[h]
                outs.append(acc[0:V_DIM, :] / acc[V_DIM:V_DIM + 1, :])
            o_ref[0, sub * tq:(sub + 1) * tq, :] = jnp.concatenate(outs, axis=0).T

        if sub < TILES_PER_STEP - 1:
            finish((i + 1, i + 1, True))
        else:
            @pl.when(i < n_tiles - 1)
            def _():
                finish((i + 1, i + 1, True))

            @pl.when(i == n_tiles - 1)
            def _():
                finish(None)

    for sub in range(TILES_PER_STEP):
        run_tile(g * TILES_PER_STEP + sub, sub)


def _attention(qt, k, vt, tq):
    B, H, S, _ = k.shape
    hps = HEADS_PER_STEP
    assert tq == 2 * MXU_TILE and HEAD_SLAB <= MXU_TILE and S % (tq * TILES_PER_STEP) == 0
    return pl.pallas_call(
        functools.partial(_attn_kernel, tq=tq, n_tiles=S // tq),
        grid=(B, H // hps, S // (tq * TILES_PER_STEP)),
        in_specs=[pl.BlockSpec((1, hps, S // MXU_TILE, HEAD_SLAB, MXU_TILE), lambda b, p, i: (b, p, 0, 0, 0)),
                  pl.BlockSpec((1, hps, S, HEAD_SLAB), lambda b, p, i: (b, p, 0, 0)),
                  pl.BlockSpec((1, hps, S // MXU_TILE, V_DIM, MXU_TILE), lambda b, p, i: (b, p, 0, 0, 0))],
        out_specs=pl.BlockSpec((1, tq * TILES_PER_STEP, hps * V_DIM), lambda b, p, i: (b, i, p)),
        out_shape=jax.ShapeDtypeStruct((B, S, ATTN_W), F32),
        scratch_shapes=[pltpu.VMEM((hps, 4, MXU_TILE, MXU_TILE), F32),
                        pltpu.VMEM((hps, VALUE_ROWS, tq), F32),
                        pltpu.VMEM((hps, 1, tq), F32)],
        compiler_params=_params(3),
        name="mla_attention",
    )(qt, k, vt)


def _layout_w_in(w_in):
    lat = Q_RANK + KV_RANK
    main = jnp.concatenate([w_in[:, :, :lat], w_in[:, :, lat + QK_ROPE:]], axis=-1)
    kpe_t = jnp.swapaxes(w_in[:, :, lat:lat + QK_ROPE], 1, 2)
    return main.astype(BF16), kpe_t.astype(BF16)


def _layout_w_uq(w_uq):
    L, R, _ = w_uq.shape
    w = w_uq.reshape(L, R, N_HEADS, QK_NOPE + QK_ROPE)
    z_pad = jnp.zeros((L, R, N_HEADS, HEAD_SLAB - QK_NOPE - QK_ROPE), w.dtype)
    q = jnp.concatenate([w, z_pad], axis=-1).reshape(L, R, N_HEADS * HEAD_SLAB)
    return jnp.swapaxes(q, 1, 2).astype(BF16)


def _layout_w_ukv(w_ukv):
    L, R, _ = w_ukv.shape
    w = w_ukv.reshape(L, R, N_HEADS, QK_NOPE + V_DIM)
    k_nope, v = w[..., :QK_NOPE], w[..., QK_NOPE:]
    z = jnp.zeros((L, R, N_HEADS, HEAD_SLAB - QK_NOPE), w.dtype)
    wk = jnp.concatenate([k_nope, z], axis=-1).reshape(L, R, N_HEADS * HEAD_SLAB)
    wv_t = jnp.swapaxes(v.reshape(L, R, ATTN_W), 1, 2)
    return wk.astype(BF16), wv_t.astype(BF16)


def kernel(x, c, positions, w_ada, b_ada, ffn1_norm, ffn1_w_gu, ffn1_w_down, mix_norm, w_in, q_a_norm, w_uq, kv_a_norm, w_ukv, conv_w, attn_out_norm, conv_out_norm, w_o, ffn2_norm, ffn2_w_gu, ffn2_w_down, final_norm):
    B, S, D = x.shape
    L = w_ada.shape[0]
    tm = min(S, 512)
    tq = min(S, 512)

    mod = _modulation(c, w_ada, b_ada)
    cos, sin = _rope_tables(positions)

    wgu1, wd1 = ffn1_w_gu.astype(BF16), ffn1_w_down.astype(BF16)
    wgu2, wd2 = ffn2_w_gu.astype(BF16), ffn2_w_down.astype(BF16)
    win, wkpe_t = _layout_w_in(w_in)
    wuq_t = _layout_w_uq(w_uq)
    wk, wv_t = _layout_w_ukv(w_ukv)
    wo = w_o.astype(BF16)
    final_g = final_norm.reshape(1, D)
    rows = lambda g: g.reshape(L, 1, g.shape[-1])
    g1, g2, gm = rows(ffn1_norm), rows(ffn2_norm), rows(mix_norm)
    gq, gkv = rows(q_a_norm), rows(kv_a_norm)
    ga, gc = rows(attn_out_norm), rows(conv_out_norm)

    for l in range(L):
        x, qt, k, vt, conv = _ffn_proj(x, mod, g1, wgu1, wd1, gm, win, wkpe_t, gq, wuq_t, gkv, wk, wv_t,
                                       conv_w, gc, cos, sin, l, tm)
        attn = _attention(qt, k, vt, tq)
        x = _merge_ffn(x, mod, attn, conv, ga, wo, g2, wgu2, wd2, l, tm,
                       final_g=final_g if l == L - 1 else None)
    return x
```
